```python
import jax, jax.numpy as jnp
from jax import lax
import numpy as np

D_MODEL = 1024
BATCH = 4
SEQ = 4096
DEPTH = 2

N_EVEN = (DEPTH + 1) // 2
N_ODD = DEPTH // 2
EPS = 1e-6

D_FF = 2816

N_MEM = 256
XA_HEADS = 4
XA_HEAD_DIM = D_MODEL // XA_HEADS

A_HEADS = 8
A_HEAD_DIM = 64
A_WIDTH = A_HEADS * A_HEAD_DIM
KV_RANK = 256
IDX_HEADS = 8
IDX_DIM = 64
TOPK_MAX = 256
Q_BLOCK = 128
B_WIDTH = D_MODEL - A_WIDTH
B_BLOCKS = 8
B_BLOCK_DIM = B_WIDTH // B_BLOCKS
CONV_W = 4
LRU_C = 8.0
EVEN_SPLITS = (A_WIDTH, KV_RANK, IDX_HEADS * IDX_DIM, IDX_DIM, IDX_HEADS, B_WIDTH, B_WIDTH)
EVEN_IN = sum(EVEN_SPLITS)

GLA_HEADS = 4
GLA_DK = D_MODEL // 2
GLA_DV = D_MODEL
GLA_DKH = GLA_DK // GLA_HEADS
GLA_DVH = GLA_DV // GLA_HEADS
GLA_GATE_RANK = 16
GLA_TAU = 16.0
GLA_CHUNK = 64
ODD_SPLITS = (GLA_DK, GLA_DK, GLA_DV, GLA_GATE_RANK, GLA_DV)
ODD_IN = sum(ODD_SPLITS)

N_FFN1_PRE, N_FFN1_POST, N_MIX_PRE, N_MIX_POST, N_XA_PRE, N_XA_POST, N_MEM_NORM, N_FFN2_PRE, N_FFN2_POST = range(9)
N_NORMS = 9

kernel_name = "hybrid_dsa_rglru_gla_macaron_block"


def _split(t, sizes):
    return jnp.split(t, np.cumsum(sizes)[:-1].tolist(), axis=-1)


def rmsnorm(x, g):
    xf = x.astype(jnp.float32)
    y = xf * lax.rsqrt(jnp.mean(xf * xf, axis=-1, keepdims=True) + EPS)
    return (y * g.astype(jnp.float32)).astype(x.dtype)


def swiglu(x, w_gu, w_down):
    g, u = jnp.split(x @ w_gu, 2, axis=-1)
    return (jax.nn.silu(g) * u) @ w_down


def causal_dwconv(x, w, b):
    y = lax.conv_general_dilated(x, w[:, None, :].astype(x.dtype), window_strides=(1,),
                                 padding=[(CONV_W - 1, 0)], dimension_numbers=('NWC', 'WIO', 'NWC'),
                                 feature_group_count=x.shape[-1])
    return y + b


def rg_lru(xc, w_ra, b_ra, w_ri, b_ri, lam):
    Bb, S, _ = xc.shape
    xb = xc.reshape(Bb, S, B_BLOCKS, B_BLOCK_DIM)
    r = jax.nn.sigmoid((jnp.einsum('bsgi,gij->bsgj', xb, w_ra) + b_ra).astype(jnp.float32)).reshape(Bb, S, B_WIDTH)
    i = jax.nn.sigmoid((jnp.einsum('bsgi,gij->bsgj', xb, w_ri) + b_ri).astype(jnp.float32)).reshape(Bb, S, B_WIDTH)
    log_a = -LRU_C * r * jax.nn.softplus(-lam.astype(jnp.float32))
    a = jnp.exp(log_a)
    b = jnp.sqrt(-jnp.expm1(2.0 * log_a)) * (i * xc.astype(jnp.float32))

    def combine(left, right):
        a1, b1 = left
        a2, b2 = right
        return a1 * a2, a2 * b1 + b2

    _, h = lax.associative_scan(combine, (a, b), axis=1)
    return h.astype(xc.dtype)


def dsa_attention(q, k, v, q_idx, k_idx, w_idx):
    Bb, S = q.shape[0], q.shape[1]
    n_blk = S // Q_BLOCK
    top_k = min(TOPK_MAX, S // 4)
    key_pos = jnp.arange(S)

    def block(args):
        qb, qib, wb, start = args
        t = start + jnp.arange(Q_BLOCK)
        sc = jnp.einsum('bqhd,bsd->bqhs', qib, k_idx, preferred_element_type=jnp.float32) * (IDX_DIM ** -0.5)
        score = jnp.einsum('bqhs,bqh->bqs', jax.nn.relu(sc), wb.astype(jnp.float32) * (IDX_HEADS ** -0.5))
        causal = key_pos[None, :] <= t[:, None]
        score = jnp.where(causal[None], score, -jnp.inf)
        _, idx = lax.top_k(score, top_k)
        kg = jax.vmap(lambda kk, ii: kk[ii])(k, idx)
        vg = jax.vmap(lambda vv, ii: vv[ii])(v, idx)
        s = jnp.einsum('bqhd,bqkhd->bqhk', qb, kg, preferred_element_type=jnp.float32) * (A_HEAD_DIM ** -0.5)
        valid = (idx <= t[None, :, None])[:, :, None, :]
        p = jax.nn.softmax(jnp.where(valid, s, -jnp.inf), axis=-1)
        return jnp.einsum('bqhk,bqkhd->bqhd', p.astype(vg.dtype), vg)

    def to_blocks(a):
        return a.reshape(Bb, n_blk, Q_BLOCK, *a.shape[2:]).swapaxes(0, 1)

    starts = jnp.arange(n_blk) * Q_BLOCK
    out = lax.map(block, (to_blocks(q), to_blocks(q_idx), to_blocks(w_idx), starts))
    return out.swapaxes(0, 1).reshape(q.shape)


def even_mixer(h, w_in, kv_norm, w_uk, w_uv, conv_w, conv_b, w_ra, b_ra, w_ri, b_ri, lam, w_out):
    Bb, S, _ = h.shape
    q, c_kv, q_i, k_i, w_i, gate_b, x_b = _split(h @ w_in, EVEN_SPLITS)
    c_kv = rmsnorm(c_kv, kv_norm)
    k = (c_kv @ w_uk).reshape(Bb, S, A_HEADS, A_HEAD_DIM)
    v = (c_kv @ w_uv).reshape(Bb, S, A_HEADS, A_HEAD_DIM)
    a_out = dsa_attention(q.reshape(Bb, S, A_HEADS, A_HEAD_DIM), k, v,
                          q_i.reshape(Bb, S, IDX_HEADS, IDX_DIM), k_i, w_i).reshape(Bb, S, A_WIDTH)
    xc = causal_dwconv(x_b, conv_w, conv_b)
    b_out = rg_lru(xc, w_ra, b_ra, w_ri, b_ri, lam) * jax.nn.gelu(gate_b)
    return jnp.concatenate([a_out, b_out], axis=-1) @ w_out


def gla_mixer(h, w_in, w_g2, b_g, head_norm, w_out):
    Bb, S, _ = h.shape
    nc = S // GLA_CHUNK
    q, k, v, g_lr, r = _split(h @ w_in, ODD_SPLITS)
    gk = jax.nn.log_sigmoid((g_lr @ w_g2 + b_g).astype(jnp.float32)) / GLA_TAU

    def chunks(a, dh):
        return a.astype(jnp.float32).reshape(Bb, nc, GLA_CHUNK, GLA_HEADS, dh).transpose(0, 1, 3, 2, 4)

    q = chunks(q, GLA_DKH) * (GLA_DKH ** -0.5)
    k = chunks(k, GLA_DKH)
    v = chunks(v, GLA_DVH)
    G = jnp.cumsum(chunks(gk, GLA_DKH), axis=3)
    G_last = G[:, :, :, -1:, :]
    q_dec = q * jnp.exp(G)
    k_inv = k * jnp.exp(-G)
    k_rem = k * jnp.exp(G_last - G)
    tril = jnp.tril(jnp.ones((GLA_CHUNK, GLA_CHUNK), dtype=bool))
    att = jnp.where(tril, jnp.einsum('bnhid,bnhjd->bnhij', q_dec, k_inv), 0.0)
    o_intra = jnp.einsum('bnhij,bnhjv->bnhiv', att, v)

    def step(state, xs):
        qd, kr, vv, dec = xs
        o = jnp.einsum('bhid,bhdv->bhiv', qd, state)
        state = dec[..., None] * state + jnp.einsum('bhjd,bhjv->bhdv', kr, vv)
        return state, o

    s0 = jnp.zeros((Bb, GLA_HEADS, GLA_DKH, GLA_DVH), jnp.float32)
    xs = (q_dec.swapaxes(0, 1), k_rem.swapaxes(0, 1), v.swapaxes(0, 1), jnp.exp(G_last[:, :, :, 0, :]).swapaxes(0, 1))
    _, o_inter = lax.scan(step, s0, xs)
    o = o_intra + o_inter.swapaxes(0, 1)
    o = o.transpose(0, 1, 3, 2, 4).reshape(Bb, S, GLA_HEADS, GLA_DVH)
    o = rmsnorm(o, head_norm).reshape(Bb, S, GLA_DV).astype(h.dtype) * jax.nn.silu(r)
    return o @ w_out


def mem_cross_attention(h, mem_n, w_q, w_kv, w_o):
    Bb, S, _ = h.shape
    M = mem_n.shape[1]
    q = (h @ w_q).reshape(Bb, S, XA_HEADS, XA_HEAD_DIM)
    k, v = jnp.split(mem_n @ w_kv, 2, axis=-1)
    k = k.reshape(Bb, M, XA_HEADS, XA_HEAD_DIM)
    v = v.reshape(Bb, M, XA_HEADS, XA_HEAD_DIM)
    s = jnp.einsum('bshd,bmhd->bhsm', q, k, preferred_element_type=jnp.float32) * (XA_HEAD_DIM ** -0.5)
    p = jax.nn.softmax(s, axis=-1)
    o = jnp.einsum('bhsm,bmhd->bshd', p.astype(v.dtype), v).reshape(Bb, S, D_MODEL)
    return o @ w_o


def setup_inputs(seed: int = 0) -> dict:
    key = jax.random.key(seed)
    ks = iter(jax.random.split(key, 32))
    f32 = jnp.float32

    def nrm(shape, fan_in):
        return jax.random.normal(next(ks), shape, f32) * (fan_in ** -0.5)

    def small(shape, scale=0.01):
        return jax.random.normal(next(ks), shape, f32) * scale

    def gain(shape):
        return 1.0 + jax.random.normal(next(ks), shape, f32) * 0.02

    u = jax.random.uniform(next(ks), (N_EVEN, B_WIDTH), f32, 0.9, 0.999)
    a_base = u ** (1.0 / LRU_C)
    lam = jnp.log(a_base) - jnp.log1p(-a_base)

    return {
        "x": jax.random.normal(next(ks), (BATCH, SEQ, D_MODEL), f32),
        "mem": jax.random.normal(next(ks), (BATCH, N_MEM, D_MODEL), f32),
        "norms": gain((DEPTH, N_NORMS, D_MODEL)),
        "ffn_w_gu": nrm((DEPTH, 2, D_MODEL, 2 * D_FF), D_MODEL),
        "ffn_w_down": nrm((DEPTH, 2, D_FF, D_MODEL), D_FF),
        "xa_w_q": nrm((DEPTH, D_MODEL, D_MODEL), D_MODEL),
        "xa_w_kv": nrm((DEPTH, D_MODEL, 2 * D_MODEL), D_MODEL),
        "xa_w_o": nrm((DEPTH, D_MODEL, D_MODEL), D_MODEL),
        "ev_w_in": nrm((N_EVEN, D_MODEL, EVEN_IN), D_MODEL),
        "ev_kv_norm": gain((N_EVEN, KV_RANK)),
        "ev_w_uk": nrm((N_EVEN, KV_RANK, A_WIDTH), KV_RANK),
        "ev_w_uv": nrm((N_EVEN, KV_RANK, A_WIDTH), KV_RANK),
        "ev_conv_w": nrm((N_EVEN, CONV_W, B_WIDTH), CONV_W),
        "ev_conv_b": small((N_EVEN, B_WIDTH)),
        "ev_w_ra": nrm((N_EVEN, B_BLOCKS, B_BLOCK_DIM, B_BLOCK_DIM), B_BLOCK_DIM),
        "ev_b_ra": small((N_EVEN, B_BLOCKS, B_BLOCK_DIM)),
        "ev_w_ri": nrm((N_EVEN, B_BLOCKS, B_BLOCK_DIM, B_BLOCK_DIM), B_BLOCK_DIM),
        "ev_b_ri": small((N_EVEN, B_BLOCKS, B_BLOCK_DIM)),
        "ev_lam": lam,
        "ev_w_out": nrm((N_EVEN, D_MODEL, D_MODEL), D_MODEL),
        "od_w_in": nrm((N_ODD, D_MODEL, ODD_IN), D_MODEL),
        "od_w_g2": nrm((N_ODD, GLA_GATE_RANK, GLA_DK), GLA_GATE_RANK),
        "od_b_g": small((N_ODD, GLA_DK), 0.1),
        "od_head_norm": gain((N_ODD, GLA_DVH)),
        "od_w_out": nrm((N_ODD, GLA_DV, D_MODEL), GLA_DV),
    }


def reference(x, mem, norms, ffn_w_gu, ffn_w_down, xa_w_q, xa_w_kv, xa_w_o,
              ev_w_in, ev_kv_norm, ev_w_uk, ev_w_uv, ev_conv_w, ev_conv_b, ev_w_ra, ev_b_ra,
              ev_w_ri, ev_b_ri, ev_lam, ev_w_out,
              od_w_in, od_w_g2, od_b_g, od_head_norm, od_w_out):
    h = x
    for layer in range(DEPTH):
        g = norms[layer]
        f = swiglu(rmsnorm(h, g[N_FFN1_PRE]), ffn_w_gu[layer, 0], ffn_w_down[layer, 0])
        h = h + 0.5 * rmsnorm(f, g[N_FFN1_POST])
        hn = rmsnorm(h, g[N_MIX_PRE])
        if layer % 2 == 0:
            e = layer // 2
            m = even_mixer(hn, ev_w_in[e], ev_kv_norm[e], ev_w_uk[e], ev_w_uv[e], ev_conv_w[e], ev_conv_b[e],
                           ev_w_ra[e], ev_b_ra[e], ev_w_ri[e], ev_b_ri[e], ev_lam[e], ev_w_out[e])
        else:
            o = layer // 2
            m = gla_mixer(hn, od_w_in[o], od_w_g2[o], od_b_g[o], od_head_norm[o], od_w_out[o])
        h = h + rmsnorm(m, g[N_MIX_POST])
        mem_n = rmsnorm(mem, g[N_MEM_NORM])
        c = mem_cross_attention(rmsnorm(h, g[N_XA_PRE]), mem_n, xa_w_q[layer], xa_w_kv[layer], xa_w_o[layer])
        h = h + rmsnorm(c, g[N_XA_POST])
        f = swiglu(rmsnorm(h, g[N_FFN2_PRE]), ffn_w_gu[layer, 1], ffn_w_down[layer, 1])
        h = h + 0.5 * rmsnorm(f, g[N_FFN2_POST])
    return h
```

```python
import functools

import jax
import jax.numpy as jnp
from jax import lax
from jax.experimental import pallas as pl
from jax.experimental.pallas import tpu as pltpu

F32 = jnp.float32
BF16 = jnp.bfloat16

EPS = 1e-6
D_MODEL = 1024
D_FF = 2816
XA_HEADS = 4
XA_HEAD_DIM = D_MODEL // XA_HEADS
A_HEADS = 8
A_HEAD_DIM = 64
A_WIDTH = A_HEADS * A_HEAD_DIM
KV_RANK = 256
IDX_HEADS = 8
IDX_DIM = 64
TOPK_MAX = 256
B_WIDTH = D_MODEL - A_WIDTH
B_BLOCKS = 8
B_BLOCK_DIM = B_WIDTH // B_BLOCKS
CONV_W = 4
LRU_C = 8.0
GLA_HEADS = 4
GLA_DK = D_MODEL // 2
GLA_DV = D_MODEL
GLA_DKH = GLA_DK // GLA_HEADS
GLA_DVH = GLA_DV // GLA_HEADS
GLA_GATE_RANK = 16
GLA_TAU = 16.0
GLA_CHUNK = 64
(N_FFN1_PRE, N_FFN1_POST, N_MIX_PRE, N_MIX_POST, N_XA_PRE, N_XA_POST, N_MEM_NORM,
 N_FFN2_PRE, N_FFN2_POST) = range(9)

LANES = 128
SUBLANES = 8
VMEM_LIMIT = 48 * 1024 * 1024

NEG_BIG = -1e30
INT_MIN = -2 ** 31
POS_INF_CODE = 0x7F800000
NEG_INF_CODE = -0x7F800001


def _cparams(sem):
    return pltpu.CompilerParams(dimension_semantics=sem, vmem_limit_bytes=VMEM_LIMIT)


def _rms(x, g):
    return x * lax.rsqrt(jnp.mean(x * x, axis=-1, keepdims=True) + EPS) * g


def _dot(a, b):
    return jnp.dot(a, b, preferred_element_type=F32)


def _dot_nt(a, b):
    return lax.dot_general(a, b, (((1,), (1,)), ((), ())), preferred_element_type=F32)


def _dot_tn(a, b):
    return lax.dot_general(a, b, (((0,), (0,)), ((), ())), preferred_element_type=F32)


def _sigmoid(x):
    return 1.0 / (1.0 + jnp.exp(-x))


FFN_TM = 1024
FFN_TF = 256


def _ffn_kernel(h_ref, gpre_ref, gpost_ref, wg_ref, wu_ref, wd_ref, o_ref, xn_ref, acc_ref):
    j = pl.program_id(1)

    @pl.when(j == 0)
    def _():
        xn_ref[...] = _rms(h_ref[...], gpre_ref[...]).astype(BF16)

    xn = xn_ref[...]
    g = _dot(xn, wg_ref[...])
    u = _dot(xn, wu_ref[...])
    act = (g * _sigmoid(g) * u).astype(BF16)
    part = _dot(act, wd_ref[...])

    @pl.when(j == 0)
    def _():
        acc_ref[...] = part

    @pl.when(j > 0)
    def _():
        acc_ref[...] += part

    @pl.when(j == pl.num_programs(1) - 1)
    def _():
        o_ref[...] = h_ref[...] + 0.5 * _rms(acc_ref[...], gpost_ref[...])


def _ffn(h, g_pre, g_post, w_gu, w_down):
    T, D = h.shape
    F = w_down.shape[0]
    tm, tf = min(FFN_TM, T), FFN_TF
    nf = F // tf
    return pl.pallas_call(
        _ffn_kernel,
        grid=(T // tm, nf),
        in_specs=[
            pl.BlockSpec((tm, D), lambda i, j: (i, 0)),
            pl.BlockSpec((1, D), lambda i, j: (0, 0)),
            pl.BlockSpec((1, D), lambda i, j: (0, 0)),
            pl.BlockSpec((D, tf), lambda i, j: (0, j)),
            pl.BlockSpec((D, tf), lambda i, j: (0, j + nf)),
            pl.BlockSpec((tf, D), lambda i, j: (j, 0)),
        ],
        out_specs=pl.BlockSpec((tm, D), lambda i, j: (i, 0)),
        out_shape=jax.ShapeDtypeStruct((T, D), F32),
        scratch_shapes=[pltpu.VMEM((tm, D), BF16), pltpu.VMEM((tm, D), F32)],
        compiler_params=_cparams(("parallel", "arbitrary")),
        name="ffn",
    )(h, g_pre, g_post, w_gu, w_gu, w_down)


PROJ_TM = 512


def _out_proj_kernel(*refs, offsets, scale):
    h_ref, g_ref = refs[0], refs[1]
    part_refs = refs[2:2 + len(offsets)]
    w_ref = refs[2 + len(offsets)]
    o_ref = refs[3 + len(offsets)]
    m = None
    for p_ref, off in zip(part_refs, offsets):
        kk = p_ref.shape[-1]
        term = _dot(p_ref[...], w_ref[off:off + kk, :])
        m = term if m is None else m + term
    o_ref[...] = h_ref[...] + scale * _rms(m, g_ref[...])


def _out_proj(h, g_post, parts, w_out, scale=1.0):
    T, D = h.shape
    tm = min(PROJ_TM, T)
    offsets, off = [], 0
    for p in parts:
        offsets.append(off)
        off += p.shape[-1]
    in_specs = [pl.BlockSpec((tm, D), lambda i: (i, 0)), pl.BlockSpec((1, D), lambda i: (0, 0))]
    in_specs += [pl.BlockSpec((tm, p.shape[-1]), lambda i: (i, 0)) for p in parts]
    in_specs += [pl.BlockSpec(w_out.shape, lambda i: (0, 0))]
    return pl.pallas_call(
        functools.partial(_out_proj_kernel, offsets=tuple(offsets), scale=scale),
        grid=(T // tm,),
        in_specs=in_specs,
        out_specs=pl.BlockSpec((tm, D), lambda i: (i, 0)),
        out_shape=jax.ShapeDtypeStruct((T, D), F32),
        compiler_params=_cparams(("parallel",)),
        name="out_proj",
    )(h, g_post, *parts, w_out)


EV_Q = (0, 512)
EV_CKV = (512, 768)
EV_QI = (768, 1280)
EV_KI2 = (1280, 1408)
EV_GATE = (1408, 1920)
EV_XB = (1920, 2432)
EV_WI = (2432, 2560)
EV_COLS = 2560


def _even_proj_kernel(h_ref, g_ref, w_ref, kvn_ref, wuk_ref, wuv_ref,
                      q_ref, k_ref, v_ref, qi_ref, ki_ref, wi_ref, gate_ref, xb_ref):
    xn = _rms(h_ref[...], g_ref[...]).astype(BF16)

    def seg(ab):
        return _dot(xn, w_ref[:, ab[0]:ab[1]])

    q_ref[...] = (seg(EV_Q) * (A_HEAD_DIM ** -0.5)).astype(BF16)
    ckv = _rms(seg(EV_CKV), kvn_ref[...]).astype(BF16)
    k_ref[...] = _dot(ckv, wuk_ref[...]).astype(BF16)
    v_ref[...] = _dot(ckv, wuv_ref[...]).astype(BF16)
    qi_ref[...] = (seg(EV_QI) * (IDX_DIM ** -0.5)).astype(BF16)
    ki_ref[...] = seg(EV_KI2).astype(BF16)
    wi_ref[...] = seg(EV_WI)[:, :IDX_HEADS] * (IDX_HEADS ** -0.5)
    gate_ref[...] = seg(EV_GATE)
    xb_ref[...] = seg(EV_XB)


def _even_proj(h, g_pre, w_in, kv_norm, w_uk, w_uv):
    T, D = h.shape
    tm = min(PROJ_TM, T)
    row = lambda i: (i, 0)
    fixed = lambda i: (0, 0)
    outs = [
        (A_WIDTH, BF16), (A_WIDTH, BF16), (A_WIDTH, BF16), (IDX_HEADS * IDX_DIM, BF16),
        (2 * IDX_DIM, BF16), (IDX_HEADS, F32), (B_WIDTH, F32), (B_WIDTH, F32),
    ]
    return pl.pallas_call(
        _even_proj_kernel,
        grid=(T // tm,),
        in_specs=[
            pl.BlockSpec((tm, D), row),
            pl.BlockSpec((1, D), fixed),
            pl.BlockSpec(w_in.shape, fixed),
            pl.BlockSpec((1, KV_RANK), fixed),
            pl.BlockSpec(w_uk.shape, fixed),
            pl.BlockSpec(w_uv.shape, fixed),
        ],
        out_specs=[pl.BlockSpec((tm, n), row) for n, _ in outs],
        out_shape=[jax.ShapeDtypeStruct((T, n), dt) for n, dt in outs],
        compiler_params=_cparams(("parallel",)),
        name="even_proj",
    )(h, g_pre, w_in, kv_norm, w_uk, w_uv)


DSA_TQ = 128
DSA_CK = 512


def _dsa_kernel(q_ref, qi_ref, wi_ref, k_ref, v_ref, ki_ref, tri_ref, o_ref,
                key_ref, qm_ref, qim_ref, m_ref, l_ref, acc_ref, *, top_k):
    TQ, CK = DSA_TQ, DSA_CK
    j = pl.program_id(1)
    q0 = j * TQ
    nkc = (q0 + TQ + CK - 1) // CK

    lane = lax.broadcasted_iota(jnp.int32, (TQ, LANES), 1)
    low_half = lane < A_HEAD_DIM

    for h in range(A_HEADS):
        pr = slice((h // 2) * LANES, (h // 2 + 1) * LANES)
        keep = low_half if h % 2 == 0 else jnp.logical_not(low_half)
        qm_ref[h] = jnp.where(keep, q_ref[0, :, pr], jnp.zeros((), BF16))
        qim_ref[h] = jnp.where(keep, qi_ref[0, :, pr], jnp.zeros((), BF16))

    t_pos = q0 + lax.broadcasted_iota(jnp.int32, (TQ, CK), 0)
    k_iota = lax.broadcasted_iota(jnp.int32, (TQ, CK), 1)
    w_all = wi_ref[0]

    def score_chunk(c, carry):
        base = pl.multiple_of(c * CK, CK)
        kic = ki_ref[0, pl.ds(base, CK), :]
        acc = jnp.zeros((TQ, CK), F32)
        for h in range(IDX_HEADS):
            sc = _dot_nt(qim_ref[h], kic)
            acc = acc + jnp.maximum(sc, 0.0) * w_all[:, h:h + 1]
        key_ref[c] = jnp.where(base + k_iota <= t_pos, acc, -jnp.inf)
        return carry

    lax.fori_loop(0, nkc, score_chunk, 0)

    def count(pred_fn):
        def body(c, cnt):
            hit = pred_fn(key_ref[c]).astype(jnp.int32)
            part = hit[:, 0:LANES]
            for s in range(1, CK // LANES):
                part = part + hit[:, s * LANES:(s + 1) * LANES]
            return cnt + part
        cnt = lax.fori_loop(0, nkc, body, jnp.zeros((TQ, LANES), jnp.int32))
        return jnp.sum(cnt, axis=-1, keepdims=True)

    def code_to_float(code):
        code = jnp.clip(code, NEG_INF_CODE, POS_INF_CODE)
        return lax.bitcast_convert_type(code ^ ((code >> 31) & jnp.int32(0x7FFFFFFF)), F32)

    def bit_pass(i, code):
        cand = code + lax.shift_left(jnp.int32(1), 31 - i)
        cand_f = code_to_float(cand)
        n_ge = count(lambda sc: sc >= cand_f)
        return jnp.where(n_ge >= top_k, cand, code)

    thr = code_to_float(lax.fori_loop(0, 32, bit_pass, jnp.full((TQ, 1), INT_MIN, jnp.int32)))
    n_gt = count(lambda sc: sc > thr)
    need = jnp.where(thr > -jnp.inf, (top_k - n_gt).astype(F32), 0.0)

    m_ref[...] = jnp.full(m_ref.shape, -jnp.inf, F32)
    l_ref[...] = jnp.zeros(l_ref.shape, F32)
    acc_ref[...] = jnp.zeros(acc_ref.shape, F32)

    def attend_chunk(c, eq_seen):
        base = pl.multiple_of(c * CK, CK)
        kk = key_ref[c]
        eq = kk == thr
        rank = eq_seen + _dot(eq.astype(BF16), tri_ref[...])
        sel = (kk > thr) | (eq & (rank <= need))
        bias = jnp.where(sel, 0.0, NEG_BIG)
        for pr in range(A_HEADS // 2):
            kp = k_ref[0, pl.ds(base, CK), pr * LANES:(pr + 1) * LANES]
            vp = v_ref[0, pl.ds(base, CK), pr * LANES:(pr + 1) * LANES]
            pv, alpha = [], []
            for half in range(2):
                h = 2 * pr + half
                s = _dot_nt(qm_ref[h], kp) + bias
                m_old = m_ref[h]
                m_new = jnp.maximum(m_old, jnp.max(s, axis=-1, keepdims=True))
                a = jnp.exp(m_old - m_new)
                p = jnp.exp(s - m_new)
                l_ref[h] = a * l_ref[h] + jnp.sum(p, axis=-1, keepdims=True)
                m_ref[h] = m_new
                pv.append(_dot(p.astype(BF16), vp))
                alpha.append(a)
            acc_ref[pr] = (acc_ref[pr] * jnp.where(low_half, alpha[0], alpha[1])
                           + jnp.where(low_half, pv[0], pv[1]))
        return eq_seen + jnp.sum(eq.astype(F32), axis=-1, keepdims=True)

    lax.fori_loop(0, nkc, attend_chunk, jnp.zeros((TQ, 1), F32))

    for pr in range(A_HEADS // 2):
        denom = jnp.where(low_half, l_ref[2 * pr], l_ref[2 * pr + 1])
        o_ref[0, :, pr * LANES:(pr + 1) * LANES] = (acc_ref[pr] / denom).astype(o_ref.dtype)


def _dsa(q, k, v, qi, ki2, wi):
    B, S, _ = q.shape
    TQ, CK = DSA_TQ, DSA_CK
    nc = S // CK
    top_k = min(TOPK_MAX, S // 4)
    tri = (jnp.arange(CK)[:, None] <= jnp.arange(CK)[None, :]).astype(BF16)
    blk = lambda b, j: (b, j, 0)
    full = lambda b, j: (b, 0, 0)
    return pl.pallas_call(
        functools.partial(_dsa_kernel, top_k=top_k),
        grid=(B, S // TQ),
        in_specs=[
            pl.BlockSpec((1, TQ, A_WIDTH), blk),
            pl.BlockSpec((1, TQ, IDX_HEADS * IDX_DIM), blk),
            pl.BlockSpec((1, TQ, IDX_HEADS), blk),
            pl.BlockSpec((1, S, A_WIDTH), full),
            pl.BlockSpec((1, S, A_WIDTH), full),
            pl.BlockSpec((1, S, 2 * IDX_DIM), full),
            pl.BlockSpec((CK, CK), lambda b, j: (0, 0)),
        ],
        out_specs=pl.BlockSpec((1, TQ, A_WIDTH), blk),
        out_shape=jax.ShapeDtypeStruct((B, S, A_WIDTH), BF16),
        scratch_shapes=[
            pltpu.VMEM((nc, TQ, CK), F32),
            pltpu.VMEM((A_HEADS, TQ, LANES), BF16),
            pltpu.VMEM((IDX_HEADS, TQ, LANES), BF16),
            pltpu.VMEM((A_HEADS, TQ, 1), F32),
            pltpu.VMEM((A_HEADS, TQ, 1), F32),
            pltpu.VMEM((A_HEADS // 2, TQ, LANES), F32),
        ],
        compiler_params=_cparams(("parallel", "arbitrary")),
        name="dsa",
    )(q, qi, wi, k, v, ki2, tri)


LRU_TS = 512
HALO = SUBLANES


def _softplus(x):
    return jnp.maximum(x, 0.0) + jnp.log1p(jnp.exp(-jnp.abs(x)))


def _gelu_tanh(x):
    return 0.5 * x * (1.0 + jnp.tanh(0.7978845608028654 * (x + 0.044715 * (x * x * x))))


def _rglru_kernel(xb_ref, gate_ref, cw_ref, cb_ref, wra_ref, bra_ref, wri_ref, bri_ref, lam_ref,
                  o_ref, xs_ref, a_ref, b_ref, hc_ref):
    ts = xb_ref.shape[1]
    C = xb_ref.shape[2]

    @pl.when(pl.program_id(1) == 0)
    def _():
        xs_ref[0:HALO, :] = jnp.zeros((HALO, C), F32)
        hc_ref[...] = jnp.zeros(hc_ref.shape, F32)

    x = xb_ref[0]
    xs_ref[HALO:HALO + ts, :] = x
    xc = cb_ref[...] + jnp.zeros((ts, C), F32)
    for kk in range(CONV_W):
        off = HALO - (CONV_W - 1) + kk
        xc = xc + cw_ref[kk:kk + 1, :] * xs_ref[off:off + ts, :]
    xs_ref[0:HALO, :] = x[ts - HALO:ts, :]

    xcb = xc.astype(BF16)
    r = _sigmoid(_dot(xcb, wra_ref[...]) + bra_ref[...])
    gi = _sigmoid(_dot(xcb, wri_ref[...]) + bri_ref[...])
    log_a = (-LRU_C) * r * _softplus(-lam_ref[...])
    a = jnp.exp(log_a)
    a_ref[...] = a
    b_ref[...] = jnp.sqrt(-jnp.tanh(log_a) * (1.0 + a * a)) * (gi * xc)

    row = lax.broadcasted_iota(jnp.int32, (SUBLANES, C), 0)

    def group(g, carry):
        r0 = pl.multiple_of(g * SUBLANES, SUBLANES)
        av = a_ref[pl.ds(r0, SUBLANES), :]
        bv = b_ref[pl.ds(r0, SUBLANES), :]
        for sh in (1, 2, 4):
            a_sh = pltpu.roll(av, sh, axis=0)
            b_sh = pltpu.roll(bv, sh, axis=0)
            ok = row >= sh
            bv = jnp.where(ok, av * b_sh + bv, bv)
            av = jnp.where(ok, av * a_sh, av)
        h8 = av * carry + bv
        a_ref[pl.ds(r0, SUBLANES), :] = h8
        return jnp.broadcast_to(h8[SUBLANES - 1:SUBLANES, :], (SUBLANES, C))

    hc_ref[...] = lax.fori_loop(0, ts // SUBLANES, group, hc_ref[...])
    o_ref[0] = (a_ref[...] * _gelu_tanh(gate_ref[0])).astype(o_ref.dtype)


def _rglru(xb, gate, conv_w, conv_b, w_ra, b_ra, w_ri, b_ri, lam):
    B, S, C = xb.shape
    ts = min(LRU_TS, S)
    blk = lambda b, s: (b, s, 0)
    fixed = lambda b, s: (0, 0)
    return pl.pallas_call(
        _rglru_kernel,
        grid=(B, S // ts),
        in_specs=[
            pl.BlockSpec((1, ts, C), blk),
            pl.BlockSpec((1, ts, C), blk),
            pl.BlockSpec((CONV_W, C), fixed),
            pl.BlockSpec((1, C), fixed),
            pl.BlockSpec((C, C), fixed),
            pl.BlockSpec((1, C), fixed),
            pl.BlockSpec((C, C), fixed),
            pl.BlockSpec((1, C), fixed),
            pl.BlockSpec((1, C), fixed),
        ],
        out_specs=pl.BlockSpec((1, ts, C), blk),
        out_shape=jax.ShapeDtypeStruct((B, S, C), BF16),
        scratch_shapes=[
            pltpu.VMEM((HALO + ts, C), F32),
            pltpu.VMEM((ts, C), F32),
            pltpu.VMEM((ts, C), F32),
            pltpu.VMEM((SUBLANES, C), F32),
        ],
        compiler_params=_cparams(("parallel", "arbitrary")),
        name="rglru",
    )(xb, gate, conv_w, conv_b, w_ra, b_ra, w_ri, b_ri, lam)


OD_QKVR = 3072
OD_GLR = (3072, 3200)
OD_COLS = 3200


def _odd_proj_kernel(h_ref, g_ref, w_ref, wg2_ref, bg_ref, q_ref, k_ref, v_ref, r_ref, gk_ref):
    xn = _rms(h_ref[...], g_ref[...]).astype(BF16)

    def seg(a, b):
        return _dot(xn, w_ref[:, a:b])

    q_ref[...] = seg(0, GLA_DK) * (GLA_DKH ** -0.5)
    k_ref[...] = seg(GLA_DK, 2 * GLA_DK)
    v_ref[...] = seg(2 * GLA_DK, 2 * GLA_DK + GLA_DV).astype(BF16)
    r_ref[...] = seg(2 * GLA_DK + GLA_DV, OD_QKVR)
    glr = seg(*OD_GLR).astype(BF16)
    z = _dot(glr, wg2_ref[...]) + bg_ref[...]
    gk_ref[...] = (-_softplus(-z)) * (1.0 / GLA_TAU)


def _odd_proj(h, g_pre, w_in, w_g2, b_g):
    T, D = h.shape
    tm = min(PROJ_TM, T)
    row = lambda i: (i, 0)
    fixed = lambda i: (0, 0)
    outs = [(GLA_DK, F32), (GLA_DK, F32), (GLA_DV, BF16), (GLA_DV, F32), (GLA_DK, F32)]
    return pl.pallas_call(
        _odd_proj_kernel,
        grid=(T // tm,),
        in_specs=[
            pl.BlockSpec((tm, D), row),
            pl.BlockSpec((1, D), fixed),
            pl.BlockSpec(w_in.shape, fixed),
            pl.BlockSpec(w_g2.shape, fixed),
            pl.BlockSpec((1, GLA_DK), fixed),
        ],
        out_specs=[pl.BlockSpec((tm, n), row) for n, _ in outs],
        out_shape=[jax.ShapeDtypeStruct((T, n), dt) for n, dt in outs],
        compiler_params=_cparams(("parallel",)),
        name="odd_proj",
    )(h, g_pre, w_in, w_g2, b_g)


GLA_TS = 256


def _gla_kernel(q_ref, k_ref, v_ref, gk_ref, r_ref, hn_ref, o_ref, st_ref):
    ts = q_ref.shape[1]
    C = GLA_CHUNK

    @pl.when(pl.program_id(1) == 0)
    def _():
        st_ref[...] = jnp.zeros(st_ref.shape, F32)

    ri = lax.broadcasted_iota(jnp.int32, (C, C), 0)
    ci = lax.broadcasted_iota(jnp.int32, (C, C), 1)
    tril = ri >= ci
    tril_f = tril.astype(F32)

    for c in range(ts // C):
        rows = slice(c * C, (c + 1) * C)
        for h in range(GLA_HEADS):
            ksl = slice(h * GLA_DKH, (h + 1) * GLA_DKH)
            vsl = slice(h * GLA_DVH, (h + 1) * GLA_DVH)
            G = jnp.dot(tril_f, gk_ref[0, rows, ksl], preferred_element_type=F32,
                        precision=lax.Precision.HIGHEST)
            g_last = G[C - 1:C, :]
            kc = k_ref[0, rows, ksl]
            qd = (q_ref[0, rows, ksl] * jnp.exp(G)).astype(BF16)
            k_inv = (kc * jnp.exp(-G)).astype(BF16)
            k_rem = (kc * jnp.exp(g_last - G)).astype(BF16)
            vh = v_ref[0, rows, vsl]
            att = jnp.where(tril, _dot_nt(qd, k_inv), 0.0).astype(BF16)
            st = st_ref[h]
            o = _dot(att, vh) + _dot_nt(qd, st.astype(BF16))
            st_ref[h] = st * jnp.exp(g_last) + _dot_tn(vh, k_rem)
            on = _rms(o, hn_ref[...])
            rr = r_ref[0, rows, vsl]
            o_ref[0, rows, vsl] = (on * (rr * _sigmoid(rr))).astype(o_ref.dtype)


def _gla(q, k, v, gk, r, head_norm):
    B, S, _ = q.shape
    ts = min(GLA_TS, S)
    blk = lambda b, s: (b, s, 0)
    return pl.pallas_call(
        _gla_kernel,
        grid=(B, S // ts),
        in_specs=[
            pl.BlockSpec((1, ts, GLA_DK), blk),
            pl.BlockSpec((1, ts, GLA_DK), blk),
            pl.BlockSpec((1, ts, GLA_DV), blk),
            pl.BlockSpec((1, ts, GLA_DK), blk),
            pl.BlockSpec((1, ts, GLA_DV), blk),
            pl.BlockSpec((1, GLA_DVH), lambda b, s: (0, 0)),
        ],
        out_specs=pl.BlockSpec((1, ts, GLA_DV), blk),
        out_shape=jax.ShapeDtypeStruct((B, S, GLA_DV), BF16),
        scratch_shapes=[pltpu.VMEM((GLA_HEADS, GLA_DVH, GLA_DKH), F32)],
        compiler_params=_cparams(("parallel", "arbitrary")),
        name="gla",
    )(q, k, v, gk, r, head_norm)


def _xa_kv_kernel(mem_ref, g_ref, w_ref, k_ref, v_ref):
    mn = _rms(mem_ref[0], g_ref[...]).astype(BF16)
    k_ref[0] = (_dot(mn, w_ref[:, :D_MODEL]) * (XA_HEAD_DIM ** -0.5)).astype(BF16)
    v_ref[0] = _dot(mn, w_ref[:, D_MODEL:]).astype(BF16)


def _xa_kv(mem, g_mem, w_kv):
    B, M, D = mem.shape
    blk = lambda b: (b, 0, 0)
    return pl.pallas_call(
        _xa_kv_kernel,
        grid=(B,),
        in_specs=[pl.BlockSpec((1, M, D), blk), pl.BlockSpec((1, D), lambda b: (0, 0)),
                  pl.BlockSpec(w_kv.shape, lambda b: (0, 0))],
        out_specs=[pl.BlockSpec((1, M, D), blk), pl.BlockSpec((1, M, D), blk)],
        out_shape=[jax.ShapeDtypeStruct((B, M, D), BF16)] * 2,
        compiler_params=_cparams(("parallel",)),
        name="xa_kv",
    )(mem, g_mem, w_kv)


XA_TM = 512


def _xa_kernel(h_ref, gpre_ref, gpost_ref, wq_ref, k_ref, v_ref, wo_ref, o_ref):
    x = h_ref[0]
    xn = _rms(x, gpre_ref[...]).astype(BF16)
    q = _dot(xn, wq_ref[...]).astype(BF16)
    heads = []
    for h in range(XA_HEADS):
        sl = slice(h * XA_HEAD_DIM, (h + 1) * XA_HEAD_DIM)
        s = _dot_nt(q[:, sl], k_ref[0, :, sl])
        p = jnp.exp(s - jnp.max(s, axis=-1, keepdims=True))
        oh = _dot(p.astype(BF16), v_ref[0, :, sl]) / jnp.sum(p, axis=-1, keepdims=True)
        heads.append(oh.astype(BF16))
    c = _dot(jnp.concatenate(heads, axis=-1), wo_ref[...])
    o_ref[0] = x + _rms(c, gpost_ref[...])


def _xa(h, g_pre, g_post, w_q, kx, vx, w_o):
    B, S, D = h.shape
    M = kx.shape[1]
    tm = min(XA_TM, S)
    blk = lambda b, i: (b, i, 0)
    fixed = lambda b, i: (0, 0)
    return pl.pallas_call(
        _xa_kernel,
        grid=(B, S // tm),
        in_specs=[
            pl.BlockSpec((1, tm, D), blk),
            pl.BlockSpec((1, D), fixed),
            pl.BlockSpec((1, D), fixed),
            pl.BlockSpec((D, D), fixed),
            pl.BlockSpec((1, M, D), lambda b, i: (b, 0, 0)),
            pl.BlockSpec((1, M, D), lambda b, i: (b, 0, 0)),
            pl.BlockSpec((D, D), fixed),
        ],
        out_specs=pl.BlockSpec((1, tm, D), blk),
        out_shape=jax.ShapeDtypeStruct((B, S, D), F32),
        compiler_params=_cparams(("parallel", "parallel")),
        name="xa",
    )(h, g_pre, g_post, w_q, kx, vx, w_o)


def _block_diag(w):
    G, n, _ = w.shape
    eye = jnp.eye(G, dtype=w.dtype)
    return (eye[:, None, :, None] * w[:, :, None, :]).reshape(G * n, G * n)


def _even_w_in(w):
    ki = w[:, 1280:1344]
    pad = jnp.zeros((w.shape[0], LANES - IDX_HEADS), w.dtype)
    return jnp.concatenate([w[:, :1280], ki, ki, w[:, 1352:2376], w[:, 1344:1352], pad], axis=1).astype(BF16)


def _odd_w_in(w):
    pad = jnp.zeros((w.shape[0], LANES - GLA_GATE_RANK), w.dtype)
    return jnp.concatenate([w[:, :2048], w[:, 2064:3088], w[:, 2048:2064], pad], axis=1).astype(BF16)


def kernel(x, mem, norms, ffn_w_gu, ffn_w_down, xa_w_q, xa_w_kv, xa_w_o, ev_w_in, ev_kv_norm, ev_w_uk, ev_w_uv, ev_conv_w, ev_conv_b, ev_w_ra, ev_b_ra, ev_w_ri, ev_b_ri, ev_lam, ev_w_out, od_w_in, od_w_g2, od_b_g, od_head_norm, od_w_out):
    B, S, D = x.shape
    T = B * S
    depth = norms.shape[0]
    h = x.reshape(T, D)

    def gain(layer, idx):
        return norms[layer, idx][None, :]

    for layer in range(depth):
        h = _ffn(h, gain(layer, N_FFN1_PRE), gain(layer, N_FFN1_POST),
                 ffn_w_gu[layer, 0].astype(BF16), ffn_w_down[layer, 0].astype(BF16))

        if layer % 2 == 0:
            e = layer // 2
            q, k, v, qi, ki2, wi, gate, xb = _even_proj(
                h, gain(layer, N_MIX_PRE), _even_w_in(ev_w_in[e]), ev_kv_norm[e][None, :],
                ev_w_uk[e].astype(BF16), ev_w_uv[e].astype(BF16))
            r3 = lambda a: a.reshape(B, S, a.shape[-1])
            a_out = _dsa(r3(q), r3(k), r3(v), r3(qi), r3(ki2), r3(wi))
            b_out = _rglru(r3(xb), r3(gate), ev_conv_w[e], ev_conv_b[e][None, :],
                           _block_diag(ev_w_ra[e]).astype(BF16), ev_b_ra[e].reshape(1, B_WIDTH),
                           _block_diag(ev_w_ri[e]).astype(BF16), ev_b_ri[e].reshape(1, B_WIDTH),
                           ev_lam[e][None, :])
            parts = [a_out.reshape(T, A_WIDTH), b_out.reshape(T, B_WIDTH)]
            w_out = ev_w_out[e].astype(BF16)
        else:
            o = layer // 2
            w_g2 = jnp.concatenate(
                [od_w_g2[o], jnp.zeros((LANES - GLA_GATE_RANK, GLA_DK), od_w_g2.dtype)], axis=0).astype(BF16)
            q, k, v, r, gk = _odd_proj(h, gain(layer, N_MIX_PRE), _odd_w_in(od_w_in[o]), w_g2,
                                       od_b_g[o][None, :])
            r3 = lambda a: a.reshape(B, S, a.shape[-1])
            g_out = _gla(r3(q), r3(k), r3(v), r3(gk), r3(r), od_head_norm[o][None, :])
            parts = [g_out.reshape(T, GLA_DV)]
            w_out = od_w_out[o].astype(BF16)
        h = _out_proj(h, gain(layer, N_MIX_POST), parts, w_out)

        kx, vx = _xa_kv(mem, gain(layer, N_MEM_NORM), xa_w_kv[layer].astype(BF16))
        h = _xa(h.reshape(B, S, D), gain(layer, N_XA_PRE), gain(layer, N_XA_POST),
                xa_w_q[layer].astype(BF16), kx, vx, xa_w_o[layer].astype(BF16)).reshape(T, D)

        h = _ffn(h, gain(layer, N_FFN2_PRE), gain(layer, N_FFN2_POST),
                 ffn_w_gu[layer, 1].astype(BF16), ffn_w_down[layer, 1].astype(BF16))
    return h.reshape(B, S, D)
```

```python
import functools

import jax
import jax.numpy as jnp
from jax import lax
from jax.experimental import pallas as pl
from jax.experimental.pallas import tpu as pltpu

F32 = jnp.float32
BF16 = jnp.bfloat16

EPS = 1e-6
D_MODEL = 1024
D_FF = 2816
XA_HEADS = 4
XA_HEAD_DIM = D_MODEL // XA_HEADS
A_HEADS = 8
A_HEAD_DIM = 64
A_WIDTH = A_HEADS * A_HEAD_DIM
KV_RANK = 256
IDX_HEADS = 8
IDX_DIM = 64
TOPK_MAX = 256
B_WIDTH = D_MODEL - A_WIDTH
B_BLOCKS = 8
B_BLOCK_DIM = B_WIDTH // B_BLOCKS
CONV_W = 4
LRU_C = 8.0
GLA_HEADS = 4
GLA_DK = D_MODEL // 2
GLA_DV = D_MODEL
GLA_DKH = GLA_DK // GLA_HEADS
GLA_DVH = GLA_DV // GLA_HEADS
GLA_GATE_RANK = 16
GLA_TAU = 16.0
GLA_CHUNK = 64
(N_FFN1_PRE, N_FFN1_POST, N_MIX_PRE, N_MIX_POST, N_XA_PRE, N_XA_POST, N_MEM_NORM,
 N_FFN2_PRE, N_FFN2_POST) = range(9)

LANES = 128
SUBLANES = 8
VMEM_LIMIT = 48 * 1024 * 1024

NEG_BIG = -1e30
LOG2E = 1.4426950408889634
INT_MIN = -2 ** 31
POS_INF_CODE = 0x7F800000
NEG_INF_CODE = -0x7F800001


def _cparams(sem):
    return pltpu.CompilerParams(dimension_semantics=sem, vmem_limit_bytes=VMEM_LIMIT)


def _rms(x, g):
    return x * lax.rsqrt(jnp.mean(x * x, axis=-1, keepdims=True) + EPS) * g


def _dot(a, b):
    return jnp.dot(a, b, preferred_element_type=F32)


def _dot_nt(a, b):
    return lax.dot_general(a, b, (((1,), (1,)), ((), ())), preferred_element_type=F32)


def _dot_tn(a, b):
    return lax.dot_general(a, b, (((0,), (0,)), ((), ())), preferred_element_type=F32)


def _sigmoid(x):
    return 1.0 / (1.0 + jnp.exp(-x))


FFN_TM = 1024
FFN_TF = 256


def _ffn_kernel(h_ref, gpre_ref, gpost_ref, wg_ref, wu_ref, wd_ref, o_ref, xn_ref, acc_ref):
    j = pl.program_id(1)

    @pl.when(j == 0)
    def _():
        xn_ref[...] = _rms(h_ref[...], gpre_ref[...]).astype(BF16)

    xn = xn_ref[...]
    g = _dot(xn, wg_ref[...])
    u = _dot(xn, wu_ref[...])
    act = (g * _sigmoid(g) * u).astype(BF16)
    part = _dot(act, wd_ref[...])

    @pl.when(j == 0)
    def _():
        acc_ref[...] = part

    @pl.when(j > 0)
    def _():
        acc_ref[...] += part

    @pl.when(j == pl.num_programs(1) - 1)
    def _():
        o_ref[...] = h_ref[...] + 0.5 * _rms(acc_ref[...], gpost_ref[...])


def _ffn(h, g_pre, g_post, w_gu, w_down):
    T, D = h.shape
    F = w_down.shape[0]
    tm, tf = min(FFN_TM, T), FFN_TF
    nf = F // tf
    return pl.pallas_call(
        _ffn_kernel,
        grid=(T // tm, nf),
        in_specs=[
            pl.BlockSpec((tm, D), lambda i, j: (i, 0)),
            pl.BlockSpec((1, D), lambda i, j: (0, 0)),
            pl.BlockSpec((1, D), lambda i, j: (0, 0)),
            pl.BlockSpec((D, tf), lambda i, j: (0, j)),
            pl.BlockSpec((D, tf), lambda i, j: (0, j + nf)),
            pl.BlockSpec((tf, D), lambda i, j: (j, 0)),
        ],
        out_specs=pl.BlockSpec((tm, D), lambda i, j: (i, 0)),
        out_shape=jax.ShapeDtypeStruct((T, D), F32),
        scratch_shapes=[pltpu.VMEM((tm, D), BF16), pltpu.VMEM((tm, D), F32)],
        compiler_params=_cparams(("parallel", "arbitrary")),
        name="ffn",
    )(h, g_pre, g_post, w_gu, w_gu, w_down)


PROJ_TM = 512


def _out_proj_kernel(*refs, offsets, scale):
    h_ref, g_ref = refs[0], refs[1]
    part_refs = refs[2:2 + len(offsets)]
    w_ref = refs[2 + len(offsets)]
    o_ref = refs[3 + len(offsets)]
    m = None
    for p_ref, off in zip(part_refs, offsets):
        kk = p_ref.shape[-1]
        term = _dot(p_ref[...], w_ref[off:off + kk, :])
        m = term if m is None else m + term
    o_ref[...] = h_ref[...] + scale * _rms(m, g_ref[...])


def _out_proj(h, g_post, parts, w_out, scale=1.0):
    T, D = h.shape
    tm = min(PROJ_TM, T)
    offsets, off = [], 0
    for p in parts:
        offsets.append(off)
        off += p.shape[-1]
    in_specs = [pl.BlockSpec((tm, D), lambda i: (i, 0)), pl.BlockSpec((1, D), lambda i: (0, 0))]
    in_specs += [pl.BlockSpec((tm, p.shape[-1]), lambda i: (i, 0)) for p in parts]
    in_specs += [pl.BlockSpec(w_out.shape, lambda i: (0, 0))]
    return pl.pallas_call(
        functools.partial(_out_proj_kernel, offsets=tuple(offsets), scale=scale),
        grid=(T // tm,),
        in_specs=in_specs,
        out_specs=pl.BlockSpec((tm, D), lambda i: (i, 0)),
        out_shape=jax.ShapeDtypeStruct((T, D), F32),
        compiler_params=_cparams(("parallel",)),
        name="out_proj",
    )(h, g_post, *parts, w_out)


EV_Q = (0, 512)
EV_CKV = (512, 768)
EV_QI = (768, 1280)
EV_KI2 = (1280, 1408)
EV_GATE = (1408, 1920)
EV_XB = (1920, 2432)
EV_WI = (2432, 2560)
EV_COLS = 2560


DSA_TQ = 256
DSA_CK = 512


def _even_proj_kernel(h_ref, g_ref, w_ref, kvn_ref, wuk_ref, wuv_ref,
                      qt_ref, k_ref, vt_ref, qit_ref, ki_ref, wit_ref, gate_ref, xb_ref):
    xn = _rms(h_ref[...], g_ref[...]).astype(BF16)

    def seg(ab):
        return _dot(xn, w_ref[:, ab[0]:ab[1]])

    qt_ref[...] = (seg(EV_Q) * (A_HEAD_DIM ** -0.5 * LOG2E)).T.astype(BF16)
    ckv = _rms(seg(EV_CKV), kvn_ref[...]).astype(BF16)
    k_ref[...] = _dot(ckv, wuk_ref[...]).astype(BF16)
    vt_ref[0] = _dot(ckv, wuv_ref[...]).astype(BF16).T
    qit_ref[...] = (seg(EV_QI) * (IDX_DIM ** -0.5)).T.astype(BF16)
    ki_ref[...] = seg(EV_KI2).astype(BF16)
    wit_ref[...] = (seg(EV_WI) * (IDX_HEADS ** -0.5)).T[:IDX_HEADS, :]
    gate_ref[...] = seg(EV_GATE)
    xb_ref[...] = seg(EV_XB)


def _even_proj(h, g_pre, w_in, kv_norm, w_uk, w_uv):
    T, D = h.shape
    tm = DSA_CK
    row = lambda i: (i, 0)
    col = lambda i: (0, i)
    fixed = lambda i: (0, 0)
    out_specs = [
        pl.BlockSpec((A_WIDTH, tm), col),
        pl.BlockSpec((tm, A_WIDTH), row),
        pl.BlockSpec((1, A_WIDTH, tm), lambda i: (i, 0, 0)),
        pl.BlockSpec((IDX_HEADS * IDX_DIM, tm), col),
        pl.BlockSpec((tm, 2 * IDX_DIM), row),
        pl.BlockSpec((IDX_HEADS, tm), col),
        pl.BlockSpec((tm, B_WIDTH), row),
        pl.BlockSpec((tm, B_WIDTH), row),
    ]
    out_shape = [
        jax.ShapeDtypeStruct((A_WIDTH, T), BF16),
        jax.ShapeDtypeStruct((T, A_WIDTH), BF16),
        jax.ShapeDtypeStruct((T // tm, A_WIDTH, tm), BF16),
        jax.ShapeDtypeStruct((IDX_HEADS * IDX_DIM, T), BF16),
        jax.ShapeDtypeStruct((T, 2 * IDX_DIM), BF16),
        jax.ShapeDtypeStruct((IDX_HEADS, T), F32),
        jax.ShapeDtypeStruct((T, B_WIDTH), F32),
        jax.ShapeDtypeStruct((T, B_WIDTH), F32),
    ]
    return pl.pallas_call(
        _even_proj_kernel,
        grid=(T // tm,),
        in_specs=[
            pl.BlockSpec((tm, D), row),
            pl.BlockSpec((1, D), fixed),
            pl.BlockSpec(w_in.shape, fixed),
            pl.BlockSpec((1, KV_RANK), fixed),
            pl.BlockSpec(w_uk.shape, fixed),
            pl.BlockSpec(w_uv.shape, fixed),
        ],
        out_specs=out_specs,
        out_shape=out_shape,
        compiler_params=_cparams(("parallel",)),
        name="even_proj",
    )(h, g_pre, w_in, kv_norm, w_uk, w_uv)


def _col_reduce(x, op):
    rows, n = x.shape
    part = x.reshape(rows // SUBLANES, SUBLANES, n)
    part = jnp.max(part, axis=0) if op == "max" else jnp.sum(part, axis=0)
    return (jnp.max(part, axis=0, keepdims=True) if op == "max"
            else jnp.sum(part, axis=0, keepdims=True))


def _dsa_kernel(qt_ref, qit_ref, wit_ref, k_ref, vt_ref, ki_ref, tri_ref, o_ref,
                sc_ref, bias_ref, s_ref, qm_ref, qim_ref, m_ref, l_ref, acc_ref, need_ref, seen_ref,
                *, top_k):
    TQ, CK = DSA_TQ, DSA_CK
    j = pl.program_id(1)
    q0 = j * TQ
    nkc = (q0 + TQ + CK - 1) // CK

    low_half = lax.broadcasted_iota(jnp.int32, (LANES, TQ), 0) < A_HEAD_DIM
    for h in range(A_HEADS):
        pr = slice((h // 2) * LANES, (h // 2 + 1) * LANES)
        keep = low_half if h % 2 == 0 else jnp.logical_not(low_half)
        qm_ref[h] = jnp.where(keep, qt_ref[pr, :], jnp.zeros((), BF16))
        qim_ref[h] = jnp.where(keep, qit_ref[pr, :], jnp.zeros((), BF16))

    key_iota = lax.broadcasted_iota(jnp.int32, (CK, TQ), 0)
    q_pos = q0 + lax.broadcasted_iota(jnp.int32, (CK, TQ), 1)

    def score_chunk(c, carry):
        base = pl.multiple_of(c * CK, CK)
        kic = ki_ref[pl.ds(base, CK), :]
        acc = jnp.zeros((CK, TQ), F32)
        for h in range(IDX_HEADS):
            acc = acc + jnp.maximum(_dot(kic, qim_ref[h]), 0.0) * wit_ref[h:h + 1, :]
        sc_ref[c] = jnp.where(base + key_iota <= q_pos, acc, -jnp.inf)
        return carry

    lax.fori_loop(0, nkc, score_chunk, 0)

    def count(pred_fn):
        def body(c, cnt):
            hit = pred_fn(sc_ref[c]).astype(jnp.int32)
            return cnt + jnp.sum(hit.reshape(CK // SUBLANES, SUBLANES, TQ), axis=0)
        cnt = lax.fori_loop(0, nkc, body, jnp.zeros((SUBLANES, TQ), jnp.int32))
        return jnp.sum(cnt, axis=0, keepdims=True)

    def code_to_float(code):
        code = jnp.clip(code, NEG_INF_CODE, POS_INF_CODE)
        return lax.bitcast_convert_type(code ^ ((code >> 31) & jnp.int32(0x7FFFFFFF)), F32)

    def bit_pass(i, carry):
        code, n_at = carry
        cand = code + lax.shift_left(jnp.int32(1), 31 - i)
        cand_f = code_to_float(cand)
        n_ge = count(lambda sc: sc >= cand_f)
        take = n_ge >= top_k
        return jnp.where(take, cand, code), jnp.where(take, n_ge, n_at)

    code, n_at = lax.fori_loop(
        0, 32, bit_pass,
        (jnp.full((1, TQ), INT_MIN, jnp.int32), jnp.full((1, TQ), nkc * CK, jnp.int32)))
    thr = code_to_float(code)
    finite = thr > -jnp.inf
    has_ties = jnp.max(jnp.where(finite & (n_at > top_k), 1, 0)) > 0
    thr_sel = jnp.where(finite, thr, jnp.finfo(F32).min)

    m_ref[...] = jnp.full(m_ref.shape, -jnp.inf, F32)
    l_ref[...] = jnp.zeros(l_ref.shape, F32)
    acc_ref[...] = jnp.zeros(acc_ref.shape, F32)
    need_ref[...] = jnp.zeros(need_ref.shape, F32)
    seen_ref[...] = jnp.zeros(seen_ref.shape, F32)

    @pl.when(has_ties)
    def _():
        n_gt = count(lambda sc: sc > thr)
        need_ref[...] = jnp.where(finite, (top_k - n_gt).astype(F32), 0.0)

    def attend_chunk(c, carry):
        base = pl.multiple_of(c * CK, CK)

        @pl.when(jnp.logical_not(has_ties))
        def _():
            bias_ref[...] = jnp.where(sc_ref[c] >= thr_sel, 0.0, NEG_BIG)

        @pl.when(has_ties)
        def _():
            sc = sc_ref[c]
            eq = sc == thr
            rank = seen_ref[...] + _dot(tri_ref[...], eq.astype(BF16))
            sel = (sc > thr) | (eq & (rank <= need_ref[...]))
            bias_ref[...] = jnp.where(sel, 0.0, NEG_BIG)
            seen_ref[...] += _col_reduce(eq.astype(F32), "sum")

        cmax = []
        for h in range(A_HEADS):
            kp = k_ref[pl.ds(base, CK), (h // 2) * LANES:(h // 2 + 1) * LANES]
            s = _dot(kp, qm_ref[h]) + bias_ref[...]
            s_ref[h] = s
            cmax.append(_col_reduce(s, "max"))
        for h in range(A_HEADS):
            m_old = m_ref[h]
            m_new = jnp.maximum(m_old, cmax[h])
            alpha = jnp.exp2(m_old - m_new)
            p = jnp.exp2(s_ref[h] - m_new)
            l_ref[h] = alpha * l_ref[h] + _col_reduce(p, "sum")
            m_ref[h] = m_new
            vth = vt_ref[c, h * A_HEAD_DIM:(h + 1) * A_HEAD_DIM, :]
            acc_ref[h] = acc_ref[h] * alpha + _dot(vth, p.astype(BF16))
        return carry

    lax.fori_loop(0, nkc, attend_chunk, 0)

    out_t = jnp.concatenate([acc_ref[h] / l_ref[h] for h in range(A_HEADS)], axis=0)
    o_ref[...] = out_t.T.astype(o_ref.dtype)


def _dsa(qt, k, vt, qit, ki2, wit, B):
    T = k.shape[0]
    S = T // B
    TQ, CK = DSA_TQ, DSA_CK
    nc, nq = S // CK, S // TQ
    top_k = min(TOPK_MAX, S // 4)
    tri = (jnp.arange(CK)[:, None] >= jnp.arange(CK)[None, :]).astype(BF16)
    qcol = lambda b, j: (0, b * nq + j)
    return pl.pallas_call(
        functools.partial(_dsa_kernel, top_k=top_k),
        grid=(B, nq),
        in_specs=[
            pl.BlockSpec((A_WIDTH, TQ), qcol),
            pl.BlockSpec((IDX_HEADS * IDX_DIM, TQ), qcol),
            pl.BlockSpec((IDX_HEADS, TQ), qcol),
            pl.BlockSpec((S, A_WIDTH), lambda b, j: (b, 0)),
            pl.BlockSpec((nc, A_WIDTH, CK), lambda b, j: (b, 0, 0)),
            pl.BlockSpec((S, 2 * IDX_DIM), lambda b, j: (b, 0)),
            pl.BlockSpec((CK, CK), lambda b, j: (0, 0)),
        ],
        out_specs=pl.BlockSpec((TQ, A_WIDTH), lambda b, j: (b * nq + j, 0)),
        out_shape=jax.ShapeDtypeStruct((T, A_WIDTH), BF16),
        scratch_shapes=[
            pltpu.VMEM((nc, CK, TQ), F32),
            pltpu.VMEM((CK, TQ), F32),
            pltpu.VMEM((A_HEADS, CK, TQ), F32),
            pltpu.VMEM((A_HEADS, LANES, TQ), BF16),
            pltpu.VMEM((IDX_HEADS, LANES, TQ), BF16),
            pltpu.VMEM((A_HEADS, 1, TQ), F32),
            pltpu.VMEM((A_HEADS, 1, TQ), F32),
            pltpu.VMEM((A_HEADS, A_HEAD_DIM, TQ), F32),
            pltpu.VMEM((1, TQ), F32),
            pltpu.VMEM((1, TQ), F32),
        ],
        compiler_params=_cparams(("parallel", "arbitrary")),
        name="dsa",
    )(qt, qit, wit, k, vt, ki2, tri)


LRU_TS = 512
HALO = SUBLANES


def _softplus(x):
    return jnp.maximum(x, 0.0) + jnp.log1p(jnp.exp(-jnp.abs(x)))


def _gelu_tanh(x):
    return 0.5 * x * (1.0 + jnp.tanh(0.7978845608028654 * (x + 0.044715 * (x * x * x))))


def _rglru_kernel(xb_ref, gate_ref, cw_ref, cb_ref, wra_ref, bra_ref, wri_ref, bri_ref, lam_ref,
                  o_ref, xs_ref, a_ref, b_ref, hc_ref):
    ts = xb_ref.shape[1]
    C = xb_ref.shape[2]

    @pl.when(pl.program_id(1) == 0)
    def _():
        xs_ref[0:HALO, :] = jnp.zeros((HALO, C), F32)
        hc_ref[...] = jnp.zeros(hc_ref.shape, F32)

    x = xb_ref[0]
    xs_ref[HALO:HALO + ts, :] = x
    xc = cb_ref[...] + jnp.zeros((ts, C), F32)
    for kk in range(CONV_W):
        off = HALO - (CONV_W - 1) + kk
        xc = xc + cw_ref[kk:kk + 1, :] * xs_ref[off:off + ts, :]
    xs_ref[0:HALO, :] = x[ts - HALO:ts, :]

    xcb = xc.astype(BF16)
    r = _sigmoid(_dot(xcb, wra_ref[...]) + bra_ref[...])
    gi = _sigmoid(_dot(xcb, wri_ref[...]) + bri_ref[...])
    log_a = (-LRU_C) * r * _softplus(-lam_ref[...])
    a = jnp.exp(log_a)
    a_ref[...] = a
    b_ref[...] = jnp.sqrt(-jnp.tanh(log_a) * (1.0 + a * a)) * (gi * xc)

    row = lax.broadcasted_iota(jnp.int32, (SUBLANES, C), 0)

    def group(g, carry):
        r0 = pl.multiple_of(g * SUBLANES, SUBLANES)
        av = a_ref[pl.ds(r0, SUBLANES), :]
        bv = b_ref[pl.ds(r0, SUBLANES), :]
        for sh in (1, 2, 4):
            a_sh = pltpu.roll(av, sh, axis=0)
            b_sh = pltpu.roll(bv, sh, axis=0)
            ok = row >= sh
            bv = jnp.where(ok, av * b_sh + bv, bv)
            av = jnp.where(ok, av * a_sh, av)
        h8 = av * carry + bv
        a_ref[pl.ds(r0, SUBLANES), :] = h8
        return jnp.broadcast_to(h8[SUBLANES - 1:SUBLANES, :], (SUBLANES, C))

    hc_ref[...] = lax.fori_loop(0, ts // SUBLANES, group, hc_ref[...])
    o_ref[0] = (a_ref[...] * _gelu_tanh(gate_ref[0])).astype(o_ref.dtype)


def _rglru(xb, gate, conv_w, conv_b, w_ra, b_ra, w_ri, b_ri, lam):
    B, S, C = xb.shape
    ts = min(LRU_TS, S)
    blk = lambda b, s: (b, s, 0)
    fixed = lambda b, s: (0, 0)
    return pl.pallas_call(
        _rglru_kernel,
        grid=(B, S // ts),
        in_specs=[
            pl.BlockSpec((1, ts, C), blk),
            pl.BlockSpec((1, ts, C), blk),
            pl.BlockSpec((CONV_W, C), fixed),
            pl.BlockSpec((1, C), fixed),
            pl.BlockSpec((C, C), fixed),
            pl.BlockSpec((1, C), fixed),
            pl.BlockSpec((C, C), fixed),
            pl.BlockSpec((1, C), fixed),
            pl.BlockSpec((1, C), fixed),
        ],
        out_specs=pl.BlockSpec((1, ts, C), blk),
        out_shape=jax.ShapeDtypeStruct((B, S, C), BF16),
        scratch_shapes=[
            pltpu.VMEM((HALO + ts, C), F32),
            pltpu.VMEM((ts, C), F32),
            pltpu.VMEM((ts, C), F32),
            pltpu.VMEM((SUBLANES, C), F32),
        ],
        compiler_params=_cparams(("parallel", "arbitrary")),
        name="rglru",
    )(xb, gate, conv_w, conv_b, w_ra, b_ra, w_ri, b_ri, lam)


OD_QKVR = 3072
OD_GLR = (3072, 3200)
OD_COLS = 3200


def _odd_proj_kernel(h_ref, g_ref, w_ref, wg2_ref, bg_ref, q_ref, k_ref, v_ref, r_ref, gk_ref):
    xn = _rms(h_ref[...], g_ref[...]).astype(BF16)

    def seg(a, b):
        return _dot(xn, w_ref[:, a:b])

    q_ref[...] = seg(0, GLA_DK) * (GLA_DKH ** -0.5)
    k_ref[...] = seg(GLA_DK, 2 * GLA_DK)
    v_ref[...] = seg(2 * GLA_DK, 2 * GLA_DK + GLA_DV).astype(BF16)
    r_ref[...] = seg(2 * GLA_DK + GLA_DV, OD_QKVR)
    glr = seg(*OD_GLR).astype(BF16)
    z = _dot(glr, wg2_ref[...]) + bg_ref[...]
    gk_ref[...] = (-_softplus(-z)) * (1.0 / GLA_TAU)


def _odd_proj(h, g_pre, w_in, w_g2, b_g):
    T, D = h.shape
    tm = min(PROJ_TM, T)
    row = lambda i: (i, 0)
    fixed = lambda i: (0, 0)
    outs = [(GLA_DK, F32), (GLA_DK, F32), (GLA_DV, BF16), (GLA_DV, F32), (GLA_DK, F32)]
    return pl.pallas_call(
        _odd_proj_kernel,
        grid=(T // tm,),
        in_specs=[
            pl.BlockSpec((tm, D), row),
            pl.BlockSpec((1, D), fixed),
            pl.BlockSpec(w_in.shape, fixed),
            pl.BlockSpec(w_g2.shape, fixed),
            pl.BlockSpec((1, GLA_DK), fixed),
        ],
        out_specs=[pl.BlockSpec((tm, n), row) for n, _ in outs],
        out_shape=[jax.ShapeDtypeStruct((T, n), dt) for n, dt in outs],
        compiler_params=_cparams(("parallel",)),
        name="odd_proj",
    )(h, g_pre, w_in, w_g2, b_g)


GLA_TS = 256


def _gla_kernel(q_ref, k_ref, v_ref, gk_ref, r_ref, hn_ref, o_ref, st_ref):
    ts = q_ref.shape[1]
    C = GLA_CHUNK

    @pl.when(pl.program_id(1) == 0)
    def _():
        st_ref[...] = jnp.zeros(st_ref.shape, F32)

    ri = lax.broadcasted_iota(jnp.int32, (C, C), 0)
    ci = lax.broadcasted_iota(jnp.int32, (C, C), 1)
    tril = ri >= ci
    tril_f = tril.astype(F32)

    for c in range(ts // C):
        rows = slice(c * C, (c + 1) * C)
        for h in range(GLA_HEADS):
            ksl = slice(h * GLA_DKH, (h + 1) * GLA_DKH)
            vsl = slice(h * GLA_DVH, (h + 1) * GLA_DVH)
            G = jnp.dot(tril_f, gk_ref[0, rows, ksl], preferred_element_type=F32,
                        precision=lax.Precision.HIGHEST)
            g_last = G[C - 1:C, :]
            kc = k_ref[0, rows, ksl]
            qd = (q_ref[0, rows, ksl] * jnp.exp(G)).astype(BF16)
            k_inv = (kc * jnp.exp(-G)).astype(BF16)
            k_rem = (kc * jnp.exp(g_last - G)).astype(BF16)
            vh = v_ref[0, rows, vsl]
            att = jnp.where(tril, _dot_nt(qd, k_inv), 0.0).astype(BF16)
            st = st_ref[h]
            o = _dot(att, vh) + _dot_nt(qd, st.astype(BF16))
            st_ref[h] = st * jnp.exp(g_last) + _dot_tn(vh, k_rem)
            on = _rms(o, hn_ref[...])
            rr = r_ref[0, rows, vsl]
            o_ref[0, rows, vsl] = (on * (rr * _sigmoid(rr))).astype(o_ref.dtype)


def _gla(q, k, v, gk, r, head_norm):
    B, S, _ = q.shape
    ts = min(GLA_TS, S)
    blk = lambda b, s: (b, s, 0)
    return pl.pallas_call(
        _gla_kernel,
        grid=(B, S // ts),
        in_specs=[
            pl.BlockSpec((1, ts, GLA_DK), blk),
            pl.BlockSpec((1, ts, GLA_DK), blk),
            pl.BlockSpec((1, ts, GLA_DV), blk),
            pl.BlockSpec((1, ts, GLA_DK), blk),
            pl.BlockSpec((1, ts, GLA_DV), blk),
            pl.BlockSpec((1, GLA_DVH), lambda b, s: (0, 0)),
        ],
        out_specs=pl.BlockSpec((1, ts, GLA_DV), blk),
        out_shape=jax.ShapeDtypeStruct((B, S, GLA_DV), BF16),
        scratch_shapes=[pltpu.VMEM((GLA_HEADS, GLA_DVH, GLA_DKH), F32)],
        compiler_params=_cparams(("parallel", "arbitrary")),
        name="gla",
    )(q, k, v, gk, r, head_norm)


def _xa_kv_kernel(mem_ref, g_ref, w_ref, k_ref, v_ref):
    mn = _rms(mem_ref[0], g_ref[...]).astype(BF16)
    k_ref[0] = (_dot(mn, w_ref[:, :D_MODEL]) * (XA_HEAD_DIM ** -0.5)).astype(BF16)
    v_ref[0] = _dot(mn, w_ref[:, D_MODEL:]).astype(BF16)


def _xa_kv(mem, g_mem, w_kv):
    B, M, D = mem.shape
    blk = lambda b: (b, 0, 0)
    return pl.pallas_call(
        _xa_kv_kernel,
        grid=(B,),
        in_specs=[pl.BlockSpec((1, M, D), blk), pl.BlockSpec((1, D), lambda b: (0, 0)),
                  pl.BlockSpec(w_kv.shape, lambda b: (0, 0))],
        out_specs=[pl.BlockSpec((1, M, D), blk), pl.BlockSpec((1, M, D), blk)],
        out_shape=[jax.ShapeDtypeStruct((B, M, D), BF16)] * 2,
        compiler_params=_cparams(("parallel",)),
        name="xa_kv",
    )(mem, g_mem, w_kv)


XA_TM = 512


def _xa_kernel(h_ref, gpre_ref, gpost_ref, wq_ref, k_ref, v_ref, wo_ref, o_ref):
    x = h_ref[0]
    xn = _rms(x, gpre_ref[...]).astype(BF16)
    q = _dot(xn, wq_ref[...]).astype(BF16)
    heads = []
    for h in range(XA_HEADS):
        sl = slice(h * XA_HEAD_DIM, (h + 1) * XA_HEAD_DIM)
        s = _dot_nt(q[:, sl], k_ref[0, :, sl])
        p = jnp.exp(s - jnp.max(s, axis=-1, keepdims=True))
        oh = _dot(p.astype(BF16), v_ref[0, :, sl]) / jnp.sum(p, axis=-1, keepdims=True)
        heads.append(oh.astype(BF16))
    c = _dot(jnp.concatenate(heads, axis=-1), wo_ref[...])
    o_ref[0] = x + _rms(c, gpost_ref[...])


def _xa(h, g_pre, g_post, w_q, kx, vx, w_o):
    B, S, D = h.shape
    M = kx.shape[1]
    tm = min(XA_TM, S)
    blk = lambda b, i: (b, i, 0)
    fixed = lambda b, i: (0, 0)
    return pl.pallas_call(
        _xa_kernel,
        grid=(B, S // tm),
        in_specs=[
            pl.BlockSpec((1, tm, D), blk),
            pl.BlockSpec((1, D), fixed),
            pl.BlockSpec((1, D), fixed),
            pl.BlockSpec((D, D), fixed),
            pl.BlockSpec((1, M, D), lambda b, i: (b, 0, 0)),
            pl.BlockSpec((1, M, D), lambda b, i: (b, 0, 0)),
            pl.BlockSpec((D, D), fixed),
        ],
        out_specs=pl.BlockSpec((1, tm, D), blk),
        out_shape=jax.ShapeDtypeStruct((B, S, D), F32),
        compiler_params=_cparams(("parallel", "parallel")),
        name="xa",
    )(h, g_pre, g_post, w_q, kx, vx, w_o)


def _block_diag(w):
    G, n, _ = w.shape
    eye = jnp.eye(G, dtype=w.dtype)
    return (eye[:, None, :, None] * w[:, :, None, :]).reshape(G * n, G * n)


def _even_w_in(w):
    ki = w[:, 1280:1344]
    pad = jnp.zeros((w.shape[0], LANES - IDX_HEADS), w.dtype)
    return jnp.concatenate([w[:, :1280], ki, ki, w[:, 1352:2376], w[:, 1344:1352], pad], axis=1).astype(BF16)


def _odd_w_in(w):
    pad = jnp.zeros((w.shape[0], LANES - GLA_GATE_RANK), w.dtype)
    return jnp.concatenate([w[:, :2048], w[:, 2064:3088], w[:, 2048:2064], pad], axis=1).astype(BF16)


def kernel(x, mem, norms, ffn_w_gu, ffn_w_down, xa_w_q, xa_w_kv, xa_w_o, ev_w_in, ev_kv_norm, ev_w_uk, ev_w_uv, ev_conv_w, ev_conv_b, ev_w_ra, ev_b_ra, ev_w_ri, ev_b_ri, ev_lam, ev_w_out, od_w_in, od_w_g2, od_b_g, od_head_norm, od_w_out):
    B, S, D = x.shape
    T = B * S
    depth = norms.shape[0]
    h = x.reshape(T, D)

    def gain(layer, idx):
        return norms[layer, idx][None, :]

    for layer in range(depth):
        h = _ffn(h, gain(layer, N_FFN1_PRE), gain(layer, N_FFN1_POST),
                 ffn_w_gu[layer, 0].astype(BF16), ffn_w_down[layer, 0].astype(BF16))

        if layer % 2 == 0:
            e = layer // 2
            qt, k, vt, qit, ki2, wit, gate, xb = _even_proj(
                h, gain(layer, N_MIX_PRE), _even_w_in(ev_w_in[e]), ev_kv_norm[e][None, :],
                ev_w_uk[e].astype(BF16), ev_w_uv[e].astype(BF16))
            r3 = lambda a: a.reshape(B, S, a.shape[-1])
            a_out = _dsa(qt, k, vt, qit, ki2, wit, B)
            b_out = _rglru(r3(xb), r3(gate), ev_conv_w[e], ev_conv_b[e][None, :],
                           _block_diag(ev_w_ra[e]).astype(BF16), ev_b_ra[e].reshape(1, B_WIDTH),
                           _block_diag(ev_w_ri[e]).astype(BF16), ev_b_ri[e].reshape(1, B_WIDTH),
                           ev_lam[e][None, :])
            parts = [a_out, b_out.reshape(T, B_WIDTH)]
            w_out = ev_w_out[e].astype(BF16)
        else:
            o = layer // 2
            w_g2 = jnp.concatenate(
                [od_w_g2[o], jnp.zeros((LANES - GLA_GATE_RANK, GLA_DK), od_w_g2.dtype)], axis=0).astype(BF16)
            q, k, v, r, gk = _odd_proj(h, gain(layer, N_MIX_PRE), _odd_w_in(od_w_in[o]), w_g2,
                                       od_b_g[o][None, :])
            r3 = lambda a: a.reshape(B, S, a.shape[-1])
            g_out = _gla(r3(q), r3(k), r3(v), r3(gk), r3(r), od_head_norm[o][None, :])
            parts = [g_out.reshape(T, GLA_DV)]
            w_out = od_w_out[o].astype(BF16)
        h = _out_proj(h, gain(layer, N_MIX_POST), parts, w_out)

        kx, vx = _xa_kv(mem, gain(layer, N_MEM_NORM), xa_w_kv[layer].astype(BF16))
        h = _xa(h.reshape(B, S, D), gain(layer, N_XA_PRE), gain(layer, N_XA_POST),
                xa_w_q[layer].astype(BF16), kx, vx, xa_w_o[layer].astype(BF16)).reshape(T, D)

        h = _ffn(h, gain(layer, N_FFN2_PRE), gain(layer, N_FFN2_POST),
                 ffn_w_gu[layer, 1].astype(BF16), ffn_w_down[layer, 1].astype(BF16))
    return h.reshape(B, S, D)
```

```python
import functools

import jax
import jax.numpy as jnp
from jax import lax
from jax.experimental import pallas as pl
from jax.experimental.pallas import tpu as pltpu

F32 = jnp.float32
BF16 = jnp.bfloat16

EPS = 1e-6
D_MODEL = 1024
D_FF = 2816
XA_HEADS = 4
XA_HEAD_DIM = D_MODEL // XA_HEADS
A_HEADS = 8
A_HEAD_DIM = 64
A_WIDTH = A_HEADS * A_HEAD_DIM
KV_RANK = 256
IDX_HEADS = 8
IDX_DIM = 64
TOPK_MAX = 256
B_WIDTH = D_MODEL - A_WIDTH
B_BLOCKS = 8
B_BLOCK_DIM = B_WIDTH // B_BLOCKS
CONV_W = 4
LRU_C = 8.0
GLA_HEADS = 4
GLA_DK = D_MODEL // 2
GLA_DV = D_MODEL
GLA_DKH = GLA_DK // GLA_HEADS
GLA_DVH = GLA_DV // GLA_HEADS
GLA_GATE_RANK = 16
GLA_TAU = 16.0
GLA_CHUNK = 64
(N_FFN1_PRE, N_FFN1_POST, N_MIX_PRE, N_MIX_POST, N_XA_PRE, N_XA_POST, N_MEM_NORM,
 N_FFN2_PRE, N_FFN2_POST) = range(9)

LANES = 128
SUBLANES = 8
VMEM_LIMIT = 48 * 1024 * 1024

NEG_BIG = -1e30
LOG2E = 1.4426950408889634
INT_MIN = -2 ** 31
POS_INF_CODE = 0x7F800000
NEG_INF_CODE = -0x7F800001


def _cparams(sem):
    return pltpu.CompilerParams(dimension_semantics=sem, vmem_limit_bytes=VMEM_LIMIT)


def _rms(x, g):
    return x * lax.rsqrt(jnp.mean(x * x, axis=-1, keepdims=True) + EPS) * g


def _dot(a, b):
    return jnp.dot(a, b, preferred_element_type=F32)


def _dot_nt(a, b):
    return lax.dot_general(a, b, (((1,), (1,)), ((), ())), preferred_element_type=F32)


def _dot_tn(a, b):
    return lax.dot_general(a, b, (((0,), (0,)), ((), ())), preferred_element_type=F32)


def _sigmoid(x):
    return 1.0 / (1.0 + jnp.exp(-x))


FFN_TM = 512
FFN_TF = 256


def _ffn_kernel(h_ref, gpre_ref, gpost_ref, wgu_ref, wd_ref, o_ref, act_ref):
    F = wd_ref.shape[0]
    x = h_ref[...]
    xn = _rms(x, gpre_ref[...]).astype(BF16)
    for c in range(F // FFN_TF):
        g = _dot(xn, wgu_ref[:, c * FFN_TF:(c + 1) * FFN_TF])
        u = _dot(xn, wgu_ref[:, F + c * FFN_TF:F + (c + 1) * FFN_TF])
        act_ref[:, c * FFN_TF:(c + 1) * FFN_TF] = (g * _sigmoid(g) * u).astype(BF16)
    f = _dot(act_ref[...], wd_ref[...])
    o_ref[...] = x + 0.5 * _rms(f, gpost_ref[...])


def _resident(shape):
    return pl.BlockSpec(shape, lambda *_: (0,) * len(shape), pipeline_mode=pl.Buffered(1))


def _ffn(h, g_pre, g_post, w_gu, w_down):
    T, D = h.shape
    F = w_down.shape[0]
    tm = min(FFN_TM, T)
    return pl.pallas_call(
        _ffn_kernel,
        grid=(T // tm,),
        in_specs=[
            pl.BlockSpec((tm, D), lambda i: (i, 0)),
            pl.BlockSpec((1, D), lambda i: (0, 0)),
            pl.BlockSpec((1, D), lambda i: (0, 0)),
            _resident(w_gu.shape),
            _resident(w_down.shape),
        ],
        out_specs=pl.BlockSpec((tm, D), lambda i: (i, 0)),
        out_shape=jax.ShapeDtypeStruct((T, D), F32),
        scratch_shapes=[pltpu.VMEM((tm, F), BF16)],
        compiler_params=_cparams(("parallel",)),
        name="ffn",
    )(h, g_pre, g_post, w_gu, w_down)


PROJ_TM = 512


def _out_proj_kernel(*refs, offsets, scale):
    h_ref, g_ref = refs[0], refs[1]
    part_refs = refs[2:2 + len(offsets)]
    w_ref = refs[2 + len(offsets)]
    o_ref = refs[3 + len(offsets)]
    m = None
    for p_ref, off in zip(part_refs, offsets):
        kk = p_ref.shape[-1]
        term = _dot(p_ref[...], w_ref[off:off + kk, :])
        m = term if m is None else m + term
    o_ref[...] = h_ref[...] + scale * _rms(m, g_ref[...])


def _out_proj(h, g_post, parts, w_out, scale=1.0):
    T, D = h.shape
    tm = min(PROJ_TM, T)
    offsets, off = [], 0
    for p in parts:
        offsets.append(off)
        off += p.shape[-1]
    in_specs = [pl.BlockSpec((tm, D), lambda i: (i, 0)), pl.BlockSpec((1, D), lambda i: (0, 0))]
    in_specs += [pl.BlockSpec((tm, p.shape[-1]), lambda i: (i, 0)) for p in parts]
    in_specs += [pl.BlockSpec(w_out.shape, lambda i: (0, 0))]
    return pl.pallas_call(
        functools.partial(_out_proj_kernel, offsets=tuple(offsets), scale=scale),
        grid=(T // tm,),
        in_specs=in_specs,
        out_specs=pl.BlockSpec((tm, D), lambda i: (i, 0)),
        out_shape=jax.ShapeDtypeStruct((T, D), F32),
        compiler_params=_cparams(("parallel",)),
        name="out_proj",
    )(h, g_post, *parts, w_out)


EV_Q = (0, 512)
EV_CKV = (512, 768)
EV_QI = (768, 1280)
EV_KI2 = (1280, 1408)
EV_GATE = (1408, 1920)
EV_XB = (1920, 2432)
EV_WI = (2432, 2560)
EV_COLS = 2560


DSA_TQ = 256
DSA_CK = 512


def _even_proj_kernel(h_ref, g_ref, w_ref, kvn_ref, wuk_ref, wuv_ref,
                      qt_ref, k_ref, vt_ref, qit_ref, ki_ref, wit_ref, gate_ref, xb_ref):
    xn = _rms(h_ref[...], g_ref[...]).astype(BF16)

    def seg(ab):
        return _dot(xn, w_ref[:, ab[0]:ab[1]])

    qt_ref[...] = (seg(EV_Q) * (A_HEAD_DIM ** -0.5 * LOG2E)).T.astype(BF16)
    ckv = _rms(seg(EV_CKV), kvn_ref[...]).astype(BF16)
    k_ref[...] = _dot(ckv, wuk_ref[...]).astype(BF16)
    vt_ref[0] = _dot(ckv, wuv_ref[...]).astype(BF16).T
    qit_ref[...] = (seg(EV_QI) * (IDX_DIM ** -0.5)).T.astype(BF16)
    ki_ref[...] = seg(EV_KI2).astype(BF16)
    wit_ref[...] = (seg(EV_WI) * (IDX_HEADS ** -0.5)).T[:IDX_HEADS, :]
    gate_ref[...] = seg(EV_GATE)
    xb_ref[...] = seg(EV_XB)


def _even_proj(h, g_pre, w_in, kv_norm, w_uk, w_uv):
    T, D = h.shape
    tm = DSA_CK
    row = lambda i: (i, 0)
    col = lambda i: (0, i)
    fixed = lambda i: (0, 0)
    out_specs = [
        pl.BlockSpec((A_WIDTH, tm), col),
        pl.BlockSpec((tm, A_WIDTH), row),
        pl.BlockSpec((1, A_WIDTH, tm), lambda i: (i, 0, 0)),
        pl.BlockSpec((IDX_HEADS * IDX_DIM, tm), col),
        pl.BlockSpec((tm, 2 * IDX_DIM), row),
        pl.BlockSpec((IDX_HEADS, tm), col),
        pl.BlockSpec((tm, B_WIDTH), row),
        pl.BlockSpec((tm, B_WIDTH), row),
    ]
    out_shape = [
        jax.ShapeDtypeStruct((A_WIDTH, T), BF16),
        jax.ShapeDtypeStruct((T, A_WIDTH), BF16),
        jax.ShapeDtypeStruct((T // tm, A_WIDTH, tm), BF16),
        jax.ShapeDtypeStruct((IDX_HEADS * IDX_DIM, T), BF16),
        jax.ShapeDtypeStruct((T, 2 * IDX_DIM), BF16),
        jax.ShapeDtypeStruct((IDX_HEADS, T), F32),
        jax.ShapeDtypeStruct((T, B_WIDTH), F32),
        jax.ShapeDtypeStruct((T, B_WIDTH), F32),
    ]
    return pl.pallas_call(
        _even_proj_kernel,
        grid=(T // tm,),
        in_specs=[
            pl.BlockSpec((tm, D), row),
            pl.BlockSpec((1, D), fixed),
            pl.BlockSpec(w_in.shape, fixed),
            pl.BlockSpec((1, KV_RANK), fixed),
            pl.BlockSpec(w_uk.shape, fixed),
            pl.BlockSpec(w_uv.shape, fixed),
        ],
        out_specs=out_specs,
        out_shape=out_shape,
        compiler_params=_cparams(("parallel",)),
        name="even_proj",
    )(h, g_pre, w_in, kv_norm, w_uk, w_uv)


def _col_reduce(x, op):
    rows, n = x.shape
    part = x.reshape(rows // SUBLANES, SUBLANES, n)
    part = jnp.max(part, axis=0) if op == "max" else jnp.sum(part, axis=0)
    return (jnp.max(part, axis=0, keepdims=True) if op == "max"
            else jnp.sum(part, axis=0, keepdims=True))


def _dsa_kernel(qt_ref, qit_ref, wit_ref, k_ref, vt_ref, ki_ref, tri_ref, o_ref,
                sc_ref, bias_ref, s_ref, qm_ref, qim_ref, m_ref, l_ref, acc_ref, need_ref, seen_ref,
                *, top_k):
    TQ, CK = DSA_TQ, DSA_CK
    j = pl.program_id(1)
    q0 = j * TQ
    nkc = (q0 + TQ + CK - 1) // CK

    low_half = lax.broadcasted_iota(jnp.int32, (LANES, TQ), 0) < A_HEAD_DIM
    for h in range(A_HEADS):
        pr = slice((h // 2) * LANES, (h // 2 + 1) * LANES)
        keep = low_half if h % 2 == 0 else jnp.logical_not(low_half)
        qm_ref[h] = jnp.where(keep, qt_ref[pr, :], jnp.zeros((), BF16))
        qim_ref[h] = jnp.where(keep, qit_ref[pr, :], jnp.zeros((), BF16))

    key_iota = lax.broadcasted_iota(jnp.int32, (CK, TQ), 0)
    q_pos = q0 + lax.broadcasted_iota(jnp.int32, (CK, TQ), 1)

    def score_chunk(c, carry):
        base = pl.multiple_of(c * CK, CK)
        kic = ki_ref[pl.ds(base, CK), :]
        acc = jnp.zeros((CK, TQ), F32)
        for h in range(IDX_HEADS):
            acc = acc + jnp.maximum(_dot(kic, qim_ref[h]), 0.0) * wit_ref[h:h + 1, :]
        sc_ref[c] = jnp.where(base + key_iota <= q_pos, acc, -jnp.inf)
        return carry

    lax.fori_loop(0, nkc, score_chunk, 0)

    def count(pred_fn):
        def body(c, cnt):
            hit = pred_fn(sc_ref[c]).astype(jnp.int32)
            return cnt + jnp.sum(hit.reshape(CK // SUBLANES, SUBLANES, TQ), axis=0)
        cnt = lax.fori_loop(0, nkc, body, jnp.zeros((SUBLANES, TQ), jnp.int32))
        return jnp.sum(cnt, axis=0, keepdims=True)

    def code_to_float(code):
        code = jnp.clip(code, NEG_INF_CODE, POS_INF_CODE)
        return lax.bitcast_convert_type(code ^ ((code >> 31) & jnp.int32(0x7FFFFFFF)), F32)

    def bit_pass(i, carry):
        code, n_at = carry
        cand = code + lax.shift_left(jnp.int32(1), 31 - i)
        cand_f = code_to_float(cand)
        n_ge = count(lambda sc: sc >= cand_f)
        take = n_ge >= top_k
        return jnp.where(take, cand, code), jnp.where(take, n_ge, n_at)

    code, n_at = lax.fori_loop(
        0, 32, bit_pass,
        (jnp.full((1, TQ), INT_MIN, jnp.int32), jnp.full((1, TQ), nkc * CK, jnp.int32)))
    thr = code_to_float(code)
    finite = thr > -jnp.inf
    has_ties = jnp.max(jnp.where(finite & (n_at > top_k), 1, 0)) > 0
    thr_sel = jnp.where(finite, thr, jnp.finfo(F32).min)

    m_ref[...] = jnp.full(m_ref.shape, -jnp.inf, F32)
    l_ref[...] = jnp.zeros(l_ref.shape, F32)
    acc_ref[...] = jnp.zeros(acc_ref.shape, F32)
    need_ref[...] = jnp.zeros(need_ref.shape, F32)
    seen_ref[...] = jnp.zeros(seen_ref.shape, F32)

    @pl.when(has_ties)
    def _():
        n_gt = count(lambda sc: sc > thr)
        need_ref[...] = jnp.where(finite, (top_k - n_gt).astype(F32), 0.0)

    def attend_chunk(c, carry):
        base = pl.multiple_of(c * CK, CK)

        @pl.when(jnp.logical_not(has_ties))
        def _():
            bias_ref[...] = jnp.where(sc_ref[c] >= thr_sel, 0.0, NEG_BIG)

        @pl.when(has_ties)
        def _():
            sc = sc_ref[c]
            eq = sc == thr
            rank = seen_ref[...] + _dot(tri_ref[...], eq.astype(BF16))
            sel = (sc > thr) | (eq & (rank <= need_ref[...]))
            bias_ref[...] = jnp.where(sel, 0.0, NEG_BIG)
            seen_ref[...] += _col_reduce(eq.astype(F32), "sum")

        cmax = []
        for h in range(A_HEADS):
            kp = k_ref[pl.ds(base, CK), (h // 2) * LANES:(h // 2 + 1) * LANES]
            s = _dot(kp, qm_ref[h]) + bias_ref[...]
            s_ref[h] = s
            cmax.append(_col_reduce(s, "max"))
        for h in range(A_HEADS):
            m_old = m_ref[h]
            m_new = jnp.maximum(m_old, cmax[h])
            alpha = jnp.exp2(m_old - m_new)
            p = jnp.exp2(s_ref[h] - m_new)
            l_ref[h] = alpha * l_ref[h] + _col_reduce(p, "sum")
            m_ref[h] = m_new
            vth = vt_ref[c, h * A_HEAD_DIM:(h + 1) * A_HEAD_DIM, :]
            acc_ref[h] = acc_ref[h] * alpha + _dot(vth, p.astype(BF16))
        return carry

    lax.fori_loop(0, nkc, attend_chunk, 0)

    out_t = jnp.concatenate([acc_ref[h] / l_ref[h] for h in range(A_HEADS)], axis=0)
    o_ref[...] = out_t.T.astype(o_ref.dtype)


def _dsa(qt, k, vt, qit, ki2, wit, B):
    T = k.shape[0]
    S = T // B
    TQ, CK = DSA_TQ, DSA_CK
    nc, nq = S // CK, S // TQ
    top_k = min(TOPK_MAX, S // 4)
    tri = (jnp.arange(CK)[:, None] >= jnp.arange(CK)[None, :]).astype(BF16)
    qcol = lambda b, j: (0, b * nq + j)
    return pl.pallas_call(
        functools.partial(_dsa_kernel, top_k=top_k),
        grid=(B, nq),
        in_specs=[
            pl.BlockSpec((A_WIDTH, TQ), qcol),
            pl.BlockSpec((IDX_HEADS * IDX_DIM, TQ), qcol),
            pl.BlockSpec((IDX_HEADS, TQ), qcol),
            pl.BlockSpec((S, A_WIDTH), lambda b, j: (b, 0)),
            pl.BlockSpec((nc, A_WIDTH, CK), lambda b, j: (b, 0, 0)),
            pl.BlockSpec((S, 2 * IDX_DIM), lambda b, j: (b, 0)),
            pl.BlockSpec((CK, CK), lambda b, j: (0, 0)),
        ],
        out_specs=pl.BlockSpec((TQ, A_WIDTH), lambda b, j: (b * nq + j, 0)),
        out_shape=jax.ShapeDtypeStruct((T, A_WIDTH), BF16),
        scratch_shapes=[
            pltpu.VMEM((nc, CK, TQ), F32),
            pltpu.VMEM((CK, TQ), F32),
            pltpu.VMEM((A_HEADS, CK, TQ), F32),
            pltpu.VMEM((A_HEADS, LANES, TQ), BF16),
            pltpu.VMEM((IDX_HEADS, LANES, TQ), BF16),
            pltpu.VMEM((A_HEADS, 1, TQ), F32),
            pltpu.VMEM((A_HEADS, 1, TQ), F32),
            pltpu.VMEM((A_HEADS, A_HEAD_DIM, TQ), F32),
            pltpu.VMEM((1, TQ), F32),
            pltpu.VMEM((1, TQ), F32),
        ],
        compiler_params=_cparams(("parallel", "arbitrary")),
        name="dsa",
    )(qt, qit, wit, k, vt, ki2, tri)


LRU_TS = 512
HALO = SUBLANES


def _softplus(x):
    return jnp.maximum(x, 0.0) + jnp.log1p(jnp.exp(-jnp.abs(x)))


def _gelu_tanh(x):
    return 0.5 * x * (1.0 + jnp.tanh(0.7978845608028654 * (x + 0.044715 * (x * x * x))))


def _rglru_kernel(xb_ref, gate_ref, cw_ref, cb_ref, wra_ref, bra_ref, wri_ref, bri_ref, lam_ref,
                  o_ref, xs_ref, a_ref, b_ref, hc_ref):
    ts = xb_ref.shape[1]
    C = xb_ref.shape[2]

    @pl.when(pl.program_id(1) == 0)
    def _():
        xs_ref[0:HALO, :] = jnp.zeros((HALO, C), F32)
        hc_ref[...] = jnp.zeros(hc_ref.shape, F32)

    x = xb_ref[0]
    xs_ref[HALO:HALO + ts, :] = x
    xc = cb_ref[...] + jnp.zeros((ts, C), F32)
    for kk in range(CONV_W):
        off = HALO - (CONV_W - 1) + kk
        xc = xc + cw_ref[kk:kk + 1, :] * xs_ref[off:off + ts, :]
    xs_ref[0:HALO, :] = x[ts - HALO:ts, :]

    xcb = xc.astype(BF16)
    r = _sigmoid(_dot(xcb, wra_ref[...]) + bra_ref[...])
    gi = _sigmoid(_dot(xcb, wri_ref[...]) + bri_ref[...])
    log_a = (-LRU_C) * r * _softplus(-lam_ref[...])
    a = jnp.exp(log_a)
    a_ref[...] = a
    b_ref[...] = jnp.sqrt(-jnp.tanh(log_a) * (1.0 + a * a)) * (gi * xc)

    row = lax.broadcasted_iota(jnp.int32, (SUBLANES, C), 0)

    def group(g, carry):
        r0 = pl.multiple_of(g * SUBLANES, SUBLANES)
        av = a_ref[pl.ds(r0, SUBLANES), :]
        bv = b_ref[pl.ds(r0, SUBLANES), :]
        for sh in (1, 2, 4):
            a_sh = pltpu.roll(av, sh, axis=0)
            b_sh = pltpu.roll(bv, sh, axis=0)
            ok = row >= sh
            bv = jnp.where(ok, av * b_sh + bv, bv)
            av = jnp.where(ok, av * a_sh, av)
        h8 = av * carry + bv
        a_ref[pl.ds(r0, SUBLANES), :] = h8
        return jnp.broadcast_to(h8[SUBLANES - 1:SUBLANES, :], (SUBLANES, C))

    hc_ref[...] = lax.fori_loop(0, ts // SUBLANES, group, hc_ref[...])
    o_ref[0] = (a_ref[...] * _gelu_tanh(gate_ref[0])).astype(o_ref.dtype)


def _rglru(xb, gate, conv_w, conv_b, w_ra, b_ra, w_ri, b_ri, lam):
    B, S, C = xb.shape
    ts = min(LRU_TS, S)
    blk = lambda b, s: (b, s, 0)
    fixed = lambda b, s: (0, 0)
    return pl.pallas_call(
        _rglru_kernel,
        grid=(B, S // ts),
        in_specs=[
            pl.BlockSpec((1, ts, C), blk),
            pl.BlockSpec((1, ts, C), blk),
            pl.BlockSpec((CONV_W, C), fixed),
            pl.BlockSpec((1, C), fixed),
            pl.BlockSpec((C, C), fixed),
            pl.BlockSpec((1, C), fixed),
            pl.BlockSpec((C, C), fixed),
            pl.BlockSpec((1, C), fixed),
            pl.BlockSpec((1, C), fixed),
        ],
        out_specs=pl.BlockSpec((1, ts, C), blk),
        out_shape=jax.ShapeDtypeStruct((B, S, C), BF16),
        scratch_shapes=[
            pltpu.VMEM((HALO + ts, C), F32),
            pltpu.VMEM((ts, C), F32),
            pltpu.VMEM((ts, C), F32),
            pltpu.VMEM((SUBLANES, C), F32),
        ],
        compiler_params=_cparams(("parallel", "arbitrary")),
        name="rglru",
    )(xb, gate, conv_w, conv_b, w_ra, b_ra, w_ri, b_ri, lam)


OD_QKVR = 3072
OD_GLR = (3072, 3200)
OD_COLS = 3200


def _odd_proj_kernel(h_ref, g_ref, w_ref, wg2_ref, bg_ref, q_ref, k_ref, v_ref, r_ref, gk_ref):
    xn = _rms(h_ref[...], g_ref[...]).astype(BF16)

    def seg(a, b):
        return _dot(xn, w_ref[:, a:b])

    q_ref[...] = seg(0, GLA_DK) * (GLA_DKH ** -0.5)
    k_ref[...] = seg(GLA_DK, 2 * GLA_DK)
    v_ref[...] = seg(2 * GLA_DK, 2 * GLA_DK + GLA_DV).astype(BF16)
    r_ref[...] = seg(2 * GLA_DK + GLA_DV, OD_QKVR)
    glr = seg(*OD_GLR).astype(BF16)
    z = _dot(glr, wg2_ref[...]) + bg_ref[...]
    gk_ref[...] = (-_softplus(-z)) * (1.0 / GLA_TAU)


def _odd_proj(h, g_pre, w_in, w_g2, b_g):
    T, D = h.shape
    tm = min(PROJ_TM, T)
    row = lambda i: (i, 0)
    fixed = lambda i: (0, 0)
    outs = [(GLA_DK, F32), (GLA_DK, F32), (GLA_DV, BF16), (GLA_DV, F32), (GLA_DK, F32)]
    return pl.pallas_call(
        _odd_proj_kernel,
        grid=(T // tm,),
        in_specs=[
            pl.BlockSpec((tm, D), row),
            pl.BlockSpec((1, D), fixed),
            pl.BlockSpec(w_in.shape, fixed),
            pl.BlockSpec(w_g2.shape, fixed),
            pl.BlockSpec((1, GLA_DK), fixed),
        ],
        out_specs=[pl.BlockSpec((tm, n), row) for n, _ in outs],
        out_shape=[jax.ShapeDtypeStruct((T, n), dt) for n, dt in outs],
        compiler_params=_cparams(("parallel",)),
        name="odd_proj",
    )(h, g_pre, w_in, w_g2, b_g)


GLA_TS = 256


def _gla_kernel(q_ref, k_ref, v_ref, gk_ref, r_ref, hn_ref, tri_ref, o_ref,
                st_ref, qd_ref, oi_ref, u_ref, stb_ref):
    ts = q_ref.shape[1]
    C = GLA_CHUNK
    nch = ts // C

    @pl.when(pl.program_id(1) == 0)
    def _():
        st_ref[...] = jnp.zeros(st_ref.shape, F32)

    gk = gk_ref[0]
    g_hi = gk.astype(BF16)
    rem = gk - g_hi.astype(F32)
    g_mid = rem.astype(BF16)
    g_lo = (rem - g_mid.astype(F32)).astype(BF16)
    tri = tri_ref[...]
    G = _dot(tri, g_hi) + _dot(tri, g_mid) + _dot(tri, g_lo)

    kf = k_ref[0]
    qd_ref[...] = (q_ref[0] * jnp.exp(G)).astype(BF16)
    k_inv = (kf * jnp.exp(-G)).astype(BF16)
    g_last = [G[(c + 1) * C - 1:(c + 1) * C, :] for c in range(nch)]
    k_rem = jnp.concatenate(
        [kf[c * C:(c + 1) * C, :] * jnp.exp(g_last[c] - G[c * C:(c + 1) * C, :]) for c in range(nch)],
        axis=0).astype(BF16)

    ri = lax.broadcasted_iota(jnp.int32, (C, C), 0)
    ci = lax.broadcasted_iota(jnp.int32, (C, C), 1)
    tril = ri >= ci

    for c in range(nch):
        rows = slice(c * C, (c + 1) * C)
        for h in range(GLA_HEADS):
            ksl = slice(h * GLA_DKH, (h + 1) * GLA_DKH)
            vsl = slice(h * GLA_DVH, (h + 1) * GLA_DVH)
            vh = v_ref[0, rows, vsl]
            att = jnp.where(tril, _dot_nt(qd_ref[rows, ksl], k_inv[rows, ksl]), 0.0).astype(BF16)
            oi_ref[rows, vsl] = _dot(att, vh)
            u_ref[c, h] = _dot_tn(vh, k_rem[rows, ksl])

    for h in range(GLA_HEADS):
        ksl = slice(h * GLA_DKH, (h + 1) * GLA_DKH)
        st = st_ref[h]
        for c in range(nch):
            stb_ref[c, h] = st.astype(BF16)
            st = st * jnp.exp(g_last[c][:, ksl]) + u_ref[c, h]
        st_ref[h] = st

    for c in range(nch):
        rows = slice(c * C, (c + 1) * C)
        for h in range(GLA_HEADS):
            ksl = slice(h * GLA_DKH, (h + 1) * GLA_DKH)
            vsl = slice(h * GLA_DVH, (h + 1) * GLA_DVH)
            o = oi_ref[rows, vsl] + _dot_nt(qd_ref[rows, ksl], stb_ref[c, h])
            on = _rms(o, hn_ref[...])
            rr = r_ref[0, rows, vsl]
            o_ref[0, rows, vsl] = (on * (rr * _sigmoid(rr))).astype(o_ref.dtype)


def _gla(q, k, v, gk, r, head_norm):
    B, S, _ = q.shape
    ts = min(GLA_TS, S)
    nch = ts // GLA_CHUNK
    pos = jnp.arange(ts)
    tri = ((pos[:, None] >= pos[None, :])
           & (pos[:, None] // GLA_CHUNK == pos[None, :] // GLA_CHUNK)).astype(BF16)
    blk = lambda b, s: (b, s, 0)
    return pl.pallas_call(
        _gla_kernel,
        grid=(B, S // ts),
        in_specs=[
            pl.BlockSpec((1, ts, GLA_DK), blk),
            pl.BlockSpec((1, ts, GLA_DK), blk),
            pl.BlockSpec((1, ts, GLA_DV), blk),
            pl.BlockSpec((1, ts, GLA_DK), blk),
            pl.BlockSpec((1, ts, GLA_DV), blk),
            pl.BlockSpec((1, GLA_DVH), lambda b, s: (0, 0)),
            pl.BlockSpec((ts, ts), lambda b, s: (0, 0)),
        ],
        out_specs=pl.BlockSpec((1, ts, GLA_DV), blk),
        out_shape=jax.ShapeDtypeStruct((B, S, GLA_DV), BF16),
        scratch_shapes=[
            pltpu.VMEM((GLA_HEADS, GLA_DVH, GLA_DKH), F32),
            pltpu.VMEM((ts, GLA_DK), BF16),
            pltpu.VMEM((ts, GLA_DV), F32),
            pltpu.VMEM((nch, GLA_HEADS, GLA_DVH, GLA_DKH), F32),
            pltpu.VMEM((nch, GLA_HEADS, GLA_DVH, GLA_DKH), BF16),
        ],
        compiler_params=_cparams(("parallel", "arbitrary")),
        name="gla",
    )(q, k, v, gk, r, head_norm, tri)


def _xa_kv_kernel(mem_ref, g_ref, w_ref, k_ref, v_ref):
    mn = _rms(mem_ref[0], g_ref[...]).astype(BF16)
    k_ref[0] = (_dot(mn, w_ref[:, :D_MODEL]) * (XA_HEAD_DIM ** -0.5)).astype(BF16)
    v_ref[0] = _dot(mn, w_ref[:, D_MODEL:]).astype(BF16)


def _xa_kv(mem, g_mem, w_kv):
    B, M, D = mem.shape
    blk = lambda b: (b, 0, 0)
    return pl.pallas_call(
        _xa_kv_kernel,
        grid=(B,),
        in_specs=[pl.BlockSpec((1, M, D), blk), pl.BlockSpec((1, D), lambda b: (0, 0)),
                  pl.BlockSpec(w_kv.shape, lambda b: (0, 0))],
        out_specs=[pl.BlockSpec((1, M, D), blk), pl.BlockSpec((1, M, D), blk)],
        out_shape=[jax.ShapeDtypeStruct((B, M, D), BF16)] * 2,
        compiler_params=_cparams(("parallel",)),
        name="xa_kv",
    )(mem, g_mem, w_kv)


XA_TM = 512


def _xa_kernel(h_ref, gpre_ref, gpost_ref, wq_ref, k_ref, v_ref, wo_ref, o_ref):
    x = h_ref[0]
    xn = _rms(x, gpre_ref[...]).astype(BF16)
    q = _dot(xn, wq_ref[...]).astype(BF16)
    heads = []
    for h in range(XA_HEADS):
        sl = slice(h * XA_HEAD_DIM, (h + 1) * XA_HEAD_DIM)
        s = _dot_nt(q[:, sl], k_ref[0, :, sl])
        p = jnp.exp(s - jnp.max(s, axis=-1, keepdims=True))
        oh = _dot(p.astype(BF16), v_ref[0, :, sl]) / jnp.sum(p, axis=-1, keepdims=True)
        heads.append(oh.astype(BF16))
    c = _dot(jnp.concatenate(heads, axis=-1), wo_ref[...])
    o_ref[0] = x + _rms(c, gpost_ref[...])


def _xa(h, g_pre, g_post, w_q, kx, vx, w_o):
    B, S, D = h.shape
    M = kx.shape[1]
    tm = min(XA_TM, S)
    blk = lambda b, i: (b, i, 0)
    fixed = lambda b, i: (0, 0)
    return pl.pallas_call(
        _xa_kernel,
        grid=(B, S // tm),
        in_specs=[
            pl.BlockSpec((1, tm, D), blk),
            pl.BlockSpec((1, D), fixed),
            pl.BlockSpec((1, D), fixed),
            pl.BlockSpec((D, D), fixed),
            pl.BlockSpec((1, M, D), lambda b, i: (b, 0, 0)),
            pl.BlockSpec((1, M, D), lambda b, i: (b, 0, 0)),
            pl.BlockSpec((D, D), fixed),
        ],
        out_specs=pl.BlockSpec((1, tm, D), blk),
        out_shape=jax.ShapeDtypeStruct((B, S, D), F32),
        compiler_params=_cparams(("parallel", "parallel")),
        name="xa",
    )(h, g_pre, g_post, w_q, kx, vx, w_o)


def _block_diag(w):
    G, n, _ = w.shape
    eye = jnp.eye(G, dtype=w.dtype)
    return (eye[:, None, :, None] * w[:, :, None, :]).reshape(G * n, G * n)


def _even_w_in(w):
    ki = w[:, 1280:1344]
    pad = jnp.zeros((w.shape[0], LANES - IDX_HEADS), w.dtype)
    return jnp.concatenate([w[:, :1280], ki, ki, w[:, 1352:2376], w[:, 1344:1352], pad], axis=1).astype(BF16)


def _odd_w_in(w):
    pad = jnp.zeros((w.shape[0], LANES - GLA_GATE_RANK), w.dtype)
    return jnp.concatenate([w[:, :2048], w[:, 2064:3088], w[:, 2048:2064], pad], axis=1).astype(BF16)


def kernel(x, mem, norms, ffn_w_gu, ffn_w_down, xa_w_q, xa_w_kv, xa_w_o, ev_w_in, ev_kv_norm, ev_w_uk, ev_w_uv, ev_conv_w, ev_conv_b, ev_w_ra, ev_b_ra, ev_w_ri, ev_b_ri, ev_lam, ev_w_out, od_w_in, od_w_g2, od_b_g, od_head_norm, od_w_out):
    B, S, D = x.shape
    T = B * S
    depth = norms.shape[0]
    h = x.reshape(T, D)

    def gain(layer, idx):
        return norms[layer, idx][None, :]

    for layer in range(depth):
        h = _ffn(h, gain(layer, N_FFN1_PRE), gain(layer, N_FFN1_POST),
                 ffn_w_gu[layer, 0].astype(BF16), ffn_w_down[layer, 0].astype(BF16))

        if layer % 2 == 0:
            e = layer // 2
            qt, k, vt, qit, ki2, wit, gate, xb = _even_proj(
                h, gain(layer, N_MIX_PRE), _even_w_in(ev_w_in[e]), ev_kv_norm[e][None, :],
                ev_w_uk[e].astype(BF16), ev_w_uv[e].astype(BF16))
            r3 = lambda a: a.reshape(B, S, a.shape[-1])
            a_out = _dsa(qt, k, vt, qit, ki2, wit, B)
            b_out = _rglru(r3(xb), r3(gate), ev_conv_w[e], ev_conv_b[e][None, :],
                           _block_diag(ev_w_ra[e]).astype(BF16), ev_b_ra[e].reshape(1, B_WIDTH),
                           _block_diag(ev_w_ri[e]).astype(BF16), ev_b_ri[e].reshape(1, B_WIDTH),
                           ev_lam[e][None, :])
            parts = [a_out, b_out.reshape(T, B_WIDTH)]
            w_out = ev_w_out[e].astype(BF16)
        else:
            o = layer // 2
            w_g2 = jnp.concatenate(
                [od_w_g2[o], jnp.zeros((LANES - GLA_GATE_RANK, GLA_DK), od_w_g2.dtype)], axis=0).astype(BF16)
            q, k, v, r, gk = _odd_proj(h, gain(layer, N_MIX_PRE), _odd_w_in(od_w_in[o]), w_g2,
                                       od_b_g[o][None, :])
            r3 = lambda a: a.reshape(B, S, a.shape[-1])
            g_out = _gla(r3(q), r3(k), r3(v), r3(gk), r3(r), od_head_norm[o][None, :])
            parts = [g_out.reshape(T, GLA_DV)]
            w_out = od_w_out[o].astype(BF16)
        h = _out_proj(h, gain(layer, N_MIX_POST), parts, w_out)

        kx, vx = _xa_kv(mem, gain(layer, N_MEM_NORM), xa_w_kv[layer].astype(BF16))
        h = _xa(h.reshape(B, S, D), gain(layer, N_XA_PRE), gain(layer, N_XA_POST),
                xa_w_q[layer].astype(BF16), kx, vx, xa_w_o[layer].astype(BF16)).reshape(T, D)

        h = _ffn(h, gain(layer, N_FFN2_PRE), gain(layer, N_FFN2_POST),
                 ffn_w_gu[layer, 1].astype(BF16), ffn_w_down[layer, 1].astype(BF16))
    return h.reshape(B, S, D)
```

```python
import functools

import jax
import jax.numpy as jnp
from jax import lax
from jax.experimental import pallas as pl
from jax.experimental.pallas import tpu as pltpu

F32 = jnp.float32
BF16 = jnp.bfloat16

EPS = 1e-6
D_MODEL = 1024
D_FF = 2816
XA_HEADS = 4
XA_HEAD_DIM = D_MODEL // XA_HEADS
A_HEADS = 8
A_HEAD_DIM = 64
A_WIDTH = A_HEADS * A_HEAD_DIM
KV_RANK = 256
IDX_HEADS = 8
IDX_DIM = 64
TOPK_MAX = 256
B_WIDTH = D_MODEL - A_WIDTH
B_BLOCKS = 8
B_BLOCK_DIM = B_WIDTH // B_BLOCKS
CONV_W = 4
LRU_C = 8.0
GLA_HEADS = 4
GLA_DK = D_MODEL // 2
GLA_DV = D_MODEL
GLA_DKH = GLA_DK // GLA_HEADS
GLA_DVH = GLA_DV // GLA_HEADS
GLA_GATE_RANK = 16
GLA_TAU = 16.0
GLA_CHUNK = 64
(N_FFN1_PRE, N_FFN1_POST, N_MIX_PRE, N_MIX_POST, N_XA_PRE, N_XA_POST, N_MEM_NORM,
 N_FFN2_PRE, N_FFN2_POST) = range(9)

LANES = 128
SUBLANES = 8
VMEM_LIMIT = 48 * 1024 * 1024

NEG_BIG = -1e30
LOG2E = 1.4426950408889634
INT_MIN = -2 ** 31
POS_INF_CODE = 0x7F800000
NEG_INF_CODE = -0x7F800001


def _cparams(sem):
    return pltpu.CompilerParams(dimension_semantics=sem, vmem_limit_bytes=VMEM_LIMIT)


def _rms(x, g):
    return x * lax.rsqrt(jnp.mean(x * x, axis=-1, keepdims=True) + EPS) * g


def _dot(a, b):
    return jnp.dot(a, b, preferred_element_type=F32)


def _dot_nt(a, b):
    return lax.dot_general(a, b, (((1,), (1,)), ((), ())), preferred_element_type=F32)


def _dot_tn(a, b):
    return lax.dot_general(a, b, (((0,), (0,)), ((), ())), preferred_element_type=F32)


def _sigmoid(x):
    return 1.0 / (1.0 + jnp.exp(-x))


FFN_TM = 512
FFN_TF = 256


def _ffn_kernel(h_ref, gpre_ref, gpost_ref, wgu_ref, wd_ref, o_ref, act_ref):
    F = wd_ref.shape[0]
    x = h_ref[...]
    xn = _rms(x, gpre_ref[...]).astype(BF16)
    for c in range(F // FFN_TF):
        g = _dot(xn, wgu_ref[:, c * FFN_TF:(c + 1) * FFN_TF])
        u = _dot(xn, wgu_ref[:, F + c * FFN_TF:F + (c + 1) * FFN_TF])
        act_ref[:, c * FFN_TF:(c + 1) * FFN_TF] = (g * _sigmoid(g) * u).astype(BF16)
    f = _dot(act_ref[...], wd_ref[...])
    o_ref[...] = x + 0.5 * _rms(f, gpost_ref[...])


def _resident(arr, lead=()):
    tail = arr.shape[len(lead):]
    index = tuple(lead) + (0,) * len(tail)
    return pl.BlockSpec((None,) * len(lead) + tail, lambda *_: index, pipeline_mode=pl.Buffered(1))


def _ffn(h, g_pre, g_post, w_gu, w_down, lead):
    T, D = h.shape
    F = w_down.shape[-2]
    tm = min(FFN_TM, T)
    return pl.pallas_call(
        _ffn_kernel,
        grid=(T // tm,),
        in_specs=[
            pl.BlockSpec((tm, D), lambda i: (i, 0)),
            pl.BlockSpec((1, D), lambda i: (0, 0)),
            pl.BlockSpec((1, D), lambda i: (0, 0)),
            _resident(w_gu, lead),
            _resident(w_down, lead),
        ],
        out_specs=pl.BlockSpec((tm, D), lambda i: (i, 0)),
        out_shape=jax.ShapeDtypeStruct((T, D), F32),
        scratch_shapes=[pltpu.VMEM((tm, F), BF16)],
        compiler_params=_cparams(("parallel",)),
        name="ffn",
    )(h, g_pre, g_post, w_gu, w_down)


PROJ_TM = 512


EV_Q = (0, 512)
EV_CKV = (512, 768)
EV_QI = (768, 1280)
EV_KI2 = (1280, 1408)
EV_GATE = (1408, 1920)
EV_XB = (1920, 2432)
EV_WI = (2432, 2560)
EV_COLS = 2560


DSA_TQ = 256
DSA_CK = 512


def _even_proj_kernel(h_ref, g_ref, w_ref, kvn_ref, wuk_ref, wuv_ref,
                      qt_ref, k_ref, vt_ref, qit_ref, ki_ref, wit_ref, gate_ref, xb_ref):
    xn = _rms(h_ref[...], g_ref[...]).astype(BF16)

    def seg(ab):
        return _dot(xn, w_ref[:, ab[0]:ab[1]])

    qt_ref[...] = (seg(EV_Q) * (A_HEAD_DIM ** -0.5 * LOG2E)).T.astype(BF16)
    ckv = _rms(seg(EV_CKV), kvn_ref[...]).astype(BF16)
    k_ref[...] = _dot(ckv, wuk_ref[...]).astype(BF16)
    vt_ref[0] = _dot(ckv, wuv_ref[...]).astype(BF16).T
    qit_ref[...] = (seg(EV_QI) * (IDX_DIM ** -0.5)).T.astype(BF16)
    ki_ref[...] = seg(EV_KI2).astype(BF16)
    wit_ref[...] = (seg(EV_WI) * (IDX_HEADS ** -0.5)).T[:IDX_HEADS, :]
    gate_ref[...] = seg(EV_GATE)
    xb_ref[...] = seg(EV_XB)


def _even_proj(h, g_pre, w_in, kv_norm, w_uk, w_uv):
    T, D = h.shape
    tm = DSA_CK
    row = lambda i: (i, 0)
    col = lambda i: (0, i)
    fixed = lambda i: (0, 0)
    out_specs = [
        pl.BlockSpec((A_WIDTH, tm), col),
        pl.BlockSpec((tm, A_WIDTH), row),
        pl.BlockSpec((1, A_WIDTH, tm), lambda i: (i, 0, 0)),
        pl.BlockSpec((IDX_HEADS * IDX_DIM, tm), col),
        pl.BlockSpec((tm, 2 * IDX_DIM), row),
        pl.BlockSpec((IDX_HEADS, tm), col),
        pl.BlockSpec((tm, B_WIDTH), row),
        pl.BlockSpec((tm, B_WIDTH), row),
    ]
    out_shape = [
        jax.ShapeDtypeStruct((A_WIDTH, T), BF16),
        jax.ShapeDtypeStruct((T, A_WIDTH), BF16),
        jax.ShapeDtypeStruct((T // tm, A_WIDTH, tm), BF16),
        jax.ShapeDtypeStruct((IDX_HEADS * IDX_DIM, T), BF16),
        jax.ShapeDtypeStruct((T, 2 * IDX_DIM), BF16),
        jax.ShapeDtypeStruct((IDX_HEADS, T), F32),
        jax.ShapeDtypeStruct((T, B_WIDTH), F32),
        jax.ShapeDtypeStruct((T, B_WIDTH), F32),
    ]
    return pl.pallas_call(
        _even_proj_kernel,
        grid=(T // tm,),
        in_specs=[
            pl.BlockSpec((tm, D), row),
            pl.BlockSpec((1, D), fixed),
            pl.BlockSpec(w_in.shape, fixed),
            pl.BlockSpec((1, KV_RANK), fixed),
            pl.BlockSpec(w_uk.shape, fixed),
            pl.BlockSpec(w_uv.shape, fixed),
        ],
        out_specs=out_specs,
        out_shape=out_shape,
        compiler_params=_cparams(("parallel",)),
        name="even_proj",
    )(h, g_pre, w_in, kv_norm, w_uk, w_uv)


ACC_ROWS = 4 * SUBLANES


def _col_partial(x, op):
    rows, n = x.shape
    part = x.reshape(rows // ACC_ROWS, ACC_ROWS, n)
    return jnp.max(part, axis=0) if op == "max" else jnp.sum(part, axis=0)


def _col_reduce(x, op):
    part = _col_partial(x, op)
    return (jnp.max(part, axis=0, keepdims=True) if op == "max"
            else jnp.sum(part, axis=0, keepdims=True))


def _dsa_kernel(qt_ref, qit_ref, wit_ref, k_ref, vt_ref, ki_ref, tri_ref, o_ref,
                sc_ref, bias_ref, s_ref, qm_ref, qim_ref, m_ref, l_ref, acc_ref, need_ref, seen_ref,
                *, top_k):
    TQ, CK = DSA_TQ, DSA_CK
    j = pl.program_id(1)
    q0 = j * TQ
    nkc = (q0 + TQ + CK - 1) // CK

    low_half = lax.broadcasted_iota(jnp.int32, (LANES, TQ), 0) < A_HEAD_DIM
    for h in range(A_HEADS):
        pr = slice((h // 2) * LANES, (h // 2 + 1) * LANES)
        keep = low_half if h % 2 == 0 else jnp.logical_not(low_half)
        qm_ref[h] = jnp.where(keep, qt_ref[pr, :], jnp.zeros((), BF16))
        qim_ref[h] = jnp.where(keep, qit_ref[pr, :], jnp.zeros((), BF16))

    key_iota = lax.broadcasted_iota(jnp.int32, (CK, TQ), 0)
    q_pos = q0 + lax.broadcasted_iota(jnp.int32, (CK, TQ), 1)

    def score_chunk(c, carry):
        base = pl.multiple_of(c * CK, CK)
        kic = ki_ref[pl.ds(base, CK), :]
        acc = jnp.zeros((CK, TQ), F32)
        for h in range(IDX_HEADS):
            acc = acc + jnp.maximum(_dot(kic, qim_ref[h]), 0.0) * wit_ref[h:h + 1, :]
        sc_ref[c] = jnp.where(base + key_iota <= q_pos, acc, -jnp.inf)
        return carry

    lax.fori_loop(0, nkc, score_chunk, 0)

    def count(pred_fn):
        def body(c, cnt):
            for g in range(CK // ACC_ROWS):
                hit = pred_fn(sc_ref[c, g * ACC_ROWS:(g + 1) * ACC_ROWS, :])
                cnt = jnp.where(hit, cnt + 1, cnt)
            return cnt
        cnt = lax.fori_loop(0, nkc, body, jnp.zeros((ACC_ROWS, TQ), jnp.int32))
        return jnp.sum(cnt, axis=0, keepdims=True)

    def code_to_float(code):
        code = jnp.clip(code, NEG_INF_CODE, POS_INF_CODE)
        return lax.bitcast_convert_type(code ^ ((code >> 31) & jnp.int32(0x7FFFFFFF)), F32)

    def bit_pass(i, carry):
        code, n_at = carry
        cand = code + lax.shift_left(jnp.int32(1), 31 - i)
        cand_f = code_to_float(cand)
        n_ge = count(lambda sc: sc >= cand_f)
        take = n_ge >= top_k
        return jnp.where(take, cand, code), jnp.where(take, n_ge, n_at)

    code, n_at = lax.fori_loop(
        0, 32, bit_pass,
        (jnp.full((1, TQ), INT_MIN, jnp.int32), jnp.full((1, TQ), nkc * CK, jnp.int32)))
    thr = code_to_float(code)
    finite = thr > -jnp.inf
    has_ties = jnp.max(jnp.where(finite & (n_at > top_k), 1, 0)) > 0
    thr_sel = jnp.where(finite, thr, jnp.finfo(F32).min)

    m_ref[...] = jnp.full(m_ref.shape, -jnp.inf, F32)
    l_ref[...] = jnp.zeros(l_ref.shape, F32)
    acc_ref[...] = jnp.zeros(acc_ref.shape, F32)
    need_ref[...] = jnp.zeros(need_ref.shape, F32)
    seen_ref[...] = jnp.zeros(seen_ref.shape, F32)

    @pl.when(has_ties)
    def _():
        n_gt = count(lambda sc: sc > thr)
        need_ref[...] = jnp.where(finite, (top_k - n_gt).astype(F32), 0.0)

    def attend_chunk(c, carry):
        base = pl.multiple_of(c * CK, CK)

        @pl.when(jnp.logical_not(has_ties))
        def _():
            bias_ref[...] = jnp.where(sc_ref[c] >= thr_sel, 0.0, NEG_BIG)

        @pl.when(has_ties)
        def _():
            sc = sc_ref[c]
            eq = sc == thr
            rank = seen_ref[...] + _dot(tri_ref[...], eq.astype(BF16))
            sel = (sc > thr) | (eq & (rank <= need_ref[...]))
            bias_ref[...] = jnp.where(sel, 0.0, NEG_BIG)
            seen_ref[...] += _col_reduce(eq.astype(F32), "sum")

        cmax = []
        for h in range(A_HEADS):
            kp = k_ref[pl.ds(base, CK), (h // 2) * LANES:(h // 2 + 1) * LANES]
            s = _dot(kp, qm_ref[h]) + bias_ref[...]
            s_ref[h] = s
            cmax.append(_col_reduce(s, "max"))
        for h in range(A_HEADS):
            m_old = m_ref[h]
            m_new = jnp.maximum(m_old, cmax[h])
            alpha = jnp.exp2(m_old - m_new)
            p = jnp.exp2(s_ref[h] - m_new)
            l_ref[h] = alpha * l_ref[h] + _col_reduce(p, "sum")
            m_ref[h] = m_new
            vth = vt_ref[c, h * A_HEAD_DIM:(h + 1) * A_HEAD_DIM, :]
            acc_ref[h] = acc_ref[h] * alpha + _dot(vth, p.astype(BF16))
        return carry

    lax.fori_loop(0, nkc, attend_chunk, 0)

    out_t = jnp.concatenate([acc_ref[h] / l_ref[h] for h in range(A_HEADS)], axis=0)
    o_ref[...] = out_t.T.astype(o_ref.dtype)


def _dsa(qt, k, vt, qit, ki2, wit, B):
    T = k.shape[0]
    S = T // B
    TQ, CK = DSA_TQ, DSA_CK
    nc, nq = S // CK, S // TQ
    top_k = min(TOPK_MAX, S // 4)
    tri = (jnp.arange(CK)[:, None] >= jnp.arange(CK)[None, :]).astype(BF16)
    qcol = lambda b, j: (0, b * nq + j)
    return pl.pallas_call(
        functools.partial(_dsa_kernel, top_k=top_k),
        grid=(B, nq),
        in_specs=[
            pl.BlockSpec((A_WIDTH, TQ), qcol),
            pl.BlockSpec((IDX_HEADS * IDX_DIM, TQ), qcol),
            pl.BlockSpec((IDX_HEADS, TQ), qcol),
            pl.BlockSpec((S, A_WIDTH), lambda b, j: (b, 0)),
            pl.BlockSpec((nc, A_WIDTH, CK), lambda b, j: (b, 0, 0)),
            pl.BlockSpec((S, 2 * IDX_DIM), lambda b, j: (b, 0)),
            pl.BlockSpec((CK, CK), lambda b, j: (0, 0)),
        ],
        out_specs=pl.BlockSpec((TQ, A_WIDTH), lambda b, j: (b * nq + j, 0)),
        out_shape=jax.ShapeDtypeStruct((T, A_WIDTH), BF16),
        scratch_shapes=[
            pltpu.VMEM((nc, CK, TQ), F32),
            pltpu.VMEM((CK, TQ), F32),
            pltpu.VMEM((A_HEADS, CK, TQ), F32),
            pltpu.VMEM((A_HEADS, LANES, TQ), BF16),
            pltpu.VMEM((IDX_HEADS, LANES, TQ), BF16),
            pltpu.VMEM((A_HEADS, 1, TQ), F32),
            pltpu.VMEM((A_HEADS, 1, TQ), F32),
            pltpu.VMEM((A_HEADS, A_HEAD_DIM, TQ), F32),
            pltpu.VMEM((1, TQ), F32),
            pltpu.VMEM((1, TQ), F32),
        ],
        compiler_params=_cparams(("parallel", "arbitrary")),
        name="dsa",
    )(qt, qit, wit, k, vt, ki2, tri)


LRU_TS = 512
HALO = SUBLANES


def _softplus(x):
    return jnp.maximum(x, 0.0) + jnp.log1p(jnp.exp(-jnp.abs(x)))


def _gelu_tanh(x):
    return 0.5 * x * (1.0 + jnp.tanh(0.7978845608028654 * (x + 0.044715 * (x * x * x))))


def _rglru_kernel(xb_ref, gate_ref, cw_ref, cb_ref, wra_ref, bra_ref, wri_ref, bri_ref, lam_ref,
                  o_ref, xs_ref, a_ref, b_ref, hc_ref):
    ts = xb_ref.shape[1]
    C = xb_ref.shape[2]

    @pl.when(pl.program_id(1) == 0)
    def _():
        xs_ref[0:HALO, :] = jnp.zeros((HALO, C), F32)
        hc_ref[...] = jnp.zeros(hc_ref.shape, F32)

    x = xb_ref[0]
    xs_ref[HALO:HALO + ts, :] = x
    xc = cb_ref[...] + jnp.zeros((ts, C), F32)
    for kk in range(CONV_W):
        off = HALO - (CONV_W - 1) + kk
        xc = xc + cw_ref[kk:kk + 1, :] * xs_ref[off:off + ts, :]
    xs_ref[0:HALO, :] = x[ts - HALO:ts, :]

    xcb = xc.astype(BF16)
    r = _sigmoid(_dot(xcb, wra_ref[...]) + bra_ref[...])
    gi = _sigmoid(_dot(xcb, wri_ref[...]) + bri_ref[...])
    log_a = (-LRU_C) * r * _softplus(-lam_ref[...])
    a = jnp.exp(log_a)
    a_ref[...] = a
    b_ref[...] = jnp.sqrt(-jnp.tanh(log_a) * (1.0 + a * a)) * (gi * xc)

    row = lax.broadcasted_iota(jnp.int32, (SUBLANES, C), 0)

    def group(g, carry):
        r0 = pl.multiple_of(g * SUBLANES, SUBLANES)
        av = a_ref[pl.ds(r0, SUBLANES), :]
        bv = b_ref[pl.ds(r0, SUBLANES), :]
        for sh in (1, 2, 4):
            a_sh = pltpu.roll(av, sh, axis=0)
            b_sh = pltpu.roll(bv, sh, axis=0)
            ok = row >= sh
            bv = jnp.where(ok, av * b_sh + bv, bv)
            av = jnp.where(ok, av * a_sh, av)
        h8 = av * carry + bv
        a_ref[pl.ds(r0, SUBLANES), :] = h8
        return jnp.broadcast_to(h8[SUBLANES - 1:SUBLANES, :], (SUBLANES, C))

    hc_ref[...] = lax.fori_loop(0, ts // SUBLANES, group, hc_ref[...])
    o_ref[0] = (a_ref[...] * _gelu_tanh(gate_ref[0])).astype(o_ref.dtype)


def _rglru(xb, gate, conv_w, conv_b, w_ra, b_ra, w_ri, b_ri, lam):
    B, S, C = xb.shape
    ts = min(LRU_TS, S)
    blk = lambda b, s: (b, s, 0)
    fixed = lambda b, s: (0, 0)
    return pl.pallas_call(
        _rglru_kernel,
        grid=(B, S // ts),
        in_specs=[
            pl.BlockSpec((1, ts, C), blk),
            pl.BlockSpec((1, ts, C), blk),
            pl.BlockSpec((CONV_W, C), fixed),
            pl.BlockSpec((1, C), fixed),
            pl.BlockSpec((C, C), fixed),
            pl.BlockSpec((1, C), fixed),
            pl.BlockSpec((C, C), fixed),
            pl.BlockSpec((1, C), fixed),
            pl.BlockSpec((1, C), fixed),
        ],
        out_specs=pl.BlockSpec((1, ts, C), blk),
        out_shape=jax.ShapeDtypeStruct((B, S, C), BF16),
        scratch_shapes=[
            pltpu.VMEM((HALO + ts, C), F32),
            pltpu.VMEM((ts, C), F32),
            pltpu.VMEM((ts, C), F32),
            pltpu.VMEM((SUBLANES, C), F32),
        ],
        compiler_params=_cparams(("parallel", "arbitrary")),
        name="rglru",
    )(xb, gate, conv_w, conv_b, w_ra, b_ra, w_ri, b_ri, lam)


OD_QKVR = 3072
OD_GLR = (3072, 3200)
OD_COLS = 3200


def _odd_proj_kernel(h_ref, g_ref, w_ref, wg2_ref, bg_ref, q_ref, k_ref, v_ref, r_ref, gk_ref):
    xn = _rms(h_ref[...], g_ref[...]).astype(BF16)

    def seg(a, b):
        return _dot(xn, w_ref[:, a:b])

    q_ref[...] = seg(0, GLA_DK) * (GLA_DKH ** -0.5)
    k_ref[...] = seg(GLA_DK, 2 * GLA_DK)
    v_ref[...] = seg(2 * GLA_DK, 2 * GLA_DK + GLA_DV).astype(BF16)
    r_ref[...] = seg(2 * GLA_DK + GLA_DV, OD_QKVR)
    glr = seg(*OD_GLR).astype(BF16)
    z = _dot(glr, wg2_ref[...]) + bg_ref[...]
    gk_ref[...] = (-_softplus(-z)) * (1.0 / GLA_TAU)


def _odd_proj(h, g_pre, w_in, w_g2, b_g):
    T, D = h.shape
    tm = min(PROJ_TM, T)
    row = lambda i: (i, 0)
    fixed = lambda i: (0, 0)
    outs = [(GLA_DK, F32), (GLA_DK, F32), (GLA_DV, BF16), (GLA_DV, F32), (GLA_DK, F32)]
    return pl.pallas_call(
        _odd_proj_kernel,
        grid=(T // tm,),
        in_specs=[
            pl.BlockSpec((tm, D), row),
            pl.BlockSpec((1, D), fixed),
            pl.BlockSpec(w_in.shape, fixed),
            pl.BlockSpec(w_g2.shape, fixed),
            pl.BlockSpec((1, GLA_DK), fixed),
        ],
        out_specs=[pl.BlockSpec((tm, n), row) for n, _ in outs],
        out_shape=[jax.ShapeDtypeStruct((T, n), dt) for n, dt in outs],
        compiler_params=_cparams(("parallel",)),
        name="odd_proj",
    )(h, g_pre, w_in, w_g2, b_g)


GLA_TS = 256


def _gla_kernel(q_ref, k_ref, v_ref, gk_ref, r_ref, hn_ref, tri_ref, o_ref,
                st_ref, qd_ref, oi_ref, u_ref, stb_ref):
    ts = q_ref.shape[1]
    C = GLA_CHUNK
    nch = ts // C

    @pl.when(pl.program_id(1) == 0)
    def _():
        st_ref[...] = jnp.zeros(st_ref.shape, F32)

    gk = gk_ref[0]
    g_hi = gk.astype(BF16)
    rem = gk - g_hi.astype(F32)
    g_mid = rem.astype(BF16)
    g_lo = (rem - g_mid.astype(F32)).astype(BF16)
    tri = tri_ref[...]
    G = _dot(tri, g_hi) + _dot(tri, g_mid) + _dot(tri, g_lo)

    kf = k_ref[0]
    qd_ref[...] = (q_ref[0] * jnp.exp(G)).astype(BF16)
    k_inv = (kf * jnp.exp(-G)).astype(BF16)
    g_last = [G[(c + 1) * C - 1:(c + 1) * C, :] for c in range(nch)]
    k_rem = jnp.concatenate(
        [kf[c * C:(c + 1) * C, :] * jnp.exp(g_last[c] - G[c * C:(c + 1) * C, :]) for c in range(nch)],
        axis=0).astype(BF16)

    ri = lax.broadcasted_iota(jnp.int32, (C, C), 0)
    ci = lax.broadcasted_iota(jnp.int32, (C, C), 1)
    tril = ri >= ci

    for c in range(nch):
        rows = slice(c * C, (c + 1) * C)
        for h in range(GLA_HEADS):
            ksl = slice(h * GLA_DKH, (h + 1) * GLA_DKH)
            vsl = slice(h * GLA_DVH, (h + 1) * GLA_DVH)
            vh = v_ref[0, rows, vsl]
            att = jnp.where(tril, _dot_nt(qd_ref[rows, ksl], k_inv[rows, ksl]), 0.0).astype(BF16)
            oi_ref[rows, vsl] = _dot(att, vh)
            u_ref[c, h] = _dot_tn(vh, k_rem[rows, ksl])

    for h in range(GLA_HEADS):
        ksl = slice(h * GLA_DKH, (h + 1) * GLA_DKH)
        st = st_ref[h]
        for c in range(nch):
            stb_ref[c, h] = st.astype(BF16)
            st = st * jnp.exp(g_last[c][:, ksl]) + u_ref[c, h]
        st_ref[h] = st

    for c in range(nch):
        rows = slice(c * C, (c + 1) * C)
        for h in range(GLA_HEADS):
            ksl = slice(h * GLA_DKH, (h + 1) * GLA_DKH)
            vsl = slice(h * GLA_DVH, (h + 1) * GLA_DVH)
            o = oi_ref[rows, vsl] + _dot_nt(qd_ref[rows, ksl], stb_ref[c, h])
            on = _rms(o, hn_ref[...])
            rr = r_ref[0, rows, vsl]
            o_ref[0, rows, vsl] = (on * (rr * _sigmoid(rr))).astype(o_ref.dtype)


def _gla(q, k, v, gk, r, head_norm):
    B, S, _ = q.shape
    ts = min(GLA_TS, S)
    nch = ts // GLA_CHUNK
    pos = jnp.arange(ts)
    tri = ((pos[:, None] >= pos[None, :])
           & (pos[:, None] // GLA_CHUNK == pos[None, :] // GLA_CHUNK)).astype(BF16)
    blk = lambda b, s: (b, s, 0)
    return pl.pallas_call(
        _gla_kernel,
        grid=(B, S // ts),
        in_specs=[
            pl.BlockSpec((1, ts, GLA_DK), blk),
            pl.BlockSpec((1, ts, GLA_DK), blk),
            pl.BlockSpec((1, ts, GLA_DV), blk),
            pl.BlockSpec((1, ts, GLA_DK), blk),
            pl.BlockSpec((1, ts, GLA_DV), blk),
            pl.BlockSpec((1, GLA_DVH), lambda b, s: (0, 0)),
            pl.BlockSpec((ts, ts), lambda b, s: (0, 0)),
        ],
        out_specs=pl.BlockSpec((1, ts, GLA_DV), blk),
        out_shape=jax.ShapeDtypeStruct((B, S, GLA_DV), BF16),
        scratch_shapes=[
            pltpu.VMEM((GLA_HEADS, GLA_DVH, GLA_DKH), F32),
            pltpu.VMEM((ts, GLA_DK), BF16),
            pltpu.VMEM((ts, GLA_DV), F32),
            pltpu.VMEM((nch, GLA_HEADS, GLA_DVH, GLA_DKH), F32),
            pltpu.VMEM((nch, GLA_HEADS, GLA_DVH, GLA_DKH), BF16),
        ],
        compiler_params=_cparams(("parallel", "arbitrary")),
        name="gla",
    )(q, k, v, gk, r, head_norm, tri)


def _xa_kv_kernel(mem_ref, g_ref, w_ref, k_ref, v_ref):
    mn = _rms(mem_ref[0], g_ref[...]).astype(BF16)
    k_ref[0] = (_dot(mn, w_ref[:, :D_MODEL]) * (XA_HEAD_DIM ** -0.5)).astype(BF16)
    v_ref[0] = _dot(mn, w_ref[:, D_MODEL:]).astype(BF16)


def _xa_kv(mem, g_mem, w_kv, lead):
    B, M, D = mem.shape
    blk = lambda b: (b, 0, 0)
    return pl.pallas_call(
        _xa_kv_kernel,
        grid=(B,),
        in_specs=[pl.BlockSpec((1, M, D), blk), pl.BlockSpec((1, D), lambda b: (0, 0)),
                  _resident(w_kv, lead)],
        out_specs=[pl.BlockSpec((1, M, D), blk), pl.BlockSpec((1, M, D), blk)],
        out_shape=[jax.ShapeDtypeStruct((B, M, D), BF16)] * 2,
        compiler_params=_cparams(("parallel",)),
        name="xa_kv",
    )(mem, g_mem, w_kv)


XA_TM = 512


def _mix_out_xa_kernel(*refs, offsets):
    n = len(offsets)
    h_ref, gmix_ref, gpre_ref, gpost_ref = refs[:4]
    part_refs = refs[4:4 + n]
    wout_ref, wq_ref, k_ref, v_ref, wo_ref, o_ref = refs[4 + n:]
    m = None
    for p_ref, off in zip(part_refs, offsets):
        kk = p_ref.shape[-1]
        term = _dot(p_ref[0], wout_ref[off:off + kk, :])
        m = term if m is None else m + term
    x = h_ref[0] + _rms(m, gmix_ref[...])
    xn = _rms(x, gpre_ref[...]).astype(BF16)
    q = _dot(xn, wq_ref[...]).astype(BF16)
    heads = []
    for h in range(XA_HEADS):
        sl = slice(h * XA_HEAD_DIM, (h + 1) * XA_HEAD_DIM)
        s = _dot_nt(q[:, sl], k_ref[0, :, sl])
        p = jnp.exp(s - jnp.max(s, axis=-1, keepdims=True))
        oh = _dot(p.astype(BF16), v_ref[0, :, sl]) / jnp.sum(p, axis=-1, keepdims=True)
        heads.append(oh.astype(BF16))
    c = _dot(jnp.concatenate(heads, axis=-1), wo_ref[...])
    o_ref[0] = x + _rms(c, gpost_ref[...])


def _mix_out_xa(h, g_mix, g_pre, g_post, parts, w_out, out_lead, w_q, kx, vx, w_o, lead):
    B, S, D = h.shape
    M = kx.shape[1]
    tm = min(XA_TM, S)
    blk = lambda b, i: (b, i, 0)
    fixed = lambda b, i: (0, 0)
    offsets, off = [], 0
    for p in parts:
        offsets.append(off)
        off += p.shape[-1]
    return pl.pallas_call(
        functools.partial(_mix_out_xa_kernel, offsets=tuple(offsets)),
        grid=(B, S // tm),
        in_specs=[
            pl.BlockSpec((1, tm, D), blk),
            pl.BlockSpec((1, D), fixed),
            pl.BlockSpec((1, D), fixed),
            pl.BlockSpec((1, D), fixed),
            *[pl.BlockSpec((1, tm, p.shape[-1]), blk) for p in parts],
            _resident(w_out, out_lead),
            _resident(w_q, lead),
            pl.BlockSpec((1, M, D), lambda b, i: (b, 0, 0)),
            pl.BlockSpec((1, M, D), lambda b, i: (b, 0, 0)),
            _resident(w_o, lead),
        ],
        out_specs=pl.BlockSpec((1, tm, D), blk),
        out_shape=jax.ShapeDtypeStruct((B, S, D), F32),
        compiler_params=_cparams(("parallel", "parallel")),
        name="mix_out_xa",
    )(h, g_mix, g_pre, g_post, *parts, w_out, w_q, kx, vx, w_o)


def _block_diag(w):
    G, n, _ = w.shape
    eye = jnp.eye(G, dtype=w.dtype)
    return (eye[:, None, :, None] * w[:, :, None, :]).reshape(G * n, G * n)


def _even_w_in(w):
    ki = w[:, 1280:1344]
    pad = jnp.zeros((w.shape[0], LANES - IDX_HEADS), w.dtype)
    return jnp.concatenate([w[:, :1280], ki, ki, w[:, 1352:2376], w[:, 1344:1352], pad], axis=1).astype(BF16)


def _odd_w_in(w):
    pad = jnp.zeros((w.shape[0], LANES - GLA_GATE_RANK), w.dtype)
    return jnp.concatenate([w[:, :2048], w[:, 2064:3088], w[:, 2048:2064], pad], axis=1).astype(BF16)


def kernel(x, mem, norms, ffn_w_gu, ffn_w_down, xa_w_q, xa_w_kv, xa_w_o, ev_w_in, ev_kv_norm, ev_w_uk, ev_w_uv, ev_conv_w, ev_conv_b, ev_w_ra, ev_b_ra, ev_w_ri, ev_b_ri, ev_lam, ev_w_out, od_w_in, od_w_g2, od_b_g, od_head_norm, od_w_out):
    B, S, D = x.shape
    T = B * S
    depth = norms.shape[0]
    h = x.reshape(T, D)

    def gain(layer, idx):
        return norms[layer, idx][None, :]

    ffn_gu, ffn_down = ffn_w_gu.astype(BF16), ffn_w_down.astype(BF16)
    w_q, w_kv, w_o = xa_w_q.astype(BF16), xa_w_kv.astype(BF16), xa_w_o.astype(BF16)
    ev_out, od_out = ev_w_out.astype(BF16), od_w_out.astype(BF16)

    for layer in range(depth):
        h = _ffn(h, gain(layer, N_FFN1_PRE), gain(layer, N_FFN1_POST), ffn_gu, ffn_down, (layer, 0))

        if layer % 2 == 0:
            e = layer // 2
            qt, k, vt, qit, ki2, wit, gate, xb = _even_proj(
                h, gain(layer, N_MIX_PRE), _even_w_in(ev_w_in[e]), ev_kv_norm[e][None, :],
                ev_w_uk[e].astype(BF16), ev_w_uv[e].astype(BF16))
            r3 = lambda a: a.reshape(B, S, a.shape[-1])
            a_out = _dsa(qt, k, vt, qit, ki2, wit, B)
            b_out = _rglru(r3(xb), r3(gate), ev_conv_w[e], ev_conv_b[e][None, :],
                           _block_diag(ev_w_ra[e]).astype(BF16), ev_b_ra[e].reshape(1, B_WIDTH),
                           _block_diag(ev_w_ri[e]).astype(BF16), ev_b_ri[e].reshape(1, B_WIDTH),
                           ev_lam[e][None, :])
            parts = [a_out.reshape(B, S, A_WIDTH), b_out]
            w_out, w_out_lead = ev_out, (e,)
        else:
            o = layer // 2
            w_g2 = jnp.concatenate(
                [od_w_g2[o], jnp.zeros((LANES - GLA_GATE_RANK, GLA_DK), od_w_g2.dtype)], axis=0).astype(BF16)
            q, k, v, r, gk = _odd_proj(h, gain(layer, N_MIX_PRE), _odd_w_in(od_w_in[o]), w_g2,
                                       od_b_g[o][None, :])
            r3 = lambda a: a.reshape(B, S, a.shape[-1])
            g_out = _gla(r3(q), r3(k), r3(v), r3(gk), r3(r), od_head_norm[o][None, :])
            parts = [g_out]
            w_out, w_out_lead = od_out, (o,)

        kx, vx = _xa_kv(mem, gain(layer, N_MEM_NORM), w_kv, (layer,))
        h = _mix_out_xa(h.reshape(B, S, D), gain(layer, N_MIX_POST), gain(layer, N_XA_PRE),
                        gain(layer, N_XA_POST), parts, w_out, w_out_lead,
                        w_q, kx, vx, w_o, (layer,)).reshape(T, D)

        h = _ffn(h, gain(layer, N_FFN2_PRE), gain(layer, N_FFN2_POST), ffn_gu, ffn_down, (layer, 1))
    return h.reshape(B, S, D)
```

```python
import functools

import jax
import jax.numpy as jnp
from jax import lax
from jax.experimental import pallas as pl
from jax.experimental.pallas import tpu as pltpu

F32 = jnp.float32
BF16 = jnp.bfloat16

EPS = 1e-6
D_MODEL = 1024
D_FF = 2816
XA_HEADS = 4
XA_HEAD_DIM = D_MODEL // XA_HEADS
A_HEADS = 8
A_HEAD_DIM = 64
A_WIDTH = A_HEADS * A_HEAD_DIM
KV_RANK = 256
IDX_HEADS = 8
IDX_DIM = 64
TOPK_MAX = 256
B_WIDTH = D_MODEL - A_WIDTH
B_BLOCKS = 8
B_BLOCK_DIM = B_WIDTH // B_BLOCKS
CONV_W = 4
LRU_C = 8.0
GLA_HEADS = 4
GLA_DK = D_MODEL // 2
GLA_DV = D_MODEL
GLA_DKH = GLA_DK // GLA_HEADS
GLA_DVH = GLA_DV // GLA_HEADS
GLA_GATE_RANK = 16
GLA_TAU = 16.0
GLA_CHUNK = 64
(N_FFN1_PRE, N_FFN1_POST, N_MIX_PRE, N_MIX_POST, N_XA_PRE, N_XA_POST, N_MEM_NORM,
 N_FFN2_PRE, N_FFN2_POST) = range(9)

LANES = 128
SUBLANES = 8
VMEM_LIMIT = 48 * 1024 * 1024

NEG_BIG = -1e30
LOG2E = 1.4426950408889634
INT_MIN = -2 ** 31
POS_INF_CODE = 0x7F800000
NEG_INF_CODE = -0x7F800001


def _cparams(sem):
    return pltpu.CompilerParams(dimension_semantics=sem, vmem_limit_bytes=VMEM_LIMIT)


def _rms(x, g):
    return x * lax.rsqrt(jnp.mean(x * x, axis=-1, keepdims=True) + EPS) * g


def _dot(a, b):
    return jnp.dot(a, b, preferred_element_type=F32)


def _dot_nt(a, b):
    return lax.dot_general(a, b, (((1,), (1,)), ((), ())), preferred_element_type=F32)


def _dot_tn(a, b):
    return lax.dot_general(a, b, (((0,), (0,)), ((), ())), preferred_element_type=F32)


def _sigmoid(x):
    return 1.0 / (1.0 + jnp.exp(-x))


FFN_TM = 1024
FFN_TF = 256


def _ffn_kernel(h_ref, gpre_ref, gpost_ref, wgu_ref, wd_ref, o_ref, act_ref):
    F = wd_ref.shape[0]
    x = h_ref[...]
    xn = _rms(x, gpre_ref[...]).astype(BF16)
    for c in range(F // FFN_TF):
        g = _dot(xn, wgu_ref[:, c * FFN_TF:(c + 1) * FFN_TF])
        u = _dot(xn, wgu_ref[:, F + c * FFN_TF:F + (c + 1) * FFN_TF])
        act_ref[:, c * FFN_TF:(c + 1) * FFN_TF] = (g * _sigmoid(g) * u).astype(BF16)
    f = _dot(act_ref[...], wd_ref[...])
    o_ref[...] = x + 0.5 * _rms(f, gpost_ref[...])


def _resident(arr, lead=()):
    tail = arr.shape[len(lead):]
    index = tuple(lead) + (0,) * len(tail)
    return pl.BlockSpec((None,) * len(lead) + tail, lambda *_: index, pipeline_mode=pl.Buffered(1))


def _ffn(h, g_pre, g_post, w_gu, w_down, lead):
    T, D = h.shape
    F = w_down.shape[-2]
    tm = min(FFN_TM, T)
    return pl.pallas_call(
        _ffn_kernel,
        grid=(T // tm,),
        in_specs=[
            pl.BlockSpec((tm, D), lambda i: (i, 0)),
            pl.BlockSpec((1, D), lambda i: (0, 0)),
            pl.BlockSpec((1, D), lambda i: (0, 0)),
            _resident(w_gu, lead),
            _resident(w_down, lead),
        ],
        out_specs=pl.BlockSpec((tm, D), lambda i: (i, 0)),
        out_shape=jax.ShapeDtypeStruct((T, D), F32),
        scratch_shapes=[pltpu.VMEM((tm, F), BF16)],
        compiler_params=_cparams(("parallel",)),
        name="ffn",
    )(h, g_pre, g_post, w_gu, w_down)


PROJ_TM = 512


EV_Q = (0, 512)
EV_CKV = (512, 768)
EV_QI = (768, 1280)
EV_KI2 = (1280, 1408)
EV_GATE = (1408, 1920)
EV_XB = (1920, 2432)
EV_WI = (2432, 2560)
EV_COLS = 2560


DSA_TQ = 256
DSA_CK = 512


def _even_proj_kernel(h_ref, g_ref, w_ref, kvn_ref, wuk_ref, wuv_ref,
                      cw_ref, cb_ref, wra_ref, bra_ref, wri_ref, bri_ref, lam_ref,
                      qt_ref, k_ref, vt_ref, qit_ref, ki_ref, wit_ref, bout_ref,
                      xs_ref, a_ref, b_ref, hc_ref, gg_ref, *, tiles_per_seq):
    _rglru_reset(pl.program_id(0) % tiles_per_seq == 0, xs_ref, hc_ref)
    xn = _rms(h_ref[...], g_ref[...]).astype(BF16)

    def seg(ab):
        return _dot(xn, w_ref[:, ab[0]:ab[1]])

    _rglru_gates(seg(EV_XB), cw_ref, cb_ref, wra_ref, bra_ref, wri_ref, bri_ref, lam_ref,
                 xs_ref, a_ref, b_ref)
    gg_ref[...] = _gelu_tanh(seg(EV_GATE))
    qt_ref[...] = (seg(EV_Q) * (A_HEAD_DIM ** -0.5 * LOG2E)).T.astype(BF16)
    ckv = _rms(seg(EV_CKV), kvn_ref[...]).astype(BF16)
    k_ref[...] = _dot(ckv, wuk_ref[...]).astype(BF16)
    vt_ref[0] = _dot(ckv, wuv_ref[...]).astype(BF16).T
    qit_ref[...] = (seg(EV_QI) * (IDX_DIM ** -0.5)).T.astype(BF16)
    ki_ref[...] = seg(EV_KI2).astype(BF16)
    wit_ref[...] = (seg(EV_WI) * (IDX_HEADS ** -0.5)).T[:IDX_HEADS, :]
    _rglru_scan(a_ref, b_ref, hc_ref)
    bout_ref[...] = (a_ref[...] * gg_ref[...]).astype(bout_ref.dtype)


def _even_proj(h, g_pre, w_in, kv_norm, w_uk, w_uv, lru_params, seq_len):
    T, D = h.shape
    tm = DSA_CK
    C = B_WIDTH
    row = lambda i: (i, 0)
    col = lambda i: (0, i)
    fixed = lambda i: (0, 0)
    out_specs = [
        pl.BlockSpec((A_WIDTH, tm), col),
        pl.BlockSpec((tm, A_WIDTH), row),
        pl.BlockSpec((1, A_WIDTH, tm), lambda i: (i, 0, 0)),
        pl.BlockSpec((IDX_HEADS * IDX_DIM, tm), col),
        pl.BlockSpec((tm, 2 * IDX_DIM), row),
        pl.BlockSpec((IDX_HEADS, tm), col),
        pl.BlockSpec((tm, C), row),
    ]
    out_shape = [
        jax.ShapeDtypeStruct((A_WIDTH, T), BF16),
        jax.ShapeDtypeStruct((T, A_WIDTH), BF16),
        jax.ShapeDtypeStruct((T // tm, A_WIDTH, tm), BF16),
        jax.ShapeDtypeStruct((IDX_HEADS * IDX_DIM, T), BF16),
        jax.ShapeDtypeStruct((T, 2 * IDX_DIM), BF16),
        jax.ShapeDtypeStruct((IDX_HEADS, T), F32),
        jax.ShapeDtypeStruct((T, C), BF16),
    ]
    return pl.pallas_call(
        functools.partial(_even_proj_kernel, tiles_per_seq=seq_len // tm),
        grid=(T // tm,),
        in_specs=[
            pl.BlockSpec((tm, D), row),
            pl.BlockSpec((1, D), fixed),
            pl.BlockSpec(w_in.shape, fixed),
            pl.BlockSpec((1, KV_RANK), fixed),
            pl.BlockSpec(w_uk.shape, fixed),
            pl.BlockSpec(w_uv.shape, fixed),
            *[pl.BlockSpec(p.shape, fixed) for p in lru_params],
        ],
        out_specs=out_specs,
        out_shape=out_shape,
        scratch_shapes=[
            pltpu.VMEM((HALO + tm, C), F32),
            pltpu.VMEM((tm, C), F32),
            pltpu.VMEM((tm, C), F32),
            pltpu.VMEM((SUBLANES, C), F32),
            pltpu.VMEM((tm, C), F32),
        ],
        compiler_params=_cparams(("arbitrary",)),
        name="even_proj",
    )(h, g_pre, w_in, kv_norm, w_uk, w_uv, *lru_params)


ACC_ROWS = 4 * SUBLANES


def _col_partial(x, op):
    rows, n = x.shape
    part = x.reshape(rows // ACC_ROWS, ACC_ROWS, n)
    return jnp.max(part, axis=0) if op == "max" else jnp.sum(part, axis=0)


def _col_reduce(x, op):
    part = _col_partial(x, op)
    return (jnp.max(part, axis=0, keepdims=True) if op == "max"
            else jnp.sum(part, axis=0, keepdims=True))


def _dsa_kernel(qt_ref, qit_ref, wit_ref, k_ref, vt_ref, ki_ref, tri_ref, o_ref,
                sc_ref, bias_ref, s_ref, qm_ref, qim_ref, m_ref, l_ref, acc_ref, need_ref, seen_ref,
                *, top_k):
    TQ, CK = DSA_TQ, DSA_CK
    j = pl.program_id(1)
    q0 = j * TQ
    nkc = (q0 + TQ + CK - 1) // CK

    low_half = lax.broadcasted_iota(jnp.int32, (LANES, TQ), 0) < A_HEAD_DIM
    for h in range(A_HEADS):
        pr = slice((h // 2) * LANES, (h // 2 + 1) * LANES)
        keep = low_half if h % 2 == 0 else jnp.logical_not(low_half)
        qm_ref[h] = jnp.where(keep, qt_ref[pr, :], jnp.zeros((), BF16))
        qim_ref[h] = jnp.where(keep, qit_ref[pr, :], jnp.zeros((), BF16))

    key_iota = lax.broadcasted_iota(jnp.int32, (CK, TQ), 0)
    q_pos = q0 + lax.broadcasted_iota(jnp.int32, (CK, TQ), 1)

    def score_chunk(c, carry):
        base = pl.multiple_of(c * CK, CK)
        kic = ki_ref[pl.ds(base, CK), :]
        acc = jnp.zeros((CK, TQ), F32)
        for h in range(IDX_HEADS):
            acc = acc + jnp.maximum(_dot(kic, qim_ref[h]), 0.0) * wit_ref[h:h + 1, :]
        sc_ref[c] = jnp.where(base + key_iota <= q_pos, acc, -jnp.inf)
        return carry

    lax.fori_loop(0, nkc, score_chunk, 0)

    def count(pred_fn):
        def body(c, cnt):
            for g in range(CK // ACC_ROWS):
                hit = pred_fn(sc_ref[c, g * ACC_ROWS:(g + 1) * ACC_ROWS, :])
                cnt = jnp.where(hit, cnt + 1, cnt)
            return cnt
        cnt = lax.fori_loop(0, nkc, body, jnp.zeros((ACC_ROWS, TQ), jnp.int32))
        return jnp.sum(cnt, axis=0, keepdims=True)

    def code_to_float(code):
        code = jnp.clip(code, NEG_INF_CODE, POS_INF_CODE)
        return lax.bitcast_convert_type(code ^ ((code >> 31) & jnp.int32(0x7FFFFFFF)), F32)

    def bit_pass(i, carry):
        code, n_at = carry
        cand = code + lax.shift_left(jnp.int32(1), 31 - i)
        cand_f = code_to_float(cand)
        n_ge = count(lambda sc: sc >= cand_f)
        take = n_ge >= top_k
        return jnp.where(take, cand, code), jnp.where(take, n_ge, n_at)

    code, n_at = lax.fori_loop(
        0, 32, bit_pass,
        (jnp.full((1, TQ), INT_MIN, jnp.int32), jnp.full((1, TQ), nkc * CK, jnp.int32)))
    thr = code_to_float(code)
    finite = thr > -jnp.inf
    has_ties = jnp.max(jnp.where(finite & (n_at > top_k), 1, 0)) > 0
    thr_sel = jnp.where(finite, thr, jnp.finfo(F32).min)

    m_ref[...] = jnp.full(m_ref.shape, -jnp.inf, F32)
    l_ref[...] = jnp.zeros(l_ref.shape, F32)
    acc_ref[...] = jnp.zeros(acc_ref.shape, F32)
    need_ref[...] = jnp.zeros(need_ref.shape, F32)
    seen_ref[...] = jnp.zeros(seen_ref.shape, F32)

    @pl.when(has_ties)
    def _():
        n_gt = count(lambda sc: sc > thr)
        need_ref[...] = jnp.where(finite, (top_k - n_gt).astype(F32), 0.0)

    def attend_chunk(c, carry):
        base = pl.multiple_of(c * CK, CK)

        @pl.when(jnp.logical_not(has_ties))
        def _():
            bias_ref[...] = jnp.where(sc_ref[c] >= thr_sel, 0.0, NEG_BIG)

        @pl.when(has_ties)
        def _():
            sc = sc_ref[c]
            eq = sc == thr
            rank = seen_ref[...] + _dot(tri_ref[...], eq.astype(BF16))
            sel = (sc > thr) | (eq & (rank <= need_ref[...]))
            bias_ref[...] = jnp.where(sel, 0.0, NEG_BIG)
            seen_ref[...] += _col_reduce(eq.astype(F32), "sum")

        cmax = []
        for h in range(A_HEADS):
            kp = k_ref[pl.ds(base, CK), (h // 2) * LANES:(h // 2 + 1) * LANES]
            s = _dot(kp, qm_ref[h]) + bias_ref[...]
            s_ref[h] = s
            cmax.append(_col_reduce(s, "max"))
        for h in range(A_HEADS):
            m_old = m_ref[h]
            m_new = jnp.maximum(m_old, cmax[h])
            alpha = jnp.exp2(m_old - m_new)
            p = jnp.exp2(s_ref[h] - m_new)
            l_ref[h] = alpha * l_ref[h] + _col_reduce(p, "sum")
            m_ref[h] = m_new
            vth = vt_ref[c, h * A_HEAD_DIM:(h + 1) * A_HEAD_DIM, :]
            acc_ref[h] = acc_ref[h] * alpha + _dot(vth, p.astype(BF16))
        return carry

    lax.fori_loop(0, nkc, attend_chunk, 0)

    out_t = jnp.concatenate([acc_ref[h] / l_ref[h] for h in range(A_HEADS)], axis=0)
    o_ref[...] = out_t.T.astype(o_ref.dtype)


def _dsa(qt, k, vt, qit, ki2, wit, B):
    T = k.shape[0]
    S = T // B
    TQ, CK = DSA_TQ, DSA_CK
    nc, nq = S // CK, S // TQ
    top_k = min(TOPK_MAX, S // 4)
    tri = (jnp.arange(CK)[:, None] >= jnp.arange(CK)[None, :]).astype(BF16)
    qcol = lambda b, j: (0, b * nq + j)
    return pl.pallas_call(
        functools.partial(_dsa_kernel, top_k=top_k),
        grid=(B, nq),
        in_specs=[
            pl.BlockSpec((A_WIDTH, TQ), qcol),
            pl.BlockSpec((IDX_HEADS * IDX_DIM, TQ), qcol),
            pl.BlockSpec((IDX_HEADS, TQ), qcol),
            pl.BlockSpec((S, A_WIDTH), lambda b, j: (b, 0)),
            pl.BlockSpec((nc, A_WIDTH, CK), lambda b, j: (b, 0, 0)),
            pl.BlockSpec((S, 2 * IDX_DIM), lambda b, j: (b, 0)),
            pl.BlockSpec((CK, CK), lambda b, j: (0, 0)),
        ],
        out_specs=pl.BlockSpec((TQ, A_WIDTH), lambda b, j: (b * nq + j, 0)),
        out_shape=jax.ShapeDtypeStruct((T, A_WIDTH), BF16),
        scratch_shapes=[
            pltpu.VMEM((nc, CK, TQ), F32),
            pltpu.VMEM((CK, TQ), F32),
            pltpu.VMEM((A_HEADS, CK, TQ), F32),
            pltpu.VMEM((A_HEADS, LANES, TQ), BF16),
            pltpu.VMEM((IDX_HEADS, LANES, TQ), BF16),
            pltpu.VMEM((A_HEADS, 1, TQ), F32),
            pltpu.VMEM((A_HEADS, 1, TQ), F32),
            pltpu.VMEM((A_HEADS, A_HEAD_DIM, TQ), F32),
            pltpu.VMEM((1, TQ), F32),
            pltpu.VMEM((1, TQ), F32),
        ],
        compiler_params=_cparams(("parallel", "arbitrary")),
        name="dsa",
    )(qt, qit, wit, k, vt, ki2, tri)


HALO = SUBLANES


def _softplus(x):
    return jnp.maximum(x, 0.0) + jnp.log1p(jnp.exp(-jnp.abs(x)))


def _gelu_tanh(x):
    return 0.5 * x * (1.0 + jnp.tanh(0.7978845608028654 * (x + 0.044715 * (x * x * x))))


def _rglru_reset(first, xs_ref, hc_ref):
    @pl.when(first)
    def _():
        xs_ref[0:HALO, :] = jnp.zeros((HALO, xs_ref.shape[1]), F32)
        hc_ref[...] = jnp.zeros(hc_ref.shape, F32)


def _rglru_gates(x, cw_ref, cb_ref, wra_ref, bra_ref, wri_ref, bri_ref, lam_ref, xs_ref, a_ref, b_ref):
    ts, C = x.shape
    xs_ref[HALO:HALO + ts, :] = x
    xc = cb_ref[...] + jnp.zeros((ts, C), F32)
    for kk in range(CONV_W):
        off = HALO - (CONV_W - 1) + kk
        xc = xc + cw_ref[kk:kk + 1, :] * xs_ref[off:off + ts, :]
    xs_ref[0:HALO, :] = x[ts - HALO:ts, :]

    xcb = xc.astype(BF16)
    r = _sigmoid(_dot(xcb, wra_ref[...]) + bra_ref[...])
    gi = _sigmoid(_dot(xcb, wri_ref[...]) + bri_ref[...])
    log_a = (-LRU_C) * r * _softplus(-lam_ref[...])
    a = jnp.exp(log_a)
    a_ref[...] = a
    b_ref[...] = jnp.sqrt(-jnp.tanh(log_a) * (1.0 + a * a)) * (gi * xc)


def _rglru_scan(a_ref, b_ref, hc_ref):
    ts, C = a_ref.shape
    row = lax.broadcasted_iota(jnp.int32, (SUBLANES, C), 0)

    def group(g, carry):
        r0 = pl.multiple_of(g * SUBLANES, SUBLANES)
        av = a_ref[pl.ds(r0, SUBLANES), :]
        bv = b_ref[pl.ds(r0, SUBLANES), :]
        for sh in (1, 2, 4):
            a_sh = pltpu.roll(av, sh, axis=0)
            b_sh = pltpu.roll(bv, sh, axis=0)
            ok = row >= sh
            bv = jnp.where(ok, av * b_sh + bv, bv)
            av = jnp.where(ok, av * a_sh, av)
        h8 = av * carry + bv
        a_ref[pl.ds(r0, SUBLANES), :] = h8
        return jnp.broadcast_to(h8[SUBLANES - 1:SUBLANES, :], (SUBLANES, C))

    hc_ref[...] = lax.fori_loop(0, ts // SUBLANES, group, hc_ref[...])


OD_QKVR = 3072
OD_GLR = (3072, 3200)
OD_COLS = 3200


def _odd_proj_kernel(h_ref, g_ref, w_ref, wg2_ref, bg_ref, q_ref, k_ref, v_ref, r_ref, gk_ref):
    xn = _rms(h_ref[...], g_ref[...]).astype(BF16)

    def seg(a, b):
        return _dot(xn, w_ref[:, a:b])

    q_ref[...] = seg(0, GLA_DK) * (GLA_DKH ** -0.5)
    k_ref[...] = seg(GLA_DK, 2 * GLA_DK)
    v_ref[...] = seg(2 * GLA_DK, 2 * GLA_DK + GLA_DV).astype(BF16)
    r_ref[...] = seg(2 * GLA_DK + GLA_DV, OD_QKVR)
    glr = seg(*OD_GLR).astype(BF16)
    z = _dot(glr, wg2_ref[...]) + bg_ref[...]
    gk_ref[...] = (-_softplus(-z)) * (1.0 / GLA_TAU)


def _odd_proj(h, g_pre, w_in, w_g2, b_g):
    T, D = h.shape
    tm = min(PROJ_TM, T)
    row = lambda i: (i, 0)
    fixed = lambda i: (0, 0)
    outs = [(GLA_DK, F32), (GLA_DK, F32), (GLA_DV, BF16), (GLA_DV, F32), (GLA_DK, F32)]
    return pl.pallas_call(
        _odd_proj_kernel,
        grid=(T // tm,),
        in_specs=[
            pl.BlockSpec((tm, D), row),
            pl.BlockSpec((1, D), fixed),
            pl.BlockSpec(w_in.shape, fixed),
            pl.BlockSpec(w_g2.shape, fixed),
            pl.BlockSpec((1, GLA_DK), fixed),
        ],
        out_specs=[pl.BlockSpec((tm, n), row) for n, _ in outs],
        out_shape=[jax.ShapeDtypeStruct((T, n), dt) for n, dt in outs],
        compiler_params=_cparams(("parallel",)),
        name="odd_proj",
    )(h, g_pre, w_in, w_g2, b_g)


GLA_TS = 256


def _gla_kernel(q_ref, k_ref, v_ref, gk_ref, r_ref, hn_ref, tri_ref, o_ref,
                st_ref, qd_ref, oi_ref, u_ref, stb_ref):
    ts = q_ref.shape[1]
    C = GLA_CHUNK
    nch = ts // C

    @pl.when(pl.program_id(1) == 0)
    def _():
        st_ref[...] = jnp.zeros(st_ref.shape, F32)

    gk = gk_ref[0]
    g_hi = gk.astype(BF16)
    rem = gk - g_hi.astype(F32)
    g_mid = rem.astype(BF16)
    g_lo = (rem - g_mid.astype(F32)).astype(BF16)
    tri = tri_ref[...]
    G = _dot(tri, g_hi) + _dot(tri, g_mid) + _dot(tri, g_lo)

    kf = k_ref[0]
    qd_ref[...] = (q_ref[0] * jnp.exp(G)).astype(BF16)
    k_inv = (kf * jnp.exp(-G)).astype(BF16)
    g_last = [G[(c + 1) * C - 1:(c + 1) * C, :] for c in range(nch)]
    k_rem = jnp.concatenate(
        [kf[c * C:(c + 1) * C, :] * jnp.exp(g_last[c] - G[c * C:(c + 1) * C, :]) for c in range(nch)],
        axis=0).astype(BF16)

    ri = lax.broadcasted_iota(jnp.int32, (C, C), 0)
    ci = lax.broadcasted_iota(jnp.int32, (C, C), 1)
    tril = ri >= ci

    for c in range(nch):
        rows = slice(c * C, (c + 1) * C)
        for h in range(GLA_HEADS):
            ksl = slice(h * GLA_DKH, (h + 1) * GLA_DKH)
            vsl = slice(h * GLA_DVH, (h + 1) * GLA_DVH)
            vh = v_ref[0, rows, vsl]
            att = jnp.where(tril, _dot_nt(qd_ref[rows, ksl], k_inv[rows, ksl]), 0.0).astype(BF16)
            oi_ref[rows, vsl] = _dot(att, vh)
            u_ref[c, h] = _dot_tn(vh, k_rem[rows, ksl])

    for h in range(GLA_HEADS):
        ksl = slice(h * GLA_DKH, (h + 1) * GLA_DKH)
        st = st_ref[h]
        for c in range(nch):
            stb_ref[c, h] = st.astype(BF16)
            st = st * jnp.exp(g_last[c][:, ksl]) + u_ref[c, h]
        st_ref[h] = st

    for c in range(nch):
        rows = slice(c * C, (c + 1) * C)
        for h in range(GLA_HEADS):
            ksl = slice(h * GLA_DKH, (h + 1) * GLA_DKH)
            vsl = slice(h * GLA_DVH, (h + 1) * GLA_DVH)
            o = oi_ref[rows, vsl] + _dot_nt(qd_ref[rows, ksl], stb_ref[c, h])
            on = _rms(o, hn_ref[...])
            rr = r_ref[0, rows, vsl]
            o_ref[0, rows, vsl] = (on * (rr * _sigmoid(rr))).astype(o_ref.dtype)


def _gla(q, k, v, gk, r, head_norm):
    B, S, _ = q.shape
    ts = min(GLA_TS, S)
    nch = ts // GLA_CHUNK
    pos = jnp.arange(ts)
    tri = ((pos[:, None] >= pos[None, :])
           & (pos[:, None] // GLA_CHUNK == pos[None, :] // GLA_CHUNK)).astype(BF16)
    blk = lambda b, s: (b, s, 0)
    return pl.pallas_call(
        _gla_kernel,
        grid=(B, S // ts),
        in_specs=[
            pl.BlockSpec((1, ts, GLA_DK), blk),
            pl.BlockSpec((1, ts, GLA_DK), blk),
            pl.BlockSpec((1, ts, GLA_DV), blk),
            pl.BlockSpec((1, ts, GLA_DK), blk),
            pl.BlockSpec((1, ts, GLA_DV), blk),
            pl.BlockSpec((1, GLA_DVH), lambda b, s: (0, 0)),
            pl.BlockSpec((ts, ts), lambda b, s: (0, 0)),
        ],
        out_specs=pl.BlockSpec((1, ts, GLA_DV), blk),
        out_shape=jax.ShapeDtypeStruct((B, S, GLA_DV), BF16),
        scratch_shapes=[
            pltpu.VMEM((GLA_HEADS, GLA_DVH, GLA_DKH), F32),
            pltpu.VMEM((ts, GLA_DK), BF16),
            pltpu.VMEM((ts, GLA_DV), F32),
            pltpu.VMEM((nch, GLA_HEADS, GLA_DVH, GLA_DKH), F32),
            pltpu.VMEM((nch, GLA_HEADS, GLA_DVH, GLA_DKH), BF16),
        ],
        compiler_params=_cparams(("parallel", "arbitrary")),
        name="gla",
    )(q, k, v, gk, r, head_norm, tri)


def _xa_kv_kernel(mem_ref, g_ref, w_ref, k_ref, v_ref):
    mn = _rms(mem_ref[0], g_ref[...]).astype(BF16)
    k_ref[0] = (_dot(mn, w_ref[:, :D_MODEL]) * (XA_HEAD_DIM ** -0.5)).astype(BF16)
    v_ref[0] = _dot(mn, w_ref[:, D_MODEL:]).astype(BF16)


def _xa_kv(mem, g_mem, w_kv, lead):
    B, M, D = mem.shape
    blk = lambda b: (b, 0, 0)
    return pl.pallas_call(
        _xa_kv_kernel,
        grid=(B,),
        in_specs=[pl.BlockSpec((1, M, D), blk), pl.BlockSpec((1, D), lambda b: (0, 0)),
                  _resident(w_kv, lead)],
        out_specs=[pl.BlockSpec((1, M, D), blk), pl.BlockSpec((1, M, D), blk)],
        out_shape=[jax.ShapeDtypeStruct((B, M, D), BF16)] * 2,
        compiler_params=_cparams(("parallel",)),
        name="xa_kv",
    )(mem, g_mem, w_kv)


XA_TM = 512


def _mix_out_xa_kernel(*refs, offsets):
    n = len(offsets)
    h_ref, gmix_ref, gpre_ref, gpost_ref = refs[:4]
    part_refs = refs[4:4 + n]
    wout_ref, wq_ref, k_ref, v_ref, wo_ref, o_ref = refs[4 + n:]
    m = None
    for p_ref, off in zip(part_refs, offsets):
        kk = p_ref.shape[-1]
        term = _dot(p_ref[0], wout_ref[off:off + kk, :])
        m = term if m is None else m + term
    x = h_ref[0] + _rms(m, gmix_ref[...])
    xn = _rms(x, gpre_ref[...]).astype(BF16)
    q = _dot(xn, wq_ref[...]).astype(BF16)
    heads = []
    for h in range(XA_HEADS):
        sl = slice(h * XA_HEAD_DIM, (h + 1) * XA_HEAD_DIM)
        s = _dot_nt(q[:, sl], k_ref[0, :, sl])
        p = jnp.exp(s - jnp.max(s, axis=-1, keepdims=True))
        oh = _dot(p.astype(BF16), v_ref[0, :, sl]) / jnp.sum(p, axis=-1, keepdims=True)
        heads.append(oh.astype(BF16))
    c = _dot(jnp.concatenate(heads, axis=-1), wo_ref[...])
    o_ref[0] = x + _rms(c, gpost_ref[...])


def _mix_out_xa(h, g_mix, g_pre, g_post, parts, w_out, out_lead, w_q, kx, vx, w_o, lead):
    B, S, D = h.shape
    M = kx.shape[1]
    tm = min(XA_TM, S)
    blk = lambda b, i: (b, i, 0)
    fixed = lambda b, i: (0, 0)
    offsets, off = [], 0
    for p in parts:
        offsets.append(off)
        off += p.shape[-1]
    return pl.pallas_call(
        functools.partial(_mix_out_xa_kernel, offsets=tuple(offsets)),
        grid=(B, S // tm),
        in_specs=[
            pl.BlockSpec((1, tm, D), blk),
            pl.BlockSpec((1, D), fixed),
            pl.BlockSpec((1, D), fixed),
            pl.BlockSpec((1, D), fixed),
            *[pl.BlockSpec((1, tm, p.shape[-1]), blk) for p in parts],
            _resident(w_out, out_lead),
            _resident(w_q, lead),
            pl.BlockSpec((1, M, D), lambda b, i: (b, 0, 0)),
            pl.BlockSpec((1, M, D), lambda b, i: (b, 0, 0)),
            _resident(w_o, lead),
        ],
        out_specs=pl.BlockSpec((1, tm, D), blk),
        out_shape=jax.ShapeDtypeStruct((B, S, D), F32),
        compiler_params=_cparams(("parallel", "parallel")),
        name="mix_out_xa",
    )(h, g_mix, g_pre, g_post, *parts, w_out, w_q, kx, vx, w_o)


def _block_diag(w):
    G, n, _ = w.shape
    eye = jnp.eye(G, dtype=w.dtype)
    return (eye[:, None, :, None] * w[:, :, None, :]).reshape(G * n, G * n)


def _even_w_in(w):
    ki = w[:, 1280:1344]
    pad = jnp.zeros((w.shape[0], LANES - IDX_HEADS), w.dtype)
    return jnp.concatenate([w[:, :1280], ki, ki, w[:, 1352:2376], w[:, 1344:1352], pad], axis=1).astype(BF16)


def _odd_w_in(w):
    pad = jnp.zeros((w.shape[0], LANES - GLA_GATE_RANK), w.dtype)
    return jnp.concatenate([w[:, :2048], w[:, 2064:3088], w[:, 2048:2064], pad], axis=1).astype(BF16)


def kernel(x, mem, norms, ffn_w_gu, ffn_w_down, xa_w_q, xa_w_kv, xa_w_o, ev_w_in, ev_kv_norm, ev_w_uk, ev_w_uv, ev_conv_w, ev_conv_b, ev_w_ra, ev_b_ra, ev_w_ri, ev_b_ri, ev_lam, ev_w_out, od_w_in, od_w_g2, od_b_g, od_head_norm, od_w_out):
    B, S, D = x.shape
    T = B * S
    depth = norms.shape[0]
    h = x.reshape(T, D)

    def gain(layer, idx):
        return norms[layer, idx][None, :]

    ffn_gu, ffn_down = ffn_w_gu.astype(BF16), ffn_w_down.astype(BF16)
    w_q, w_kv, w_o = xa_w_q.astype(BF16), xa_w_kv.astype(BF16), xa_w_o.astype(BF16)
    ev_out, od_out = ev_w_out.astype(BF16), od_w_out.astype(BF16)

    for layer in range(depth):
        h = _ffn(h, gain(layer, N_FFN1_PRE), gain(layer, N_FFN1_POST), ffn_gu, ffn_down, (layer, 0))

        if layer % 2 == 0:
            e = layer // 2
            lru_params = (ev_conv_w[e], ev_conv_b[e][None, :],
                          _block_diag(ev_w_ra[e]).astype(BF16), ev_b_ra[e].reshape(1, B_WIDTH),
                          _block_diag(ev_w_ri[e]).astype(BF16), ev_b_ri[e].reshape(1, B_WIDTH),
                          ev_lam[e][None, :])
            qt, k, vt, qit, ki2, wit, b_out = _even_proj(
                h, gain(layer, N_MIX_PRE), _even_w_in(ev_w_in[e]), ev_kv_norm[e][None, :],
                ev_w_uk[e].astype(BF16), ev_w_uv[e].astype(BF16), lru_params, S)
            a_out = _dsa(qt, k, vt, qit, ki2, wit, B)
            parts = [a_out.reshape(B, S, A_WIDTH), b_out.reshape(B, S, B_WIDTH)]
            w_out, w_out_lead = ev_out, (e,)
        else:
            o = layer // 2
            w_g2 = jnp.concatenate(
                [od_w_g2[o], jnp.zeros((LANES - GLA_GATE_RANK, GLA_DK), od_w_g2.dtype)], axis=0).astype(BF16)
            q, k, v, r, gk = _odd_proj(h, gain(layer, N_MIX_PRE), _odd_w_in(od_w_in[o]), w_g2,
                                       od_b_g[o][None, :])
            r3 = lambda a: a.reshape(B, S, a.shape[-1])
            g_out = _gla(r3(q), r3(k), r3(v), r3(gk), r3(r), od_head_norm[o][None, :])
            parts = [g_out]
            w_out, w_out_lead = od_out, (o,)

        kx, vx = _xa_kv(mem, gain(layer, N_MEM_NORM), w_kv, (layer,))
        h = _mix_out_xa(h.reshape(B, S, D), gain(layer, N_MIX_POST), gain(layer, N_XA_PRE),
                        gain(layer, N_XA_POST), parts, w_out, w_out_lead,
                        w_q, kx, vx, w_o, (layer,)).reshape(T, D)

        h = _ffn(h, gain(layer, N_FFN2_PRE), gain(layer, N_FFN2_POST), ffn_gu, ffn_down, (layer, 1))
    return h.reshape(B, S, D)
```

```python
import functools

import jax
import jax.numpy as jnp
from jax import lax
from jax.experimental import pallas as pl
from jax.experimental.pallas import tpu as pltpu

F32 = jnp.float32
BF16 = jnp.bfloat16

EPS = 1e-6
D_MODEL = 1024
D_FF = 2816
XA_HEADS = 4
XA_HEAD_DIM = D_MODEL // XA_HEADS
A_HEADS = 8
A_HEAD_DIM = 64
A_WIDTH = A_HEADS * A_HEAD_DIM
KV_RANK = 256
IDX_HEADS = 8
IDX_DIM = 64
TOPK_MAX = 256
B_WIDTH = D_MODEL - A_WIDTH
B_BLOCKS = 8
B_BLOCK_DIM = B_WIDTH // B_BLOCKS
CONV_W = 4
LRU_C = 8.0
GLA_HEADS = 4
GLA_DK = D_MODEL // 2
GLA_DV = D_MODEL
GLA_DKH = GLA_DK // GLA_HEADS
GLA_DVH = GLA_DV // GLA_HEADS
GLA_GATE_RANK = 16
GLA_TAU = 16.0
GLA_CHUNK = 64
(N_FFN1_PRE, N_FFN1_POST, N_MIX_PRE, N_MIX_POST, N_XA_PRE, N_XA_POST, N_MEM_NORM,
 N_FFN2_PRE, N_FFN2_POST) = range(9)

LANES = 128
SUBLANES = 8
VMEM_LIMIT = 48 * 1024 * 1024

NEG_BIG = -1e30
LOG2E = 1.4426950408889634
INT_MIN = -2 ** 31
POS_INF_CODE = 0x7F800000
NEG_INF_CODE = -0x7F800001


def _cparams(sem):
    return pltpu.CompilerParams(dimension_semantics=sem, vmem_limit_bytes=VMEM_LIMIT)


def _rms(x, g):
    return x * lax.rsqrt(jnp.mean(x * x, axis=-1, keepdims=True) + EPS) * g


def _dot(a, b):
    return jnp.dot(a, b, preferred_element_type=F32)


def _dot_nt(a, b):
    return lax.dot_general(a, b, (((1,), (1,)), ((), ())), preferred_element_type=F32)


def _dot_tn(a, b):
    return lax.dot_general(a, b, (((0,), (0,)), ((), ())), preferred_element_type=F32)


def _sigmoid(x):
    return 1.0 / (1.0 + jnp.exp(-x))


FFN_TM = 1024
FFN_TF = 256


def _ffn_kernel(h_ref, gpre_ref, gpost_ref, wgu_ref, wd_ref, o_ref, act_ref):
    F = wd_ref.shape[0]
    x = h_ref[...]
    xn = _rms(x, gpre_ref[...]).astype(BF16)
    for c in range(F // FFN_TF):
        g = _dot(xn, wgu_ref[:, c * FFN_TF:(c + 1) * FFN_TF])
        u = _dot(xn, wgu_ref[:, F + c * FFN_TF:F + (c + 1) * FFN_TF])
        act_ref[:, c * FFN_TF:(c + 1) * FFN_TF] = (g * _sigmoid(g) * u).astype(BF16)
    f = _dot(act_ref[...], wd_ref[...])
    o_ref[...] = x + 0.5 * _rms(f, gpost_ref[...])


def _resident(arr, lead=()):
    tail = arr.shape[len(lead):]
    index = tuple(lead) + (0,) * len(tail)
    return pl.BlockSpec((None,) * len(lead) + tail, lambda *_: index, pipeline_mode=pl.Buffered(1))


def _ffn(h, g_pre, g_post, w_gu, w_down, lead):
    T, D = h.shape
    F = w_down.shape[-2]
    tm = min(FFN_TM, T)
    return pl.pallas_call(
        _ffn_kernel,
        grid=(T // tm,),
        in_specs=[
            pl.BlockSpec((tm, D), lambda i: (i, 0)),
            pl.BlockSpec((1, D), lambda i: (0, 0)),
            pl.BlockSpec((1, D), lambda i: (0, 0)),
            _resident(w_gu, lead),
            _resident(w_down, lead),
        ],
        out_specs=pl.BlockSpec((tm, D), lambda i: (i, 0)),
        out_shape=jax.ShapeDtypeStruct((T, D), F32),
        scratch_shapes=[pltpu.VMEM((tm, F), BF16)],
        compiler_params=_cparams(("parallel",)),
        name="ffn",
    )(h, g_pre, g_post, w_gu, w_down)


PROJ_TM = 512


EV_Q = (0, 512)
EV_CKV = (512, 768)
EV_QI = (768, 1280)
EV_KI2 = (1280, 1408)
EV_GATE = (1408, 1920)
EV_XB = (1920, 2432)
EV_WI = (2432, 2560)
EV_COLS = 2560


DSA_TQ = 256
DSA_CK = 512


def _even_proj_kernel(h_ref, g_ref, w_ref, kvn_ref, wuk_ref, wuv_ref,
                      cw_ref, cb_ref, wra_ref, bra_ref, wri_ref, bri_ref, lam_ref,
                      qt_ref, k_ref, vt_ref, qit_ref, ki_ref, wit_ref, bout_ref,
                      xs_ref, a_ref, b_ref, hc_ref, gg_ref, *, tiles_per_seq):
    _rglru_reset(pl.program_id(0) % tiles_per_seq == 0, xs_ref, hc_ref)
    xn = _rms(h_ref[...], g_ref[...]).astype(BF16)

    def seg(ab):
        return _dot(xn, w_ref[:, ab[0]:ab[1]])

    _rglru_gates(seg(EV_XB), cw_ref, cb_ref, wra_ref, bra_ref, wri_ref, bri_ref, lam_ref,
                 xs_ref, a_ref, b_ref)
    gg_ref[...] = _gelu_tanh(seg(EV_GATE))
    qt_ref[...] = (seg(EV_Q) * (A_HEAD_DIM ** -0.5 * LOG2E)).T.astype(BF16)
    ckv = _rms(seg(EV_CKV), kvn_ref[...]).astype(BF16)
    k_ref[...] = _dot(ckv, wuk_ref[...]).astype(BF16)
    vt_ref[0] = _dot(ckv, wuv_ref[...]).astype(BF16).T
    qit_ref[...] = (seg(EV_QI) * (IDX_DIM ** -0.5)).T.astype(BF16)
    ki_ref[...] = seg(EV_KI2).astype(BF16)
    wit_ref[...] = (seg(EV_WI) * (IDX_HEADS ** -0.5)).T[:IDX_HEADS, :]
    _rglru_scan(a_ref, b_ref, hc_ref)
    bout_ref[...] = (a_ref[...] * gg_ref[...]).astype(bout_ref.dtype)


def _even_proj(h, g_pre, w_in, kv_norm, w_uk, w_uv, lru_params, seq_len):
    T, D = h.shape
    tm = DSA_CK
    C = B_WIDTH
    row = lambda i: (i, 0)
    col = lambda i: (0, i)
    fixed = lambda i: (0, 0)
    out_specs = [
        pl.BlockSpec((A_WIDTH, tm), col),
        pl.BlockSpec((tm, A_WIDTH), row),
        pl.BlockSpec((1, A_WIDTH, tm), lambda i: (i, 0, 0)),
        pl.BlockSpec((IDX_HEADS * IDX_DIM, tm), col),
        pl.BlockSpec((tm, 2 * IDX_DIM), row),
        pl.BlockSpec((IDX_HEADS, tm), col),
        pl.BlockSpec((tm, C), row),
    ]
    out_shape = [
        jax.ShapeDtypeStruct((A_WIDTH, T), BF16),
        jax.ShapeDtypeStruct((T, A_WIDTH), BF16),
        jax.ShapeDtypeStruct((T // tm, A_WIDTH, tm), BF16),
        jax.ShapeDtypeStruct((IDX_HEADS * IDX_DIM, T), BF16),
        jax.ShapeDtypeStruct((T, 2 * IDX_DIM), BF16),
        jax.ShapeDtypeStruct((IDX_HEADS, T), F32),
        jax.ShapeDtypeStruct((T, C), BF16),
    ]
    return pl.pallas_call(
        functools.partial(_even_proj_kernel, tiles_per_seq=seq_len // tm),
        grid=(T // tm,),
        in_specs=[
            pl.BlockSpec((tm, D), row),
            pl.BlockSpec((1, D), fixed),
            pl.BlockSpec(w_in.shape, fixed),
            pl.BlockSpec((1, KV_RANK), fixed),
            pl.BlockSpec(w_uk.shape, fixed),
            pl.BlockSpec(w_uv.shape, fixed),
            *[pl.BlockSpec(p.shape, fixed) for p in lru_params],
        ],
        out_specs=out_specs,
        out_shape=out_shape,
        scratch_shapes=[
            pltpu.VMEM((HALO + tm, C), F32),
            pltpu.VMEM((tm, C), F32),
            pltpu.VMEM((tm, C), F32),
            pltpu.VMEM((SUBLANES, C), F32),
            pltpu.VMEM((tm, C), F32),
        ],
        compiler_params=_cparams(("arbitrary",)),
        name="even_proj",
    )(h, g_pre, w_in, kv_norm, w_uk, w_uv, *lru_params)


ACC_ROWS = 4 * SUBLANES
ACC16_ROWS = 2 * ACC_ROWS
SEARCH_PASSES = 18


def _col_partial(x, op):
    rows, n = x.shape
    part = x.reshape(rows // ACC_ROWS, ACC_ROWS, n)
    return jnp.max(part, axis=0) if op == "max" else jnp.sum(part, axis=0)


def _col_reduce(x, op):
    part = _col_partial(x, op)
    return (jnp.max(part, axis=0, keepdims=True) if op == "max"
            else jnp.sum(part, axis=0, keepdims=True))


def _dsa_kernel(qt_ref, qit_ref, wit_ref, k_ref, vt_ref, ki_ref, tri_ref, o_ref,
                sc_ref, hb_ref, bias_ref, s_ref, qm_ref, qim_ref, m_ref, l_ref, acc_ref, need_ref, seen_ref,
                *, top_k):
    TQ, CK = DSA_TQ, DSA_CK
    j = pl.program_id(1)
    q0 = j * TQ
    nkc = (q0 + TQ + CK - 1) // CK

    low_half = lax.broadcasted_iota(jnp.int32, (LANES, TQ), 0) < A_HEAD_DIM
    for h in range(A_HEADS):
        pr = slice((h // 2) * LANES, (h // 2 + 1) * LANES)
        keep = low_half if h % 2 == 0 else jnp.logical_not(low_half)
        qm_ref[h] = jnp.where(keep, qt_ref[pr, :], jnp.zeros((), BF16))
        qim_ref[h] = jnp.where(keep, qit_ref[pr, :], jnp.zeros((), BF16))

    key_iota = lax.broadcasted_iota(jnp.int32, (CK, TQ), 0)
    q_pos = q0 + lax.broadcasted_iota(jnp.int32, (CK, TQ), 1)

    def score_chunk(c, carry):
        base = pl.multiple_of(c * CK, CK)
        kic = ki_ref[pl.ds(base, CK), :]
        acc = jnp.zeros((CK, TQ), F32)
        for h in range(IDX_HEADS):
            acc = acc + jnp.maximum(_dot(kic, qim_ref[h]), 0.0) * wit_ref[h:h + 1, :]
        sc = jnp.where(base + key_iota <= q_pos, acc, -jnp.inf)
        sc_ref[c] = sc
        hb_ref[c] = sc.astype(BF16)
        return carry

    lax.fori_loop(0, nkc, score_chunk, 0)

    def count(pred_fn):
        def body(c, cnt):
            for g in range(CK // ACC_ROWS):
                hit = pred_fn(sc_ref[c, g * ACC_ROWS:(g + 1) * ACC_ROWS, :])
                cnt = jnp.where(hit, cnt + 1, cnt)
            return cnt
        cnt = lax.fori_loop(0, nkc, body, jnp.zeros((ACC_ROWS, TQ), jnp.int32))
        return jnp.sum(cnt, axis=0, keepdims=True)

    def code_to_float(code):
        code = jnp.clip(code, NEG_INF_CODE, POS_INF_CODE)
        return lax.bitcast_convert_type(code ^ ((code >> 31) & jnp.int32(0x7FFFFFFF)), F32)

    def count16(cand_b):
        def body(c, cnt):
            for g in range(CK // ACC16_ROWS):
                blk = hb_ref[c, g * ACC16_ROWS:(g + 1) * ACC16_ROWS, :]
                cnt = jnp.where(blk >= cand_b, cnt + jnp.ones((), BF16), cnt)
            return cnt
        cnt = lax.fori_loop(0, nkc, body, jnp.zeros((ACC16_ROWS, TQ), BF16))
        return jnp.sum(cnt.astype(F32), axis=0, keepdims=True)

    def bit_pass16(i, prefix):
        cand = prefix + lax.shift_left(jnp.int32(1), 31 - i)
        grid = jnp.where(cand < 0, cand | jnp.int32(0xFFFF), cand)
        cand_b = jnp.broadcast_to(code_to_float(grid), (ACC16_ROWS, TQ)).astype(BF16)
        return jnp.where(count16(cand_b) >= top_k, cand, prefix)

    prefix = lax.fori_loop(0, 16, bit_pass16, jnp.full((1, TQ), INT_MIN, jnp.int32))

    def search_pass(i, carry):
        lo, hi, n_at = carry
        mid = lo + ((hi - lo) >> 1)
        mid_f = code_to_float(mid)
        n_ge = count(lambda sc: sc >= mid_f)
        take = n_ge >= top_k
        return jnp.where(take, mid, lo), jnp.where(take, hi, mid), jnp.where(take, n_ge, n_at)

    lo0 = jnp.maximum(prefix, INT_MIN + 0x10000) - 0x10000
    hi0 = jnp.minimum(prefix, 0x7FFD0000) + 0x20000
    code, _, n_at = lax.fori_loop(
        0, SEARCH_PASSES, search_pass, (lo0, hi0, jnp.full((1, TQ), nkc * CK, jnp.int32)))
    thr = code_to_float(code)
    finite = thr > -jnp.inf
    has_ties = jnp.max(jnp.where(finite & (n_at > top_k), 1, 0)) > 0
    thr_sel = jnp.where(finite, thr, jnp.finfo(F32).min)

    m_ref[...] = jnp.full(m_ref.shape, -jnp.inf, F32)
    l_ref[...] = jnp.zeros(l_ref.shape, F32)
    acc_ref[...] = jnp.zeros(acc_ref.shape, F32)
    need_ref[...] = jnp.zeros(need_ref.shape, F32)
    seen_ref[...] = jnp.zeros(seen_ref.shape, F32)

    @pl.when(has_ties)
    def _():
        n_gt = count(lambda sc: sc > thr)
        need_ref[...] = jnp.where(finite, (top_k - n_gt).astype(F32), 0.0)

    def attend_chunk(c, carry):
        base = pl.multiple_of(c * CK, CK)

        @pl.when(jnp.logical_not(has_ties))
        def _():
            bias_ref[...] = jnp.where(sc_ref[c] >= thr_sel, 0.0, NEG_BIG)

        @pl.when(has_ties)
        def _():
            sc = sc_ref[c]
            eq = sc == thr
            rank = seen_ref[...] + _dot(tri_ref[...], eq.astype(BF16))
            sel = (sc > thr) | (eq & (rank <= need_ref[...]))
            bias_ref[...] = jnp.where(sel, 0.0, NEG_BIG)
            seen_ref[...] += _col_reduce(eq.astype(F32), "sum")

        cmax = []
        for h in range(A_HEADS):
            kp = k_ref[pl.ds(base, CK), (h // 2) * LANES:(h // 2 + 1) * LANES]
            s = _dot(kp, qm_ref[h]) + bias_ref[...]
            s_ref[h] = s
            cmax.append(_col_reduce(s, "max"))
        for h in range(A_HEADS):
            m_old = m_ref[h]
            m_new = jnp.maximum(m_old, cmax[h])
            alpha = jnp.exp2(m_old - m_new)
            p = jnp.exp2(s_ref[h] - m_new)
            l_ref[h] = alpha * l_ref[h] + _col_reduce(p, "sum")
            m_ref[h] = m_new
            vth = vt_ref[c, h * A_HEAD_DIM:(h + 1) * A_HEAD_DIM, :]
            acc_ref[h] = acc_ref[h] * alpha + _dot(vth, p.astype(BF16))
        return carry

    lax.fori_loop(0, nkc, attend_chunk, 0)

    out_t = jnp.concatenate([acc_ref[h] / l_ref[h] for h in range(A_HEADS)], axis=0)
    o_ref[...] = out_t.T.astype(o_ref.dtype)


def _dsa(qt, k, vt, qit, ki2, wit, B):
    T = k.shape[0]
    S = T // B
    TQ, CK = DSA_TQ, DSA_CK
    nc, nq = S // CK, S // TQ
    top_k = min(TOPK_MAX, S // 4)
    tri = (jnp.arange(CK)[:, None] >= jnp.arange(CK)[None, :]).astype(BF16)
    qcol = lambda b, j: (0, b * nq + j)
    return pl.pallas_call(
        functools.partial(_dsa_kernel, top_k=top_k),
        grid=(B, nq),
        in_specs=[
            pl.BlockSpec((A_WIDTH, TQ), qcol),
            pl.BlockSpec((IDX_HEADS * IDX_DIM, TQ), qcol),
            pl.BlockSpec((IDX_HEADS, TQ), qcol),
            pl.BlockSpec((S, A_WIDTH), lambda b, j: (b, 0)),
            pl.BlockSpec((nc, A_WIDTH, CK), lambda b, j: (b, 0, 0)),
            pl.BlockSpec((S, 2 * IDX_DIM), lambda b, j: (b, 0)),
            pl.BlockSpec((CK, CK), lambda b, j: (0, 0)),
        ],
        out_specs=pl.BlockSpec((TQ, A_WIDTH), lambda b, j: (b * nq + j, 0)),
        out_shape=jax.ShapeDtypeStruct((T, A_WIDTH), BF16),
        scratch_shapes=[
            pltpu.VMEM((nc, CK, TQ), F32),
            pltpu.VMEM((nc, CK, TQ), BF16),
            pltpu.VMEM((CK, TQ), F32),
            pltpu.VMEM((A_HEADS, CK, TQ), F32),
            pltpu.VMEM((A_HEADS, LANES, TQ), BF16),
            pltpu.VMEM((IDX_HEADS, LANES, TQ), BF16),
            pltpu.VMEM((A_HEADS, 1, TQ), F32),
            pltpu.VMEM((A_HEADS, 1, TQ), F32),
            pltpu.VMEM((A_HEADS, A_HEAD_DIM, TQ), F32),
            pltpu.VMEM((1, TQ), F32),
            pltpu.VMEM((1, TQ), F32),
        ],
        compiler_params=_cparams(("parallel", "arbitrary")),
        name="dsa",
    )(qt, qit, wit, k, vt, ki2, tri)


HALO = SUBLANES


def _softplus(x):
    return jnp.maximum(x, 0.0) + jnp.log1p(jnp.exp(-jnp.abs(x)))


def _gelu_tanh(x):
    return 0.5 * x * (1.0 + jnp.tanh(0.7978845608028654 * (x + 0.044715 * (x * x * x))))


def _rglru_reset(first, xs_ref, hc_ref):
    @pl.when(first)
    def _():
        xs_ref[0:HALO, :] = jnp.zeros((HALO, xs_ref.shape[1]), F32)
        hc_ref[...] = jnp.zeros(hc_ref.shape, F32)


def _rglru_gates(x, cw_ref, cb_ref, wra_ref, bra_ref, wri_ref, bri_ref, lam_ref, xs_ref, a_ref, b_ref):
    ts, C = x.shape
    xs_ref[HALO:HALO + ts, :] = x
    xc = cb_ref[...] + jnp.zeros((ts, C), F32)
    for kk in range(CONV_W):
        off = HALO - (CONV_W - 1) + kk
        xc = xc + cw_ref[kk:kk + 1, :] * xs_ref[off:off + ts, :]
    xs_ref[0:HALO, :] = x[ts - HALO:ts, :]

    xcb = xc.astype(BF16)
    r = _sigmoid(_dot(xcb, wra_ref[...]) + bra_ref[...])
    gi = _sigmoid(_dot(xcb, wri_ref[...]) + bri_ref[...])
    log_a = (-LRU_C) * r * _softplus(-lam_ref[...])
    a = jnp.exp(log_a)
    a_ref[...] = a
    b_ref[...] = jnp.sqrt(-jnp.tanh(log_a) * (1.0 + a * a)) * (gi * xc)


def _rglru_scan(a_ref, b_ref, hc_ref):
    ts, C = a_ref.shape
    row = lax.broadcasted_iota(jnp.int32, (SUBLANES, C), 0)

    def group(g, carry):
        r0 = pl.multiple_of(g * SUBLANES, SUBLANES)
        av = a_ref[pl.ds(r0, SUBLANES), :]
        bv = b_ref[pl.ds(r0, SUBLANES), :]
        for sh in (1, 2, 4):
            a_sh = pltpu.roll(av, sh, axis=0)
            b_sh = pltpu.roll(bv, sh, axis=0)
            ok = row >= sh
            bv = jnp.where(ok, av * b_sh + bv, bv)
            av = jnp.where(ok, av * a_sh, av)
        h8 = av * carry + bv
        a_ref[pl.ds(r0, SUBLANES), :] = h8
        return jnp.broadcast_to(h8[SUBLANES - 1:SUBLANES, :], (SUBLANES, C))

    hc_ref[...] = lax.fori_loop(0, ts // SUBLANES, group, hc_ref[...])


OD_QKVR = 3072
OD_GLR = (3072, 3200)
OD_COLS = 3200


def _odd_proj_kernel(h_ref, g_ref, w_ref, wg2_ref, bg_ref, q_ref, k_ref, v_ref, r_ref, gk_ref):
    xn = _rms(h_ref[...], g_ref[...]).astype(BF16)

    def seg(a, b):
        return _dot(xn, w_ref[:, a:b])

    q_ref[...] = seg(0, GLA_DK) * (GLA_DKH ** -0.5)
    k_ref[...] = seg(GLA_DK, 2 * GLA_DK)
    v_ref[...] = seg(2 * GLA_DK, 2 * GLA_DK + GLA_DV).astype(BF16)
    r_ref[...] = seg(2 * GLA_DK + GLA_DV, OD_QKVR)
    glr = seg(*OD_GLR).astype(BF16)
    z = _dot(glr, wg2_ref[...]) + bg_ref[...]
    gk_ref[...] = (-_softplus(-z)) * (1.0 / GLA_TAU)


def _odd_proj(h, g_pre, w_in, w_g2, b_g):
    T, D = h.shape
    tm = min(PROJ_TM, T)
    row = lambda i: (i, 0)
    fixed = lambda i: (0, 0)
    outs = [(GLA_DK, F32), (GLA_DK, F32), (GLA_DV, BF16), (GLA_DV, F32), (GLA_DK, F32)]
    return pl.pallas_call(
        _odd_proj_kernel,
        grid=(T // tm,),
        in_specs=[
            pl.BlockSpec((tm, D), row),
            pl.BlockSpec((1, D), fixed),
            pl.BlockSpec(w_in.shape, fixed),
            pl.BlockSpec(w_g2.shape, fixed),
            pl.BlockSpec((1, GLA_DK), fixed),
        ],
        out_specs=[pl.BlockSpec((tm, n), row) for n, _ in outs],
        out_shape=[jax.ShapeDtypeStruct((T, n), dt) for n, dt in outs],
        compiler_params=_cparams(("parallel",)),
        name="odd_proj",
    )(h, g_pre, w_in, w_g2, b_g)


GLA_TS = 256


def _gla_kernel(q_ref, k_ref, v_ref, gk_ref, r_ref, hn_ref, tri_ref, o_ref,
                st_ref, qd_ref, oi_ref, u_ref, stb_ref):
    ts = q_ref.shape[1]
    C = GLA_CHUNK
    nch = ts // C

    @pl.when(pl.program_id(1) == 0)
    def _():
        st_ref[...] = jnp.zeros(st_ref.shape, F32)

    gk = gk_ref[0]
    g_hi = gk.astype(BF16)
    rem = gk - g_hi.astype(F32)
    g_mid = rem.astype(BF16)
    g_lo = (rem - g_mid.astype(F32)).astype(BF16)
    tri = tri_ref[...]
    G = _dot(tri, g_hi) + _dot(tri, g_mid) + _dot(tri, g_lo)

    kf = k_ref[0]
    qd_ref[...] = (q_ref[0] * jnp.exp(G)).astype(BF16)
    k_inv = (kf * jnp.exp(-G)).astype(BF16)
    g_last = [G[(c + 1) * C - 1:(c + 1) * C, :] for c in range(nch)]
    k_rem = jnp.concatenate(
        [kf[c * C:(c + 1) * C, :] * jnp.exp(g_last[c] - G[c * C:(c + 1) * C, :]) for c in range(nch)],
        axis=0).astype(BF16)

    ri = lax.broadcasted_iota(jnp.int32, (C, C), 0)
    ci = lax.broadcasted_iota(jnp.int32, (C, C), 1)
    tril = ri >= ci

    for c in range(nch):
        rows = slice(c * C, (c + 1) * C)
        for h in range(GLA_HEADS):
            ksl = slice(h * GLA_DKH, (h + 1) * GLA_DKH)
            vsl = slice(h * GLA_DVH, (h + 1) * GLA_DVH)
            vh = v_ref[0, rows, vsl]
            att = jnp.where(tril, _dot_nt(qd_ref[rows, ksl], k_inv[rows, ksl]), 0.0).astype(BF16)
            oi_ref[rows, vsl] = _dot(att, vh)
            u_ref[c, h] = _dot_tn(vh, k_rem[rows, ksl])

    for h in range(GLA_HEADS):
        ksl = slice(h * GLA_DKH, (h + 1) * GLA_DKH)
        st = st_ref[h]
        for c in range(nch):
            stb_ref[c, h] = st.astype(BF16)
            st = st * jnp.exp(g_last[c][:, ksl]) + u_ref[c, h]
        st_ref[h] = st

    for c in range(nch):
        rows = slice(c * C, (c + 1) * C)
        for h in range(GLA_HEADS):
            ksl = slice(h * GLA_DKH, (h + 1) * GLA_DKH)
            vsl = slice(h * GLA_DVH, (h + 1) * GLA_DVH)
            o = oi_ref[rows, vsl] + _dot_nt(qd_ref[rows, ksl], stb_ref[c, h])
            on = _rms(o, hn_ref[...])
            rr = r_ref[0, rows, vsl]
            o_ref[0, rows, vsl] = (on * (rr * _sigmoid(rr))).astype(o_ref.dtype)


def _gla(q, k, v, gk, r, head_norm):
    B, S, _ = q.shape
    ts = min(GLA_TS, S)
    nch = ts // GLA_CHUNK
    pos = jnp.arange(ts)
    tri = ((pos[:, None] >= pos[None, :])
           & (pos[:, None] // GLA_CHUNK == pos[None, :] // GLA_CHUNK)).astype(BF16)
    blk = lambda b, s: (b, s, 0)
    return pl.pallas_call(
        _gla_kernel,
        grid=(B, S // ts),
        in_specs=[
            pl.BlockSpec((1, ts, GLA_DK), blk),
            pl.BlockSpec((1, ts, GLA_DK), blk),
            pl.BlockSpec((1, ts, GLA_DV), blk),
            pl.BlockSpec((1, ts, GLA_DK), blk),
            pl.BlockSpec((1, ts, GLA_DV), blk),
            pl.BlockSpec((1, GLA_DVH), lambda b, s: (0, 0)),
            pl.BlockSpec((ts, ts), lambda b, s: (0, 0)),
        ],
        out_specs=pl.BlockSpec((1, ts, GLA_DV), blk),
        out_shape=jax.ShapeDtypeStruct((B, S, GLA_DV), BF16),
        scratch_shapes=[
            pltpu.VMEM((GLA_HEADS, GLA_DVH, GLA_DKH), F32),
            pltpu.VMEM((ts, GLA_DK), BF16),
            pltpu.VMEM((ts, GLA_DV), F32),
            pltpu.VMEM((nch, GLA_HEADS, GLA_DVH, GLA_DKH), F32),
            pltpu.VMEM((nch, GLA_HEADS, GLA_DVH, GLA_DKH), BF16),
        ],
        compiler_params=_cparams(("parallel", "arbitrary")),
        name="gla",
    )(q, k, v, gk, r, head_norm, tri)


def _xa_kv_kernel(mem_ref, g_ref, w_ref, k_ref, v_ref):
    mn = _rms(mem_ref[0], g_ref[...]).astype(BF16)
    k_ref[0] = (_dot(mn, w_ref[:, :D_MODEL]) * (XA_HEAD_DIM ** -0.5)).astype(BF16)
    v_ref[0] = _dot(mn, w_ref[:, D_MODEL:]).astype(BF16)


def _xa_kv(mem, g_mem, w_kv, lead):
    B, M, D = mem.shape
    blk = lambda b: (b, 0, 0)
    return pl.pallas_call(
        _xa_kv_kernel,
        grid=(B,),
        in_specs=[pl.BlockSpec((1, M, D), blk), pl.BlockSpec((1, D), lambda b: (0, 0)),
                  _resident(w_kv, lead)],
        out_specs=[pl.BlockSpec((1, M, D), blk), pl.BlockSpec((1, M, D), blk)],
        out_shape=[jax.ShapeDtypeStruct((B, M, D), BF16)] * 2,
        compiler_params=_cparams(("parallel",)),
        name="xa_kv",
    )(mem, g_mem, w_kv)


XA_TM = 512


def _mix_out_xa_kernel(*refs, offsets):
    n = len(offsets)
    h_ref, gmix_ref, gpre_ref, gpost_ref = refs[:4]
    part_refs = refs[4:4 + n]
    wout_ref, wq_ref, k_ref, v_ref, wo_ref, o_ref = refs[4 + n:]
    m = None
    for p_ref, off in zip(part_refs, offsets):
        kk = p_ref.shape[-1]
        term = _dot(p_ref[0], wout_ref[off:off + kk, :])
        m = term if m is None else m + term
    x = h_ref[0] + _rms(m, gmix_ref[...])
    xn = _rms(x, gpre_ref[...]).astype(BF16)
    q = _dot(xn, wq_ref[...]).astype(BF16)
    heads = []
    for h in range(XA_HEADS):
        sl = slice(h * XA_HEAD_DIM, (h + 1) * XA_HEAD_DIM)
        s = _dot_nt(q[:, sl], k_ref[0, :, sl])
        p = jnp.exp(s - jnp.max(s, axis=-1, keepdims=True))
        oh = _dot(p.astype(BF16), v_ref[0, :, sl]) / jnp.sum(p, axis=-1, keepdims=True)
        heads.append(oh.astype(BF16))
    c = _dot(jnp.concatenate(heads, axis=-1), wo_ref[...])
    o_ref[0] = x + _rms(c, gpost_ref[...])


def _mix_out_xa(h, g_mix, g_pre, g_post, parts, w_out, out_lead, w_q, kx, vx, w_o, lead):
    B, S, D = h.shape
    M = kx.shape[1]
    tm = min(XA_TM, S)
    blk = lambda b, i: (b, i, 0)
    fixed = lambda b, i: (0, 0)
    offsets, off = [], 0
    for p in parts:
        offsets.append(off)
        off += p.shape[-1]
    return pl.pallas_call(
        functools.partial(_mix_out_xa_kernel, offsets=tuple(offsets)),
        grid=(B, S // tm),
        in_specs=[
            pl.BlockSpec((1, tm, D), blk),
            pl.BlockSpec((1, D), fixed),
            pl.BlockSpec((1, D), fixed),
            pl.BlockSpec((1, D), fixed),
            *[pl.BlockSpec((1, tm, p.shape[-1]), blk) for p in parts],
            _resident(w_out, out_lead),
            _resident(w_q, lead),
            pl.BlockSpec((1, M, D), lambda b, i: (b, 0, 0)),
            pl.BlockSpec((1, M, D), lambda b, i: (b, 0, 0)),
            _resident(w_o, lead),
        ],
        out_specs=pl.BlockSpec((1, tm, D), blk),
        out_shape=jax.ShapeDtypeStruct((B, S, D), F32),
        compiler_params=_cparams(("parallel", "parallel")),
        name="mix_out_xa",
    )(h, g_mix, g_pre, g_post, *parts, w_out, w_q, kx, vx, w_o)


def _block_diag(w):
    G, n, _ = w.shape
    eye = jnp.eye(G, dtype=w.dtype)
    return (eye[:, None, :, None] * w[:, :, None, :]).reshape(G * n, G * n)


def _even_w_in(w):
    ki = w[:, 1280:1344]
    pad = jnp.zeros((w.shape[0], LANES - IDX_HEADS), w.dtype)
    return jnp.concatenate([w[:, :1280], ki, ki, w[:, 1352:2376], w[:, 1344:1352], pad], axis=1).astype(BF16)


def _odd_w_in(w):
    pad = jnp.zeros((w.shape[0], LANES - GLA_GATE_RANK), w.dtype)
    return jnp.concatenate([w[:, :2048], w[:, 2064:3088], w[:, 2048:2064], pad], axis=1).astype(BF16)


def kernel(x, mem, norms, ffn_w_gu, ffn_w_down, xa_w_q, xa_w_kv, xa_w_o, ev_w_in, ev_kv_norm, ev_w_uk, ev_w_uv, ev_conv_w, ev_conv_b, ev_w_ra, ev_b_ra, ev_w_ri, ev_b_ri, ev_lam, ev_w_out, od_w_in, od_w_g2, od_b_g, od_head_norm, od_w_out):
    B, S, D = x.shape
    T = B * S
    depth = norms.shape[0]
    h = x.reshape(T, D)

    def gain(layer, idx):
        return norms[layer, idx][None, :]

    ffn_gu, ffn_down = ffn_w_gu.astype(BF16), ffn_w_down.astype(BF16)
    w_q, w_kv, w_o = xa_w_q.astype(BF16), xa_w_kv.astype(BF16), xa_w_o.astype(BF16)
    ev_out, od_out = ev_w_out.astype(BF16), od_w_out.astype(BF16)

    for layer in range(depth):
        h = _ffn(h, gain(layer, N_FFN1_PRE), gain(layer, N_FFN1_POST), ffn_gu, ffn_down, (layer, 0))

        if layer % 2 == 0:
            e = layer // 2
            lru_params = (ev_conv_w[e], ev_conv_b[e][None, :],
                          _block_diag(ev_w_ra[e]).astype(BF16), ev_b_ra[e].reshape(1, B_WIDTH),
                          _block_diag(ev_w_ri[e]).astype(BF16), ev_b_ri[e].reshape(1, B_WIDTH),
                          ev_lam[e][None, :])
            qt, k, vt, qit, ki2, wit, b_out = _even_proj(
                h, gain(layer, N_MIX_PRE), _even_w_in(ev_w_in[e]), ev_kv_norm[e][None, :],
                ev_w_uk[e].astype(BF16), ev_w_uv[e].astype(BF16), lru_params, S)
            a_out = _dsa(qt, k, vt, qit, ki2, wit, B)
            parts = [a_out.reshape(B, S, A_WIDTH), b_out.reshape(B, S, B_WIDTH)]
            w_out, w_out_lead = ev_out, (e,)
        else:
            o = layer // 2
            w_g2 = jnp.concatenate(
                [od_w_g2[o], jnp.zeros((LANES - GLA_GATE_RANK, GLA_DK), od_w_g2.dtype)], axis=0).astype(BF16)
            q, k, v, r, gk = _odd_proj(h, gain(layer, N_MIX_PRE), _odd_w_in(od_w_in[o]), w_g2,
                                       od_b_g[o][None, :])
            r3 = lambda a: a.reshape(B, S, a.shape[-1])
            g_out = _gla(r3(q), r3(k), r3(v), r3(gk), r3(r), od_head_norm[o][None, :])
            parts = [g_out]
            w_out, w_out_lead = od_out, (o,)

        kx, vx = _xa_kv(mem, gain(layer, N_MEM_NORM), w_kv, (layer,))
        h = _mix_out_xa(h.reshape(B, S, D), gain(layer, N_MIX_POST), gain(layer, N_XA_PRE),
                        gain(layer, N_XA_POST), parts, w_out, w_out_lead,
                        w_q, kx, vx, w_o, (layer,)).reshape(T, D)

        h = _ffn(h, gain(layer, N_FFN2_PRE), gain(layer, N_FFN2_POST), ffn_gu, ffn_down, (layer, 1))
    return h.reshape(B, S, D)
```

```python
import functools

import jax
import jax.numpy as jnp
from jax import lax
from jax.experimental import pallas as pl
from jax.experimental.pallas import tpu as pltpu

F32 = jnp.float32
BF16 = jnp.bfloat16

EPS = 1e-6
D_MODEL = 1024
D_FF = 2816
XA_HEADS = 4
XA_HEAD_DIM = D_MODEL // XA_HEADS
A_HEADS = 8
A_HEAD_DIM = 64
A_WIDTH = A_HEADS * A_HEAD_DIM
KV_RANK = 256
IDX_HEADS = 8
IDX_DIM = 64
TOPK_MAX = 256
B_WIDTH = D_MODEL - A_WIDTH
B_BLOCKS = 8
B_BLOCK_DIM = B_WIDTH // B_BLOCKS
CONV_W = 4
LRU_C = 8.0
GLA_HEADS = 4
GLA_DK = D_MODEL // 2
GLA_DV = D_MODEL
GLA_DKH = GLA_DK // GLA_HEADS
GLA_DVH = GLA_DV // GLA_HEADS
GLA_GATE_RANK = 16
GLA_TAU = 16.0
GLA_CHUNK = 64
(N_FFN1_PRE, N_FFN1_POST, N_MIX_PRE, N_MIX_POST, N_XA_PRE, N_XA_POST, N_MEM_NORM,
 N_FFN2_PRE, N_FFN2_POST) = range(9)

LANES = 128
SUBLANES = 8
VMEM_LIMIT = 48 * 1024 * 1024

NEG_BIG = -1e30
LOG2E = 1.4426950408889634
INT_MIN = -2 ** 31
POS_INF_CODE = 0x7F800000
NEG_INF_CODE = -0x7F800001


def _cparams(sem):
    return pltpu.CompilerParams(dimension_semantics=sem, vmem_limit_bytes=VMEM_LIMIT)


def _rms(x, g):
    return x * lax.rsqrt(jnp.mean(x * x, axis=-1, keepdims=True) + EPS) * g


def _dot(a, b):
    return jnp.dot(a, b, preferred_element_type=F32)


def _dot_nt(a, b):
    return lax.dot_general(a, b, (((1,), (1,)), ((), ())), preferred_element_type=F32)


def _dot_tn(a, b):
    return lax.dot_general(a, b, (((0,), (0,)), ((), ())), preferred_element_type=F32)


def _sigmoid(x):
    return 1.0 / (1.0 + jnp.exp(-x))


FFN_TM = 1024
FFN_TF = 256


def _ffn_kernel(h_ref, gpre_ref, gpost_ref, wgu_ref, wd_ref, o_ref, act_ref):
    F = wd_ref.shape[0]
    x = h_ref[...]
    xn = _rms(x, gpre_ref[...]).astype(BF16)
    for c in range(F // FFN_TF):
        g = _dot(xn, wgu_ref[:, c * FFN_TF:(c + 1) * FFN_TF])
        u = _dot(xn, wgu_ref[:, F + c * FFN_TF:F + (c + 1) * FFN_TF])
        act_ref[:, c * FFN_TF:(c + 1) * FFN_TF] = (g * _sigmoid(g) * u).astype(BF16)
    f = _dot(act_ref[...], wd_ref[...])
    o_ref[...] = x + 0.5 * _rms(f, gpost_ref[...])


def _resident(arr, lead=()):
    tail = arr.shape[len(lead):]
    index = tuple(lead) + (0,) * len(tail)
    return pl.BlockSpec((None,) * len(lead) + tail, lambda *_: index, pipeline_mode=pl.Buffered(1))


def _ffn(h, g_pre, g_post, w_gu, w_down, lead):
    T, D = h.shape
    F = w_down.shape[-2]
    tm = min(FFN_TM, T)
    return pl.pallas_call(
        _ffn_kernel,
        grid=(T // tm,),
        in_specs=[
            pl.BlockSpec((tm, D), lambda i: (i, 0)),
            pl.BlockSpec((1, D), lambda i: (0, 0)),
            pl.BlockSpec((1, D), lambda i: (0, 0)),
            _resident(w_gu, lead),
            _resident(w_down, lead),
        ],
        out_specs=pl.BlockSpec((tm, D), lambda i: (i, 0)),
        out_shape=jax.ShapeDtypeStruct((T, D), F32),
        scratch_shapes=[pltpu.VMEM((tm, F), BF16)],
        compiler_params=_cparams(("parallel",)),
        name="ffn",
    )(h, g_pre, g_post, w_gu, w_down)


PROJ_TM = 1024


EV_Q = (0, 512)
EV_CKV = (512, 768)
EV_QI = (768, 1280)
EV_KI2 = (1280, 1408)
EV_GATE = (1408, 1920)
EV_XB = (1920, 2432)
EV_WI = (2432, 2560)
EV_COLS = 2560


DSA_TQ = 512
DSA_CK = 512


def _even_proj_kernel(h_ref, g_ref, w_ref, kvn_ref, wuk_ref, wuv_ref,
                      cw_ref, cb_ref, wra_ref, bra_ref, wri_ref, bri_ref, lam_ref,
                      qt_ref, k_ref, vt_ref, qit_ref, ki_ref, wit_ref, bout_ref,
                      xs_ref, a_ref, b_ref, hc_ref, gg_ref, *, tiles_per_seq):
    _rglru_reset(pl.program_id(0) % tiles_per_seq == 0, xs_ref, hc_ref)
    xn = _rms(h_ref[...], g_ref[...]).astype(BF16)

    def seg(ab):
        return _dot(xn, w_ref[:, ab[0]:ab[1]])

    _rglru_gates(seg(EV_XB), cw_ref, cb_ref, wra_ref, bra_ref, wri_ref, bri_ref, lam_ref,
                 xs_ref, a_ref, b_ref)
    gg_ref[...] = _gelu_tanh(seg(EV_GATE))
    qt_ref[...] = (seg(EV_Q) * (A_HEAD_DIM ** -0.5 * LOG2E)).T.astype(BF16)
    ckv = _rms(seg(EV_CKV), kvn_ref[...]).astype(BF16)
    k_ref[...] = _dot(ckv, wuk_ref[...]).astype(BF16)
    v = _dot(ckv, wuv_ref[...]).astype(BF16)
    for c in range(vt_ref.shape[0]):
        vt_ref[c] = v[c * DSA_CK:(c + 1) * DSA_CK, :].T
    qit_ref[...] = (seg(EV_QI) * (IDX_DIM ** -0.5)).T.astype(BF16)
    ki_ref[...] = seg(EV_KI2).astype(BF16)
    wit_ref[...] = (seg(EV_WI) * (IDX_HEADS ** -0.5)).T[:IDX_HEADS, :]
    _rglru_scan(a_ref, b_ref, hc_ref)
    bout_ref[...] = (a_ref[...] * gg_ref[...]).astype(bout_ref.dtype)


def _even_proj(h, g_pre, w_in, kv_norm, w_uk, w_uv, lru_params, seq_len):
    T, D = h.shape
    tm = min(PROJ_TM, seq_len)
    C = B_WIDTH
    row = lambda i: (i, 0)
    col = lambda i: (0, i)
    fixed = lambda i: (0, 0)
    out_specs = [
        pl.BlockSpec((A_WIDTH, tm), col),
        pl.BlockSpec((tm, A_WIDTH), row),
        pl.BlockSpec((tm // DSA_CK, A_WIDTH, DSA_CK), lambda i: (i, 0, 0)),
        pl.BlockSpec((IDX_HEADS * IDX_DIM, tm), col),
        pl.BlockSpec((tm, 2 * IDX_DIM), row),
        pl.BlockSpec((IDX_HEADS, tm), col),
        pl.BlockSpec((tm, C), row),
    ]
    out_shape = [
        jax.ShapeDtypeStruct((A_WIDTH, T), BF16),
        jax.ShapeDtypeStruct((T, A_WIDTH), BF16),
        jax.ShapeDtypeStruct((T // DSA_CK, A_WIDTH, DSA_CK), BF16),
        jax.ShapeDtypeStruct((IDX_HEADS * IDX_DIM, T), BF16),
        jax.ShapeDtypeStruct((T, 2 * IDX_DIM), BF16),
        jax.ShapeDtypeStruct((IDX_HEADS, T), F32),
        jax.ShapeDtypeStruct((T, C), BF16),
    ]
    return pl.pallas_call(
        functools.partial(_even_proj_kernel, tiles_per_seq=seq_len // tm),
        grid=(T // tm,),
        in_specs=[
            pl.BlockSpec((tm, D), row),
            pl.BlockSpec((1, D), fixed),
            _resident(w_in),
            pl.BlockSpec((1, KV_RANK), fixed),
            pl.BlockSpec(w_uk.shape, fixed),
            pl.BlockSpec(w_uv.shape, fixed),
            *[pl.BlockSpec(p.shape, fixed) for p in lru_params],
        ],
        out_specs=out_specs,
        out_shape=out_shape,
        scratch_shapes=[
            pltpu.VMEM((HALO + tm, C), F32),
            pltpu.VMEM((tm, C), F32),
            pltpu.VMEM((tm, C), F32),
            pltpu.VMEM((SUBLANES, C), F32),
            pltpu.VMEM((tm, C), F32),
        ],
        compiler_params=_cparams(("arbitrary",)),
        name="even_proj",
    )(h, g_pre, w_in, kv_norm, w_uk, w_uv, *lru_params)


ACC_ROWS = 4 * SUBLANES
ACC16_ROWS = 2 * ACC_ROWS
SEARCH_PASSES = 18


def _col_partial(x, op):
    rows, n = x.shape
    part = x.reshape(rows // ACC_ROWS, ACC_ROWS, n)
    return jnp.max(part, axis=0) if op == "max" else jnp.sum(part, axis=0)


def _col_reduce(x, op):
    part = _col_partial(x, op)
    return (jnp.max(part, axis=0, keepdims=True) if op == "max"
            else jnp.sum(part, axis=0, keepdims=True))


def _dsa_kernel(qt_ref, qit_ref, wit_ref, k_ref, vt_ref, ki_ref, tri_ref, o_ref,
                sc_ref, hb_ref, bias_ref, s_ref, qm_ref, qim_ref, m_ref, l_ref, acc_ref, need_ref, seen_ref,
                *, top_k):
    TQ, CK = DSA_TQ, DSA_CK
    j = pl.program_id(1)
    q0 = j * TQ
    nkc = (q0 + TQ + CK - 1) // CK

    low_half = lax.broadcasted_iota(jnp.int32, (LANES, TQ), 0) < A_HEAD_DIM
    for h in range(A_HEADS):
        pr = slice((h // 2) * LANES, (h // 2 + 1) * LANES)
        keep = low_half if h % 2 == 0 else jnp.logical_not(low_half)
        qm_ref[h] = jnp.where(keep, qt_ref[pr, :], jnp.zeros((), BF16))
        qim_ref[h] = jnp.where(keep, qit_ref[pr, :], jnp.zeros((), BF16))

    key_iota = lax.broadcasted_iota(jnp.int32, (CK, TQ), 0)
    q_pos = q0 + lax.broadcasted_iota(jnp.int32, (CK, TQ), 1)

    def score_chunk(c, carry):
        base = pl.multiple_of(c * CK, CK)
        kic = ki_ref[pl.ds(base, CK), :]
        acc = jnp.zeros((CK, TQ), F32)
        for h in range(IDX_HEADS):
            acc = acc + jnp.maximum(_dot(kic, qim_ref[h]), 0.0) * wit_ref[h:h + 1, :]
        sc = jnp.where(base + key_iota <= q_pos, acc, -jnp.inf)
        sc_ref[c] = sc
        hb_ref[c] = sc.astype(BF16)
        return carry

    lax.fori_loop(0, nkc, score_chunk, 0)

    def count(pred_fn):
        def body(c, cnt):
            for g in range(CK // ACC_ROWS):
                hit = pred_fn(sc_ref[c, g * ACC_ROWS:(g + 1) * ACC_ROWS, :])
                cnt = jnp.where(hit, cnt + 1, cnt)
            return cnt
        cnt = lax.fori_loop(0, nkc, body, jnp.zeros((ACC_ROWS, TQ), jnp.int32))
        return jnp.sum(cnt, axis=0, keepdims=True)

    def code_to_float(code):
        code = jnp.clip(code, NEG_INF_CODE, POS_INF_CODE)
        return lax.bitcast_convert_type(code ^ ((code >> 31) & jnp.int32(0x7FFFFFFF)), F32)

    def count16(cand_b):
        def body(c, cnt):
            for g in range(CK // ACC16_ROWS):
                blk = hb_ref[c, g * ACC16_ROWS:(g + 1) * ACC16_ROWS, :]
                cnt = jnp.where(blk >= cand_b, cnt + jnp.ones((), BF16), cnt)
            return cnt
        cnt = lax.fori_loop(0, nkc, body, jnp.zeros((ACC16_ROWS, TQ), BF16))
        return jnp.sum(cnt.astype(F32), axis=0, keepdims=True)

    def bit_pass16(i, prefix):
        cand = prefix + lax.shift_left(jnp.int32(1), 31 - i)
        grid = jnp.where(cand < 0, cand | jnp.int32(0xFFFF), cand)
        cand_b = jnp.broadcast_to(code_to_float(grid), (ACC16_ROWS, TQ)).astype(BF16)
        return jnp.where(count16(cand_b) >= top_k, cand, prefix)

    prefix = lax.fori_loop(0, 16, bit_pass16, jnp.full((1, TQ), INT_MIN, jnp.int32))

    def search_pass(i, carry):
        lo, hi, n_at = carry
        mid = lo + ((hi - lo) >> 1)
        mid_f = code_to_float(mid)
        n_ge = count(lambda sc: sc >= mid_f)
        take = n_ge >= top_k
        return jnp.where(take, mid, lo), jnp.where(take, hi, mid), jnp.where(take, n_ge, n_at)

    lo0 = jnp.maximum(prefix, INT_MIN + 0x10000) - 0x10000
    hi0 = jnp.minimum(prefix, 0x7FFD0000) + 0x20000
    code, _, n_at = lax.fori_loop(
        0, SEARCH_PASSES, search_pass, (lo0, hi0, jnp.full((1, TQ), nkc * CK, jnp.int32)))
    thr = code_to_float(code)
    finite = thr > -jnp.inf
    has_ties = jnp.max(jnp.where(finite & (n_at > top_k), 1, 0)) > 0
    thr_sel = jnp.where(finite, thr, jnp.finfo(F32).min)

    m_ref[...] = jnp.full(m_ref.shape, -jnp.inf, F32)
    l_ref[...] = jnp.zeros(l_ref.shape, F32)
    acc_ref[...] = jnp.zeros(acc_ref.shape, F32)
    need_ref[...] = jnp.zeros(need_ref.shape, F32)
    seen_ref[...] = jnp.zeros(seen_ref.shape, F32)

    @pl.when(has_ties)
    def _():
        n_gt = count(lambda sc: sc > thr)
        need_ref[...] = jnp.where(finite, (top_k - n_gt).astype(F32), 0.0)

    def attend_chunk(c, carry):
        base = pl.multiple_of(c * CK, CK)

        @pl.when(jnp.logical_not(has_ties))
        def _():
            bias_ref[...] = jnp.where(sc_ref[c] >= thr_sel, 0.0, NEG_BIG)

        @pl.when(has_ties)
        def _():
            sc = sc_ref[c]
            eq = sc == thr
            rank = seen_ref[...] + _dot(tri_ref[...], eq.astype(BF16))
            sel = (sc > thr) | (eq & (rank <= need_ref[...]))
            bias_ref[...] = jnp.where(sel, 0.0, NEG_BIG)
            seen_ref[...] += _col_reduce(eq.astype(F32), "sum")

        cmax = []
        for h in range(A_HEADS):
            kp = k_ref[pl.ds(base, CK), (h // 2) * LANES:(h // 2 + 1) * LANES]
            s = _dot(kp, qm_ref[h]) + bias_ref[...]
            s_ref[h] = s
            cmax.append(_col_reduce(s, "max"))
        for h in range(A_HEADS):
            m_old = m_ref[h]
            m_new = jnp.maximum(m_old, cmax[h])
            alpha = jnp.exp2(m_old - m_new)
            p = jnp.exp2(s_ref[h] - m_new)
            l_ref[h] = alpha * l_ref[h] + _col_reduce(p, "sum")
            m_ref[h] = m_new
            vth = vt_ref[c, h * A_HEAD_DIM:(h + 1) * A_HEAD_DIM, :]
            acc_ref[h] = acc_ref[h] * alpha + _dot(vth, p.astype(BF16))
        return carry

    lax.fori_loop(0, nkc, attend_chunk, 0)

    out_t = jnp.concatenate([acc_ref[h] / l_ref[h] for h in range(A_HEADS)], axis=0)
    o_ref[...] = out_t.T.astype(o_ref.dtype)


def _dsa(qt, k, vt, qit, ki2, wit, B):
    T = k.shape[0]
    S = T // B
    TQ, CK = DSA_TQ, DSA_CK
    nc, nq = S // CK, S // TQ
    top_k = min(TOPK_MAX, S // 4)
    tri = (jnp.arange(CK)[:, None] >= jnp.arange(CK)[None, :]).astype(BF16)
    qcol = lambda b, j: (0, b * nq + j)
    return pl.pallas_call(
        functools.partial(_dsa_kernel, top_k=top_k),
        grid=(B, nq),
        in_specs=[
            pl.BlockSpec((A_WIDTH, TQ), qcol),
            pl.BlockSpec((IDX_HEADS * IDX_DIM, TQ), qcol),
            pl.BlockSpec((IDX_HEADS, TQ), qcol),
            pl.BlockSpec((S, A_WIDTH), lambda b, j: (b, 0)),
            pl.BlockSpec((nc, A_WIDTH, CK), lambda b, j: (b, 0, 0)),
            pl.BlockSpec((S, 2 * IDX_DIM), lambda b, j: (b, 0)),
            pl.BlockSpec((CK, CK), lambda b, j: (0, 0)),
        ],
        out_specs=pl.BlockSpec((TQ, A_WIDTH), lambda b, j: (b * nq + j, 0)),
        out_shape=jax.ShapeDtypeStruct((T, A_WIDTH), BF16),
        scratch_shapes=[
            pltpu.VMEM((nc, CK, TQ), F32),
            pltpu.VMEM((nc, CK, TQ), BF16),
            pltpu.VMEM((CK, TQ), F32),
            pltpu.VMEM((A_HEADS, CK, TQ), F32),
            pltpu.VMEM((A_HEADS, LANES, TQ), BF16),
            pltpu.VMEM((IDX_HEADS, LANES, TQ), BF16),
            pltpu.VMEM((A_HEADS, 1, TQ), F32),
            pltpu.VMEM((A_HEADS, 1, TQ), F32),
            pltpu.VMEM((A_HEADS, A_HEAD_DIM, TQ), F32),
            pltpu.VMEM((1, TQ), F32),
            pltpu.VMEM((1, TQ), F32),
        ],
        compiler_params=_cparams(("parallel", "arbitrary")),
        name="dsa",
    )(qt, qit, wit, k, vt, ki2, tri)


HALO = SUBLANES


def _softplus(x):
    return jnp.maximum(x, 0.0) + jnp.log1p(jnp.exp(-jnp.abs(x)))


def _gelu_tanh(x):
    return 0.5 * x * (1.0 + jnp.tanh(0.7978845608028654 * (x + 0.044715 * (x * x * x))))


def _rglru_reset(first, xs_ref, hc_ref):
    @pl.when(first)
    def _():
        xs_ref[0:HALO, :] = jnp.zeros((HALO, xs_ref.shape[1]), F32)
        hc_ref[...] = jnp.zeros(hc_ref.shape, F32)


def _rglru_gates(x, cw_ref, cb_ref, wra_ref, bra_ref, wri_ref, bri_ref, lam_ref, xs_ref, a_ref, b_ref):
    ts, C = x.shape
    xs_ref[HALO:HALO + ts, :] = x
    xc = cb_ref[...] + jnp.zeros((ts, C), F32)
    for kk in range(CONV_W):
        off = HALO - (CONV_W - 1) + kk
        xc = xc + cw_ref[kk:kk + 1, :] * xs_ref[off:off + ts, :]
    xs_ref[0:HALO, :] = x[ts - HALO:ts, :]

    xcb = xc.astype(BF16)
    r = _sigmoid(_dot(xcb, wra_ref[...]) + bra_ref[...])
    gi = _sigmoid(_dot(xcb, wri_ref[...]) + bri_ref[...])
    log_a = (-LRU_C) * r * _softplus(-lam_ref[...])
    a = jnp.exp(log_a)
    a_ref[...] = a
    b_ref[...] = jnp.sqrt(-jnp.tanh(log_a) * (1.0 + a * a)) * (gi * xc)


def _rglru_scan(a_ref, b_ref, hc_ref):
    ts, C = a_ref.shape
    row = lax.broadcasted_iota(jnp.int32, (SUBLANES, C), 0)

    def group(g, carry):
        r0 = pl.multiple_of(g * SUBLANES, SUBLANES)
        av = a_ref[pl.ds(r0, SUBLANES), :]
        bv = b_ref[pl.ds(r0, SUBLANES), :]
        for sh in (1, 2, 4):
            a_sh = pltpu.roll(av, sh, axis=0)
            b_sh = pltpu.roll(bv, sh, axis=0)
            ok = row >= sh
            bv = jnp.where(ok, av * b_sh + bv, bv)
            av = jnp.where(ok, av * a_sh, av)
        h8 = av * carry + bv
        a_ref[pl.ds(r0, SUBLANES), :] = h8
        return jnp.broadcast_to(h8[SUBLANES - 1:SUBLANES, :], (SUBLANES, C))

    hc_ref[...] = lax.fori_loop(0, ts // SUBLANES, group, hc_ref[...])


OD_QKVR = 3072
OD_GLR = (3072, 3200)
OD_COLS = 3200


def _odd_proj_kernel(h_ref, g_ref, w_ref, wg2_ref, bg_ref, q_ref, k_ref, v_ref, r_ref, gk_ref):
    xn = _rms(h_ref[...], g_ref[...]).astype(BF16)

    def seg(a, b):
        return _dot(xn, w_ref[:, a:b])

    q_ref[...] = seg(0, GLA_DK) * (GLA_DKH ** -0.5)
    k_ref[...] = seg(GLA_DK, 2 * GLA_DK)
    v_ref[...] = seg(2 * GLA_DK, 2 * GLA_DK + GLA_DV).astype(BF16)
    r_ref[...] = seg(2 * GLA_DK + GLA_DV, OD_QKVR)
    glr = seg(*OD_GLR).astype(BF16)
    z = _dot(glr, wg2_ref[...]) + bg_ref[...]
    gk_ref[...] = (-_softplus(-z)) * (1.0 / GLA_TAU)


def _odd_proj(h, g_pre, w_in, w_g2, b_g):
    T, D = h.shape
    tm = min(PROJ_TM, T)
    row = lambda i: (i, 0)
    fixed = lambda i: (0, 0)
    outs = [(GLA_DK, F32), (GLA_DK, F32), (GLA_DV, BF16), (GLA_DV, F32), (GLA_DK, F32)]
    return pl.pallas_call(
        _odd_proj_kernel,
        grid=(T // tm,),
        in_specs=[
            pl.BlockSpec((tm, D), row),
            pl.BlockSpec((1, D), fixed),
            _resident(w_in),
            pl.BlockSpec(w_g2.shape, fixed),
            pl.BlockSpec((1, GLA_DK), fixed),
        ],
        out_specs=[pl.BlockSpec((tm, n), row) for n, _ in outs],
        out_shape=[jax.ShapeDtypeStruct((T, n), dt) for n, dt in outs],
        compiler_params=_cparams(("parallel",)),
        name="odd_proj",
    )(h, g_pre, w_in, w_g2, b_g)


GLA_TS = 256


def _gla_kernel(q_ref, k_ref, v_ref, gk_ref, r_ref, hn_ref, tri_ref, o_ref,
                st_ref, qd_ref, oi_ref, u_ref, stb_ref):
    ts = q_ref.shape[1]
    C = GLA_CHUNK
    nch = ts // C

    @pl.when(pl.program_id(1) == 0)
    def _():
        st_ref[...] = jnp.zeros(st_ref.shape, F32)

    gk = gk_ref[0]
    g_hi = gk.astype(BF16)
    rem = gk - g_hi.astype(F32)
    g_mid = rem.astype(BF16)
    g_lo = (rem - g_mid.astype(F32)).astype(BF16)
    tri = tri_ref[...]
    G = _dot(tri, g_hi) + _dot(tri, g_mid) + _dot(tri, g_lo)

    kf = k_ref[0]
    qd_ref[...] = (q_ref[0] * jnp.exp(G)).astype(BF16)
    k_inv = (kf * jnp.exp(-G)).astype(BF16)
    g_last = [G[(c + 1) * C - 1:(c + 1) * C, :] for c in range(nch)]
    k_rem = jnp.concatenate(
        [kf[c * C:(c + 1) * C, :] * jnp.exp(g_last[c] - G[c * C:(c + 1) * C, :]) for c in range(nch)],
        axis=0).astype(BF16)

    ri = lax.broadcasted_iota(jnp.int32, (C, C), 0)
    ci = lax.broadcasted_iota(jnp.int32, (C, C), 1)
    tril = ri >= ci

    for c in range(nch):
        rows = slice(c * C, (c + 1) * C)
        for h in range(GLA_HEADS):
            ksl = slice(h * GLA_DKH, (h + 1) * GLA_DKH)
            vsl = slice(h * GLA_DVH, (h + 1) * GLA_DVH)
            vh = v_ref[0, rows, vsl]
            att = jnp.where(tril, _dot_nt(qd_ref[rows, ksl], k_inv[rows, ksl]), 0.0).astype(BF16)
            oi_ref[rows, vsl] = _dot(att, vh)
            u_ref[c, h] = _dot_tn(vh, k_rem[rows, ksl])

    for h in range(GLA_HEADS):
        ksl = slice(h * GLA_DKH, (h + 1) * GLA_DKH)
        st = st_ref[h]
        for c in range(nch):
            stb_ref[c, h] = st.astype(BF16)
            st = st * jnp.exp(g_last[c][:, ksl]) + u_ref[c, h]
        st_ref[h] = st

    for c in range(nch):
        rows = slice(c * C, (c + 1) * C)
        for h in range(GLA_HEADS):
            ksl = slice(h * GLA_DKH, (h + 1) * GLA_DKH)
            vsl = slice(h * GLA_DVH, (h + 1) * GLA_DVH)
            o = oi_ref[rows, vsl] + _dot_nt(qd_ref[rows, ksl], stb_ref[c, h])
            on = _rms(o, hn_ref[...])
            rr = r_ref[0, rows, vsl]
            o_ref[0, rows, vsl] = (on * (rr * _sigmoid(rr))).astype(o_ref.dtype)


def _gla(q, k, v, gk, r, head_norm):
    B, S, _ = q.shape
    ts = min(GLA_TS, S)
    nch = ts // GLA_CHUNK
    pos = jnp.arange(ts)
    tri = ((pos[:, None] >= pos[None, :])
           & (pos[:, None] // GLA_CHUNK == pos[None, :] // GLA_CHUNK)).astype(BF16)
    blk = lambda b, s: (b, s, 0)
    return pl.pallas_call(
        _gla_kernel,
        grid=(B, S // ts),
        in_specs=[
            pl.BlockSpec((1, ts, GLA_DK), blk),
            pl.BlockSpec((1, ts, GLA_DK), blk),
            pl.BlockSpec((1, ts, GLA_DV), blk),
            pl.BlockSpec((1, ts, GLA_DK), blk),
            pl.BlockSpec((1, ts, GLA_DV), blk),
            pl.BlockSpec((1, GLA_DVH), lambda b, s: (0, 0)),
            pl.BlockSpec((ts, ts), lambda b, s: (0, 0)),
        ],
        out_specs=pl.BlockSpec((1, ts, GLA_DV), blk),
        out_shape=jax.ShapeDtypeStruct((B, S, GLA_DV), BF16),
        scratch_shapes=[
            pltpu.VMEM((GLA_HEADS, GLA_DVH, GLA_DKH), F32),
            pltpu.VMEM((ts, GLA_DK), BF16),
            pltpu.VMEM((ts, GLA_DV), F32),
            pltpu.VMEM((nch, GLA_HEADS, GLA_DVH, GLA_DKH), F32),
            pltpu.VMEM((nch, GLA_HEADS, GLA_DVH, GLA_DKH), BF16),
        ],
        compiler_params=_cparams(("parallel", "arbitrary")),
        name="gla",
    )(q, k, v, gk, r, head_norm, tri)


def _xa_kv_kernel(mem_ref, g_ref, w_ref, k_ref, v_ref):
    mn = _rms(mem_ref[0], g_ref[...]).astype(BF16)
    k_ref[0] = (_dot(mn, w_ref[:, :D_MODEL]) * (XA_HEAD_DIM ** -0.5)).astype(BF16)
    v_ref[0] = _dot(mn, w_ref[:, D_MODEL:]).astype(BF16)


def _xa_kv(mem, g_mem, w_kv, lead):
    B, M, D = mem.shape
    blk = lambda b: (b, 0, 0)
    return pl.pallas_call(
        _xa_kv_kernel,
        grid=(B,),
        in_specs=[pl.BlockSpec((1, M, D), blk), pl.BlockSpec((1, D), lambda b: (0, 0)),
                  _resident(w_kv, lead)],
        out_specs=[pl.BlockSpec((1, M, D), blk), pl.BlockSpec((1, M, D), blk)],
        out_shape=[jax.ShapeDtypeStruct((B, M, D), BF16)] * 2,
        compiler_params=_cparams(("parallel",)),
        name="xa_kv",
    )(mem, g_mem, w_kv)


XA_TM = 1024


def _mix_out_xa_kernel(*refs, offsets):
    n = len(offsets)
    h_ref, gmix_ref, gpre_ref, gpost_ref = refs[:4]
    part_refs = refs[4:4 + n]
    wout_ref, wq_ref, k_ref, v_ref, wo_ref, o_ref = refs[4 + n:]
    m = None
    for p_ref, off in zip(part_refs, offsets):
        kk = p_ref.shape[-1]
        term = _dot(p_ref[0], wout_ref[off:off + kk, :])
        m = term if m is None else m + term
    x = h_ref[0] + _rms(m, gmix_ref[...])
    xn = _rms(x, gpre_ref[...]).astype(BF16)
    q = _dot(xn, wq_ref[...]).astype(BF16)
    heads = []
    for h in range(XA_HEADS):
        sl = slice(h * XA_HEAD_DIM, (h + 1) * XA_HEAD_DIM)
        s = _dot_nt(q[:, sl], k_ref[0, :, sl])
        p = jnp.exp(s - jnp.max(s, axis=-1, keepdims=True))
        oh = _dot(p.astype(BF16), v_ref[0, :, sl]) / jnp.sum(p, axis=-1, keepdims=True)
        heads.append(oh.astype(BF16))
    c = _dot(jnp.concatenate(heads, axis=-1), wo_ref[...])
    o_ref[0] = x + _rms(c, gpost_ref[...])


def _mix_out_xa(h, g_mix, g_pre, g_post, parts, w_out, out_lead, w_q, kx, vx, w_o, lead):
    B, S, D = h.shape
    M = kx.shape[1]
    tm = min(XA_TM, S)
    blk = lambda b, i: (b, i, 0)
    fixed = lambda b, i: (0, 0)
    offsets, off = [], 0
    for p in parts:
        offsets.append(off)
        off += p.shape[-1]
    return pl.pallas_call(
        functools.partial(_mix_out_xa_kernel, offsets=tuple(offsets)),
        grid=(B, S // tm),
        in_specs=[
            pl.BlockSpec((1, tm, D), blk),
            pl.BlockSpec((1, D), fixed),
            pl.BlockSpec((1, D), fixed),
            pl.BlockSpec((1, D), fixed),
            *[pl.BlockSpec((1, tm, p.shape[-1]), blk) for p in parts],
            _resident(w_out, out_lead),
            _resident(w_q, lead),
            pl.BlockSpec((1, M, D), lambda b, i: (b, 0, 0)),
            pl.BlockSpec((1, M, D), lambda b, i: (b, 0, 0)),
            _resident(w_o, lead),
        ],
        out_specs=pl.BlockSpec((1, tm, D), blk),
        out_shape=jax.ShapeDtypeStruct((B, S, D), F32),
        compiler_params=_cparams(("parallel", "parallel")),
        name="mix_out_xa",
    )(h, g_mix, g_pre, g_post, *parts, w_out, w_q, kx, vx, w_o)


def _block_diag(w):
    G, n, _ = w.shape
    eye = jnp.eye(G, dtype=w.dtype)
    return (eye[:, None, :, None] * w[:, :, None, :]).reshape(G * n, G * n)


def _even_w_in(w):
    ki = w[:, 1280:1344]
    pad = jnp.zeros((w.shape[0], LANES - IDX_HEADS), w.dtype)
    return jnp.concatenate([w[:, :1280], ki, ki, w[:, 1352:2376], w[:, 1344:1352], pad], axis=1).astype(BF16)


def _odd_w_in(w):
    pad = jnp.zeros((w.shape[0], LANES - GLA_GATE_RANK), w.dtype)
    return jnp.concatenate([w[:, :2048], w[:, 2064:3088], w[:, 2048:2064], pad], axis=1).astype(BF16)


def kernel(x, mem, norms, ffn_w_gu, ffn_w_down, xa_w_q, xa_w_kv, xa_w_o, ev_w_in, ev_kv_norm, ev_w_uk, ev_w_uv, ev_conv_w, ev_conv_b, ev_w_ra, ev_b_ra, ev_w_ri, ev_b_ri, ev_lam, ev_w_out, od_w_in, od_w_g2, od_b_g, od_head_norm, od_w_out):
    B, S, D = x.shape
    T = B * S
    depth = norms.shape[0]
    h = x.reshape(T, D)

    def gain(layer, idx):
        return norms[layer, idx][None, :]

    ffn_gu, ffn_down = ffn_w_gu.astype(BF16), ffn_w_down.astype(BF16)
    w_q, w_kv, w_o = xa_w_q.astype(BF16), xa_w_kv.astype(BF16), xa_w_o.astype(BF16)
    ev_out, od_out = ev_w_out.astype(BF16), od_w_out.astype(BF16)

    for layer in range(depth):
        h = _ffn(h, gain(layer, N_FFN1_PRE), gain(layer, N_FFN1_POST), ffn_gu, ffn_down, (layer, 0))

        if layer % 2 == 0:
            e = layer // 2
            lru_params = (ev_conv_w[e], ev_conv_b[e][None, :],
                          _block_diag(ev_w_ra[e]).astype(BF16), ev_b_ra[e].reshape(1, B_WIDTH),
                          _block_diag(ev_w_ri[e]).astype(BF16), ev_b_ri[e].reshape(1, B_WIDTH),
                          ev_lam[e][None, :])
            qt, k, vt, qit, ki2, wit, b_out = _even_proj(
                h, gain(layer, N_MIX_PRE), _even_w_in(ev_w_in[e]), ev_kv_norm[e][None, :],
                ev_w_uk[e].astype(BF16), ev_w_uv[e].astype(BF16), lru_params, S)
            a_out = _dsa(qt, k, vt, qit, ki2, wit, B)
            parts = [a_out.reshape(B, S, A_WIDTH), b_out.reshape(B, S, B_WIDTH)]
            w_out, w_out_lead = ev_out, (e,)
        else:
            o = layer // 2
            w_g2 = jnp.concatenate(
                [od_w_g2[o], jnp.zeros((LANES - GLA_GATE_RANK, GLA_DK), od_w_g2.dtype)], axis=0).astype(BF16)
            q, k, v, r, gk = _odd_proj(h, gain(layer, N_MIX_PRE), _odd_w_in(od_w_in[o]), w_g2,
                                       od_b_g[o][None, :])
            r3 = lambda a: a.reshape(B, S, a.shape[-1])
            g_out = _gla(r3(q), r3(k), r3(v), r3(gk), r3(r), od_head_norm[o][None, :])
            parts = [g_out]
            w_out, w_out_lead = od_out, (o,)

        kx, vx = _xa_kv(mem, gain(layer, N_MEM_NORM), w_kv, (layer,))
        h = _mix_out_xa(h.reshape(B, S, D), gain(layer, N_MIX_POST), gain(layer, N_XA_PRE),
                        gain(layer, N_XA_POST), parts, w_out, w_out_lead,
                        w_q, kx, vx, w_o, (layer,)).reshape(T, D)

        h = _ffn(h, gain(layer, N_FFN2_PRE), gain(layer, N_FFN2_POST), ffn_gu, ffn_down, (layer, 1))
    return h.reshape(B, S, D)
```

```python
import functools

import jax
import jax.numpy as jnp
from jax import lax
from jax.experimental import pallas as pl
from jax.experimental.pallas import tpu as pltpu

F32 = jnp.float32
BF16 = jnp.bfloat16

EPS = 1e-6
D_MODEL = 1024
D_FF = 2816
XA_HEADS = 4
XA_HEAD_DIM = D_MODEL // XA_HEADS
A_HEADS = 8
A_HEAD_DIM = 64
A_WIDTH = A_HEADS * A_HEAD_DIM
KV_RANK = 256
IDX_HEADS = 8
IDX_DIM = 64
TOPK_MAX = 256
B_WIDTH = D_MODEL - A_WIDTH
B_BLOCKS = 8
B_BLOCK_DIM = B_WIDTH // B_BLOCKS
CONV_W = 4
LRU_C = 8.0
GLA_HEADS = 4
GLA_DK = D_MODEL // 2
GLA_DV = D_MODEL
GLA_DKH = GLA_DK // GLA_HEADS
GLA_DVH = GLA_DV // GLA_HEADS
GLA_GATE_RANK = 16
GLA_TAU = 16.0
GLA_CHUNK = 64
(N_FFN1_PRE, N_FFN1_POST, N_MIX_PRE, N_MIX_POST, N_XA_PRE, N_XA_POST, N_MEM_NORM,
 N_FFN2_PRE, N_FFN2_POST) = range(9)

LANES = 128
SUBLANES = 8
VMEM_LIMIT = 48 * 1024 * 1024

NEG_BIG = -1e30
LOG2E = 1.4426950408889634
INT_MIN = -2 ** 31
POS_INF_CODE = 0x7F800000
NEG_INF_CODE = -0x7F800001


def _cparams(sem):
    return pltpu.CompilerParams(dimension_semantics=sem, vmem_limit_bytes=VMEM_LIMIT)


def _rms(x, g):
    return x * lax.rsqrt(jnp.mean(x * x, axis=-1, keepdims=True) + EPS) * g


def _dot(a, b):
    return jnp.dot(a, b, preferred_element_type=F32)


def _dot_nt(a, b):
    return lax.dot_general(a, b, (((1,), (1,)), ((), ())), preferred_element_type=F32)


def _dot_tn(a, b):
    return lax.dot_general(a, b, (((0,), (0,)), ((), ())), preferred_element_type=F32)


def _sigmoid(x):
    return 1.0 / (1.0 + jnp.exp(-x))


FFN_TM = 1024
FFN_SLAB = 512
FFN_TF = 256


def _ffn_kernel(*refs, cast_next):
    if cast_next:
        (h_ref, gpre_ref, gpost_ref, wgu_ref, wd_ref, ngu_ref, nd_ref,
         o_ref, ngu_out_ref, nd_out_ref, act_ref) = refs
        ngu_out_ref[...] = ngu_ref[...].astype(BF16)
        nd_out_ref[...] = nd_ref[...].astype(BF16)
    else:
        h_ref, gpre_ref, gpost_ref, wgu_ref, wd_ref, o_ref, act_ref = refs
    F = wd_ref.shape[0]
    for r0 in range(0, h_ref.shape[0], FFN_SLAB):
        rows = slice(r0, r0 + FFN_SLAB)
        x = h_ref[rows, :]
        xn = _rms(x, gpre_ref[...]).astype(BF16)
        for c in range(F // FFN_TF):
            g = _dot(xn, wgu_ref[:, c * FFN_TF:(c + 1) * FFN_TF])
            u = _dot(xn, wgu_ref[:, F + c * FFN_TF:F + (c + 1) * FFN_TF])
            act_ref[rows, c * FFN_TF:(c + 1) * FFN_TF] = (g * _sigmoid(g) * u).astype(BF16)
        f = _dot(act_ref[rows, :], wd_ref[...])
        o_ref[rows, :] = x + 0.5 * _rms(f, gpost_ref[...])


def _resident(arr, lead=()):
    tail = arr.shape[len(lead):]
    index = tuple(lead) + (0,) * len(tail)
    return pl.BlockSpec((None,) * len(lead) + tail, lambda *_: index, pipeline_mode=pl.Buffered(1))


def _ffn(h, g_pre, g_post, w_gu, w_down, nxt=None):
    T, D = h.shape
    F = w_down.shape[0]
    tm = min(FFN_TM, T)
    steps = T // tm
    in_specs = [
        pl.BlockSpec((tm, D), lambda i: (i, 0)),
        pl.BlockSpec((1, D), lambda i: (0, 0)),
        pl.BlockSpec((1, D), lambda i: (0, 0)),
        _resident(w_gu),
        _resident(w_down),
    ]
    out_specs = [pl.BlockSpec((tm, D), lambda i: (i, 0))]
    out_shape = [jax.ShapeDtypeStruct((T, D), F32)]
    args = [h, g_pre, g_post, w_gu, w_down]
    if nxt is not None:
        gu_all, down_all, lead = nxt
        for w_all in (gu_all, down_all):
            rows, cols = w_all.shape[-2:]
            slab = rows // steps
            assert slab * steps == rows and slab % 16 == 0, (rows, steps)
            in_specs.append(pl.BlockSpec((None,) * len(lead) + (slab, cols),
                                         lambda i, lead=tuple(lead): lead + (i, 0)))
            out_specs.append(pl.BlockSpec((slab, cols), lambda i: (i, 0)))
            out_shape.append(jax.ShapeDtypeStruct((rows, cols), BF16))
            args.append(w_all)
    outs = pl.pallas_call(
        functools.partial(_ffn_kernel, cast_next=nxt is not None),
        grid=(steps,),
        in_specs=in_specs,
        out_specs=out_specs,
        out_shape=out_shape,
        scratch_shapes=[pltpu.VMEM((tm, F), BF16)],
        compiler_params=_cparams(("parallel",)),
        name="ffn",
    )(*args)
    return outs if nxt is not None else outs[0]


PROJ_TM = 1024


EV_Q = (0, 512)
EV_CKV = (512, 768)
EV_QI = (768, 1280)
EV_KI2 = (1280, 1408)
EV_GATE = (1408, 1920)
EV_XB = (1920, 2432)
EV_WI = (2432, 2560)
EV_COLS = 2560


DSA_TQ = 512
DSA_CK = 512


def _even_proj_kernel(h_ref, g_ref, w_ref, kvn_ref, wuk_ref, wuv_ref,
                      cw_ref, cb_ref, wra_ref, bra_ref, wri_ref, bri_ref, lam_ref,
                      qt_ref, k_ref, vt_ref, qit_ref, ki_ref, wit_ref, bout_ref,
                      xs_ref, a_ref, b_ref, hc_ref, gg_ref, *, tiles_per_seq):
    _rglru_reset(pl.program_id(0) % tiles_per_seq == 0, xs_ref, hc_ref)
    xn = _rms(h_ref[...], g_ref[...]).astype(BF16)

    def seg(ab):
        return _dot(xn, w_ref[:, ab[0]:ab[1]])

    _rglru_gates(seg(EV_XB), cw_ref, cb_ref, wra_ref, bra_ref, wri_ref, bri_ref, lam_ref,
                 xs_ref, a_ref, b_ref)
    gg_ref[...] = _gelu_tanh(seg(EV_GATE))
    qt_ref[...] = (seg(EV_Q) * (A_HEAD_DIM ** -0.5 * LOG2E)).T.astype(BF16)
    ckv = _rms(seg(EV_CKV), kvn_ref[...]).astype(BF16)
    k_ref[...] = _dot(ckv, wuk_ref[...]).astype(BF16)
    v = _dot(ckv, wuv_ref[...]).astype(BF16)
    for c in range(vt_ref.shape[0]):
        vt_ref[c] = v[c * DSA_CK:(c + 1) * DSA_CK, :].T
    qit_ref[...] = (seg(EV_QI) * (IDX_DIM ** -0.5)).T.astype(BF16)
    ki_ref[...] = seg(EV_KI2).astype(BF16)
    wit_ref[...] = (seg(EV_WI) * (IDX_HEADS ** -0.5)).T[:IDX_HEADS, :]
    _rglru_scan(a_ref, b_ref, hc_ref)
    bout_ref[...] = (a_ref[...] * gg_ref[...]).astype(bout_ref.dtype)


def _even_proj(h, g_pre, w_in, kv_norm, w_uk, w_uv, lru_params, seq_len):
    T, D = h.shape
    tm = min(PROJ_TM, seq_len)
    C = B_WIDTH
    row = lambda i: (i, 0)
    col = lambda i: (0, i)
    fixed = lambda i: (0, 0)
    out_specs = [
        pl.BlockSpec((A_WIDTH, tm), col),
        pl.BlockSpec((tm, A_WIDTH), row),
        pl.BlockSpec((tm // DSA_CK, A_WIDTH, DSA_CK), lambda i: (i, 0, 0)),
        pl.BlockSpec((IDX_HEADS * IDX_DIM, tm), col),
        pl.BlockSpec((tm, 2 * IDX_DIM), row),
        pl.BlockSpec((IDX_HEADS, tm), col),
        pl.BlockSpec((tm, C), row),
    ]
    out_shape = [
        jax.ShapeDtypeStruct((A_WIDTH, T), BF16),
        jax.ShapeDtypeStruct((T, A_WIDTH), BF16),
        jax.ShapeDtypeStruct((T // DSA_CK, A_WIDTH, DSA_CK), BF16),
        jax.ShapeDtypeStruct((IDX_HEADS * IDX_DIM, T), BF16),
        jax.ShapeDtypeStruct((T, 2 * IDX_DIM), BF16),
        jax.ShapeDtypeStruct((IDX_HEADS, T), F32),
        jax.ShapeDtypeStruct((T, C), BF16),
    ]
    return pl.pallas_call(
        functools.partial(_even_proj_kernel, tiles_per_seq=seq_len // tm),
        grid=(T // tm,),
        in_specs=[
            pl.BlockSpec((tm, D), row),
            pl.BlockSpec((1, D), fixed),
            _resident(w_in),
            pl.BlockSpec((1, KV_RANK), fixed),
            pl.BlockSpec(w_uk.shape, fixed),
            pl.BlockSpec(w_uv.shape, fixed),
            *[pl.BlockSpec(p.shape, fixed) for p in lru_params],
        ],
        out_specs=out_specs,
        out_shape=out_shape,
        scratch_shapes=[
            pltpu.VMEM((HALO + tm, C), F32),
            pltpu.VMEM((tm, C), F32),
            pltpu.VMEM((tm, C), F32),
            pltpu.VMEM((SUBLANES, C), F32),
            pltpu.VMEM((tm, C), F32),
        ],
        compiler_params=_cparams(("arbitrary",)),
        name="even_proj",
    )(h, g_pre, w_in, kv_norm, w_uk, w_uv, *lru_params)


ACC_ROWS = 4 * SUBLANES
ACC16_ROWS = 2 * ACC_ROWS
SEARCH_PASSES = 18


def _col_partial(x, op):
    rows, n = x.shape
    part = x.reshape(rows // ACC_ROWS, ACC_ROWS, n)
    return jnp.max(part, axis=0) if op == "max" else jnp.sum(part, axis=0)


def _col_reduce(x, op):
    part = _col_partial(x, op)
    return (jnp.max(part, axis=0, keepdims=True) if op == "max"
            else jnp.sum(part, axis=0, keepdims=True))


def _dsa_kernel(qt_ref, qit_ref, wit_ref, k_ref, vt_ref, ki_ref, tri_ref, o_ref,
                sc_ref, hb_ref, bias_ref, s_ref, qm_ref, qim_ref, m_ref, l_ref, acc_ref, need_ref, seen_ref,
                *, top_k):
    TQ, CK = DSA_TQ, DSA_CK
    j = pl.program_id(1)
    q0 = j * TQ
    nkc = (q0 + TQ + CK - 1) // CK

    low_half = lax.broadcasted_iota(jnp.int32, (LANES, TQ), 0) < A_HEAD_DIM
    for h in range(A_HEADS):
        pr = slice((h // 2) * LANES, (h // 2 + 1) * LANES)
        keep = low_half if h % 2 == 0 else jnp.logical_not(low_half)
        qm_ref[h] = jnp.where(keep, qt_ref[pr, :], jnp.zeros((), BF16))
        qim_ref[h] = jnp.where(keep, qit_ref[pr, :], jnp.zeros((), BF16))

    key_iota = lax.broadcasted_iota(jnp.int32, (CK, TQ), 0)
    q_pos = q0 + lax.broadcasted_iota(jnp.int32, (CK, TQ), 1)

    def score_chunk(c, carry):
        base = pl.multiple_of(c * CK, CK)
        kic = ki_ref[pl.ds(base, CK), :]
        acc = jnp.zeros((CK, TQ), F32)
        for h in range(IDX_HEADS):
            acc = acc + jnp.maximum(_dot(kic, qim_ref[h]), 0.0) * wit_ref[h:h + 1, :]
        sc = jnp.where(base + key_iota <= q_pos, acc, -jnp.inf)
        sc_ref[c] = sc
        hb_ref[c] = sc.astype(BF16)
        return carry

    lax.fori_loop(0, nkc, score_chunk, 0)

    def count(pred_fn):
        def body(c, cnt):
            for g in range(CK // ACC_ROWS):
                hit = pred_fn(sc_ref[c, g * ACC_ROWS:(g + 1) * ACC_ROWS, :])
                cnt = jnp.where(hit, cnt + 1, cnt)
            return cnt
        cnt = lax.fori_loop(0, nkc, body, jnp.zeros((ACC_ROWS, TQ), jnp.int32))
        return jnp.sum(cnt, axis=0, keepdims=True)

    def code_to_float(code):
        code = jnp.clip(code, NEG_INF_CODE, POS_INF_CODE)
        return lax.bitcast_convert_type(code ^ ((code >> 31) & jnp.int32(0x7FFFFFFF)), F32)

    def count16(cand_b):
        def body(c, cnt):
            for g in range(CK // ACC16_ROWS):
                blk = hb_ref[c, g * ACC16_ROWS:(g + 1) * ACC16_ROWS, :]
                cnt = jnp.where(blk >= cand_b, cnt + jnp.ones((), BF16), cnt)
            return cnt
        cnt = lax.fori_loop(0, nkc, body, jnp.zeros((ACC16_ROWS, TQ), BF16))
        return jnp.sum(cnt.astype(F32), axis=0, keepdims=True)

    def bit_pass16(i, prefix):
        cand = prefix + lax.shift_left(jnp.int32(1), 31 - i)
        grid = jnp.where(cand < 0, cand | jnp.int32(0xFFFF), cand)
        cand_b = jnp.broadcast_to(code_to_float(grid), (ACC16_ROWS, TQ)).astype(BF16)
        return jnp.where(count16(cand_b) >= top_k, cand, prefix)

    prefix = lax.fori_loop(0, 16, bit_pass16, jnp.full((1, TQ), INT_MIN, jnp.int32))

    def search_pass(i, carry):
        lo, hi, n_at = carry
        mid = lo + ((hi - lo) >> 1)
        mid_f = code_to_float(mid)
        n_ge = count(lambda sc: sc >= mid_f)
        take = n_ge >= top_k
        return jnp.where(take, mid, lo), jnp.where(take, hi, mid), jnp.where(take, n_ge, n_at)

    lo0 = jnp.maximum(prefix, INT_MIN + 0x10000) - 0x10000
    hi0 = jnp.minimum(prefix, 0x7FFD0000) + 0x20000
    code, _, n_at = lax.fori_loop(
        0, SEARCH_PASSES, search_pass, (lo0, hi0, jnp.full((1, TQ), nkc * CK, jnp.int32)))
    thr = code_to_float(code)
    finite = thr > -jnp.inf
    has_ties = jnp.max(jnp.where(finite & (n_at > top_k), 1, 0)) > 0
    thr_sel = jnp.where(finite, thr, jnp.finfo(F32).min)

    m_ref[...] = jnp.full(m_ref.shape, -jnp.inf, F32)
    l_ref[...] = jnp.zeros(l_ref.shape, F32)
    acc_ref[...] = jnp.zeros(acc_ref.shape, F32)
    need_ref[...] = jnp.zeros(need_ref.shape, F32)
    seen_ref[...] = jnp.zeros(seen_ref.shape, F32)

    @pl.when(has_ties)
    def _():
        n_gt = count(lambda sc: sc > thr)
        need_ref[...] = jnp.where(finite, (top_k - n_gt).astype(F32), 0.0)

    def attend_chunk(c, carry):
        base = pl.multiple_of(c * CK, CK)

        @pl.when(jnp.logical_not(has_ties))
        def _():
            bias_ref[...] = jnp.where(sc_ref[c] >= thr_sel, 0.0, NEG_BIG)

        @pl.when(has_ties)
        def _():
            sc = sc_ref[c]
            eq = sc == thr
            rank = seen_ref[...] + _dot(tri_ref[...], eq.astype(BF16))
            sel = (sc > thr) | (eq & (rank <= need_ref[...]))
            bias_ref[...] = jnp.where(sel, 0.0, NEG_BIG)
            seen_ref[...] += _col_reduce(eq.astype(F32), "sum")

        cmax = []
        for h in range(A_HEADS):
            kp = k_ref[pl.ds(base, CK), (h // 2) * LANES:(h // 2 + 1) * LANES]
            s = _dot(kp, qm_ref[h]) + bias_ref[...]
            s_ref[h] = s
            cmax.append(_col_reduce(s, "max"))
        for h in range(A_HEADS):
            m_old = m_ref[h]
            m_new = jnp.maximum(m_old, cmax[h])
            alpha = jnp.exp2(m_old - m_new)
            p = jnp.exp2(s_ref[h] - m_new)
            l_ref[h] = alpha * l_ref[h] + _col_reduce(p, "sum")
            m_ref[h] = m_new
            vth = vt_ref[c, h * A_HEAD_DIM:(h + 1) * A_HEAD_DIM, :]
            acc_ref[h] = acc_ref[h] * alpha + _dot(vth, p.astype(BF16))
        return carry

    lax.fori_loop(0, nkc, attend_chunk, 0)

    out_t = jnp.concatenate([acc_ref[h] / l_ref[h] for h in range(A_HEADS)], axis=0)
    o_ref[...] = out_t.T.astype(o_ref.dtype)


def _dsa(qt, k, vt, qit, ki2, wit, B):
    T = k.shape[0]
    S = T // B
    TQ, CK = DSA_TQ, DSA_CK
    nc, nq = S // CK, S // TQ
    top_k = min(TOPK_MAX, S // 4)
    tri = (jnp.arange(CK)[:, None] >= jnp.arange(CK)[None, :]).astype(BF16)
    qcol = lambda b, j: (0, b * nq + j)
    return pl.pallas_call(
        functools.partial(_dsa_kernel, top_k=top_k),
        grid=(B, nq),
        in_specs=[
            pl.BlockSpec((A_WIDTH, TQ), qcol),
            pl.BlockSpec((IDX_HEADS * IDX_DIM, TQ), qcol),
            pl.BlockSpec((IDX_HEADS, TQ), qcol),
            pl.BlockSpec((S, A_WIDTH), lambda b, j: (b, 0)),
            pl.BlockSpec((nc, A_WIDTH, CK), lambda b, j: (b, 0, 0)),
            pl.BlockSpec((S, 2 * IDX_DIM), lambda b, j: (b, 0)),
            pl.BlockSpec((CK, CK), lambda b, j: (0, 0)),
        ],
        out_specs=pl.BlockSpec((TQ, A_WIDTH), lambda b, j: (b * nq + j, 0)),
        out_shape=jax.ShapeDtypeStruct((T, A_WIDTH), BF16),
        scratch_shapes=[
            pltpu.VMEM((nc, CK, TQ), F32),
            pltpu.VMEM((nc, CK, TQ), BF16),
            pltpu.VMEM((CK, TQ), F32),
            pltpu.VMEM((A_HEADS, CK, TQ), F32),
            pltpu.VMEM((A_HEADS, LANES, TQ), BF16),
            pltpu.VMEM((IDX_HEADS, LANES, TQ), BF16),
            pltpu.VMEM((A_HEADS, 1, TQ), F32),
            pltpu.VMEM((A_HEADS, 1, TQ), F32),
            pltpu.VMEM((A_HEADS, A_HEAD_DIM, TQ), F32),
            pltpu.VMEM((1, TQ), F32),
            pltpu.VMEM((1, TQ), F32),
        ],
        compiler_params=_cparams(("parallel", "arbitrary")),
        name="dsa",
    )(qt, qit, wit, k, vt, ki2, tri)


HALO = SUBLANES


def _softplus(x):
    return jnp.maximum(x, 0.0) + jnp.log1p(jnp.exp(-jnp.abs(x)))


def _gelu_tanh(x):
    return 0.5 * x * (1.0 + jnp.tanh(0.7978845608028654 * (x + 0.044715 * (x * x * x))))


def _rglru_reset(first, xs_ref, hc_ref):
    @pl.when(first)
    def _():
        xs_ref[0:HALO, :] = jnp.zeros((HALO, xs_ref.shape[1]), F32)
        hc_ref[...] = jnp.zeros(hc_ref.shape, F32)


def _rglru_gates(x, cw_ref, cb_ref, wra_ref, bra_ref, wri_ref, bri_ref, lam_ref, xs_ref, a_ref, b_ref):
    ts, C = x.shape
    xs_ref[HALO:HALO + ts, :] = x
    xc = cb_ref[...] + jnp.zeros((ts, C), F32)
    for kk in range(CONV_W):
        off = HALO - (CONV_W - 1) + kk
        xc = xc + cw_ref[kk:kk + 1, :] * xs_ref[off:off + ts, :]
    xs_ref[0:HALO, :] = x[ts - HALO:ts, :]

    xcb = xc.astype(BF16)
    r = _sigmoid(_dot(xcb, wra_ref[...]) + bra_ref[...])
    gi = _sigmoid(_dot(xcb, wri_ref[...]) + bri_ref[...])
    log_a = (-LRU_C) * r * _softplus(-lam_ref[...])
    a = jnp.exp(log_a)
    a_ref[...] = a
    b_ref[...] = jnp.sqrt(-jnp.tanh(log_a) * (1.0 + a * a)) * (gi * xc)


def _rglru_scan(a_ref, b_ref, hc_ref):
    ts, C = a_ref.shape
    row = lax.broadcasted_iota(jnp.int32, (SUBLANES, C), 0)

    def group(g, carry):
        r0 = pl.multiple_of(g * SUBLANES, SUBLANES)
        av = a_ref[pl.ds(r0, SUBLANES), :]
        bv = b_ref[pl.ds(r0, SUBLANES), :]
        for sh in (1, 2, 4):
            a_sh = pltpu.roll(av, sh, axis=0)
            b_sh = pltpu.roll(bv, sh, axis=0)
            ok = row >= sh
            bv = jnp.where(ok, av * b_sh + bv, bv)
            av = jnp.where(ok, av * a_sh, av)
        h8 = av * carry + bv
        a_ref[pl.ds(r0, SUBLANES), :] = h8
        return jnp.broadcast_to(h8[SUBLANES - 1:SUBLANES, :], (SUBLANES, C))

    hc_ref[...] = lax.fori_loop(0, ts // SUBLANES, group, hc_ref[...])


OD_QKVR = 3072
OD_GLR = (3072, 3200)
OD_COLS = 3200


def _odd_proj_kernel(h_ref, g_ref, w_ref, wg2_ref, bg_ref, q_ref, k_ref, v_ref, r_ref, gk_ref):
    xn = _rms(h_ref[...], g_ref[...]).astype(BF16)

    def seg(a, b):
        return _dot(xn, w_ref[:, a:b])

    q_ref[...] = seg(0, GLA_DK) * (GLA_DKH ** -0.5)
    k_ref[...] = seg(GLA_DK, 2 * GLA_DK)
    v_ref[...] = seg(2 * GLA_DK, 2 * GLA_DK + GLA_DV).astype(BF16)
    r_ref[...] = seg(2 * GLA_DK + GLA_DV, OD_QKVR)
    glr = seg(*OD_GLR).astype(BF16)
    z = _dot(glr, wg2_ref[...]) + bg_ref[...]
    gk_ref[...] = (-_softplus(-z)) * (1.0 / GLA_TAU)


def _odd_proj(h, g_pre, w_in, w_g2, b_g):
    T, D = h.shape
    tm = min(PROJ_TM, T)
    row = lambda i: (i, 0)
    fixed = lambda i: (0, 0)
    outs = [(GLA_DK, F32), (GLA_DK, F32), (GLA_DV, BF16), (GLA_DV, F32), (GLA_DK, F32)]
    return pl.pallas_call(
        _odd_proj_kernel,
        grid=(T // tm,),
        in_specs=[
            pl.BlockSpec((tm, D), row),
            pl.BlockSpec((1, D), fixed),
            _resident(w_in),
            pl.BlockSpec(w_g2.shape, fixed),
            pl.BlockSpec((1, GLA_DK), fixed),
        ],
        out_specs=[pl.BlockSpec((tm, n), row) for n, _ in outs],
        out_shape=[jax.ShapeDtypeStruct((T, n), dt) for n, dt in outs],
        compiler_params=_cparams(("parallel",)),
        name="odd_proj",
    )(h, g_pre, w_in, w_g2, b_g)


GLA_TS = 256


def _gla_kernel(q_ref, k_ref, v_ref, gk_ref, r_ref, hn_ref, tri_ref, o_ref,
                st_ref, qd_ref, oi_ref, u_ref, stb_ref):
    ts = q_ref.shape[1]
    C = GLA_CHUNK
    nch = ts // C

    @pl.when(pl.program_id(1) == 0)
    def _():
        st_ref[...] = jnp.zeros(st_ref.shape, F32)

    gk = gk_ref[0]
    g_hi = gk.astype(BF16)
    rem = gk - g_hi.astype(F32)
    g_mid = rem.astype(BF16)
    g_lo = (rem - g_mid.astype(F32)).astype(BF16)
    tri = tri_ref[...]
    G = _dot(tri, g_hi) + _dot(tri, g_mid) + _dot(tri, g_lo)

    kf = k_ref[0]
    qd_ref[...] = (q_ref[0] * jnp.exp(G)).astype(BF16)
    k_inv = (kf * jnp.exp(-G)).astype(BF16)
    g_last = [G[(c + 1) * C - 1:(c + 1) * C, :] for c in range(nch)]
    k_rem = jnp.concatenate(
        [kf[c * C:(c + 1) * C, :] * jnp.exp(g_last[c] - G[c * C:(c + 1) * C, :]) for c in range(nch)],
        axis=0).astype(BF16)

    ri = lax.broadcasted_iota(jnp.int32, (C, C), 0)
    ci = lax.broadcasted_iota(jnp.int32, (C, C), 1)
    tril = ri >= ci

    for c in range(nch):
        rows = slice(c * C, (c + 1) * C)
        for h in range(GLA_HEADS):
            ksl = slice(h * GLA_DKH, (h + 1) * GLA_DKH)
            vsl = slice(h * GLA_DVH, (h + 1) * GLA_DVH)
            vh = v_ref[0, rows, vsl]
            att = jnp.where(tril, _dot_nt(qd_ref[rows, ksl], k_inv[rows, ksl]), 0.0).astype(BF16)
            oi_ref[rows, vsl] = _dot(att, vh)
            u_ref[c, h] = _dot_tn(vh, k_rem[rows, ksl])

    for h in range(GLA_HEADS):
        ksl = slice(h * GLA_DKH, (h + 1) * GLA_DKH)
        st = st_ref[h]
        for c in range(nch):
            stb_ref[c, h] = st.astype(BF16)
            st = st * jnp.exp(g_last[c][:, ksl]) + u_ref[c, h]
        st_ref[h] = st

    for c in range(nch):
        rows = slice(c * C, (c + 1) * C)
        for h in range(GLA_HEADS):
            ksl = slice(h * GLA_DKH, (h + 1) * GLA_DKH)
            vsl = slice(h * GLA_DVH, (h + 1) * GLA_DVH)
            o = oi_ref[rows, vsl] + _dot_nt(qd_ref[rows, ksl], stb_ref[c, h])
            on = _rms(o, hn_ref[...])
            rr = r_ref[0, rows, vsl]
            o_ref[0, rows, vsl] = (on * (rr * _sigmoid(rr))).astype(o_ref.dtype)


def _gla(q, k, v, gk, r, head_norm):
    B, S, _ = q.shape
    ts = min(GLA_TS, S)
    nch = ts // GLA_CHUNK
    pos = jnp.arange(ts)
    tri = ((pos[:, None] >= pos[None, :])
           & (pos[:, None] // GLA_CHUNK == pos[None, :] // GLA_CHUNK)).astype(BF16)
    blk = lambda b, s: (b, s, 0)
    return pl.pallas_call(
        _gla_kernel,
        grid=(B, S // ts),
        in_specs=[
            pl.BlockSpec((1, ts, GLA_DK), blk),
            pl.BlockSpec((1, ts, GLA_DK), blk),
            pl.BlockSpec((1, ts, GLA_DV), blk),
            pl.BlockSpec((1, ts, GLA_DK), blk),
            pl.BlockSpec((1, ts, GLA_DV), blk),
            pl.BlockSpec((1, GLA_DVH), lambda b, s: (0, 0)),
            pl.BlockSpec((ts, ts), lambda b, s: (0, 0)),
        ],
        out_specs=pl.BlockSpec((1, ts, GLA_DV), blk),
        out_shape=jax.ShapeDtypeStruct((B, S, GLA_DV), BF16),
        scratch_shapes=[
            pltpu.VMEM((GLA_HEADS, GLA_DVH, GLA_DKH), F32),
            pltpu.VMEM((ts, GLA_DK), BF16),
            pltpu.VMEM((ts, GLA_DV), F32),
            pltpu.VMEM((nch, GLA_HEADS, GLA_DVH, GLA_DKH), F32),
            pltpu.VMEM((nch, GLA_HEADS, GLA_DVH, GLA_DKH), BF16),
        ],
        compiler_params=_cparams(("parallel", "arbitrary")),
        name="gla",
    )(q, k, v, gk, r, head_norm, tri)


def _xa_kv_kernel(mem_ref, g_ref, w_ref, k_ref, v_ref):
    mn = _rms(mem_ref[0], g_ref[...]).astype(BF16)
    k_ref[0] = (_dot(mn, w_ref[:, :D_MODEL]) * (XA_HEAD_DIM ** -0.5)).astype(BF16)
    v_ref[0] = _dot(mn, w_ref[:, D_MODEL:]).astype(BF16)


def _xa_kv(mem, g_mem, w_kv, lead):
    B, M, D = mem.shape
    blk = lambda b: (b, 0, 0)
    return pl.pallas_call(
        _xa_kv_kernel,
        grid=(B,),
        in_specs=[pl.BlockSpec((1, M, D), blk), pl.BlockSpec((1, D), lambda b: (0, 0)),
                  _resident(w_kv, lead)],
        out_specs=[pl.BlockSpec((1, M, D), blk), pl.BlockSpec((1, M, D), blk)],
        out_shape=[jax.ShapeDtypeStruct((B, M, D), BF16)] * 2,
        compiler_params=_cparams(("parallel",)),
        name="xa_kv",
    )(mem, g_mem, w_kv)


XA_TM = 1024


def _mix_out_xa_kernel(*refs, offsets):
    n = len(offsets)
    h_ref, gmix_ref, gpre_ref, gpost_ref = refs[:4]
    part_refs = refs[4:4 + n]
    wout_ref, wq_ref, k_ref, v_ref, wo_ref, o_ref = refs[4 + n:]
    m = None
    for p_ref, off in zip(part_refs, offsets):
        kk = p_ref.shape[-1]
        term = _dot(p_ref[0], wout_ref[off:off + kk, :])
        m = term if m is None else m + term
    x = h_ref[0] + _rms(m, gmix_ref[...])
    xn = _rms(x, gpre_ref[...]).astype(BF16)
    q = _dot(xn, wq_ref[...]).astype(BF16)
    heads = []
    for h in range(XA_HEADS):
        sl = slice(h * XA_HEAD_DIM, (h + 1) * XA_HEAD_DIM)
        s = _dot_nt(q[:, sl], k_ref[0, :, sl])
        p = jnp.exp(s - jnp.max(s, axis=-1, keepdims=True))
        oh = _dot(p.astype(BF16), v_ref[0, :, sl]) / jnp.sum(p, axis=-1, keepdims=True)
        heads.append(oh.astype(BF16))
    c = _dot(jnp.concatenate(heads, axis=-1), wo_ref[...])
    o_ref[0] = x + _rms(c, gpost_ref[...])


def _mix_out_xa(h, g_mix, g_pre, g_post, parts, w_out, out_lead, w_q, kx, vx, w_o, lead):
    B, S, D = h.shape
    M = kx.shape[1]
    tm = min(XA_TM, S)
    blk = lambda b, i: (b, i, 0)
    fixed = lambda b, i: (0, 0)
    offsets, off = [], 0
    for p in parts:
        offsets.append(off)
        off += p.shape[-1]
    return pl.pallas_call(
        functools.partial(_mix_out_xa_kernel, offsets=tuple(offsets)),
        grid=(B, S // tm),
        in_specs=[
            pl.BlockSpec((1, tm, D), blk),
            pl.BlockSpec((1, D), fixed),
            pl.BlockSpec((1, D), fixed),
            pl.BlockSpec((1, D), fixed),
            *[pl.BlockSpec((1, tm, p.shape[-1]), blk) for p in parts],
            _resident(w_out, out_lead),
            _resident(w_q, lead),
            pl.BlockSpec((1, M, D), lambda b, i: (b, 0, 0)),
            pl.BlockSpec((1, M, D), lambda b, i: (b, 0, 0)),
            _resident(w_o, lead),
        ],
        out_specs=pl.BlockSpec((1, tm, D), blk),
        out_shape=jax.ShapeDtypeStruct((B, S, D), F32),
        compiler_params=_cparams(("parallel", "parallel")),
        name="mix_out_xa",
    )(h, g_mix, g_pre, g_post, *parts, w_out, w_q, kx, vx, w_o)


def _block_diag(w):
    G, n, _ = w.shape
    eye = jnp.eye(G, dtype=w.dtype)
    return (eye[:, None, :, None] * w[:, :, None, :]).reshape(G * n, G * n)


def _even_w_in(w):
    ki = w[:, 1280:1344]
    pad = jnp.zeros((w.shape[0], LANES - IDX_HEADS), w.dtype)
    return jnp.concatenate([w[:, :1280], ki, ki, w[:, 1352:2376], w[:, 1344:1352], pad], axis=1).astype(BF16)


def _odd_w_in(w):
    pad = jnp.zeros((w.shape[0], LANES - GLA_GATE_RANK), w.dtype)
    return jnp.concatenate([w[:, :2048], w[:, 2064:3088], w[:, 2048:2064], pad], axis=1).astype(BF16)


def kernel(x, mem, norms, ffn_w_gu, ffn_w_down, xa_w_q, xa_w_kv, xa_w_o, ev_w_in, ev_kv_norm, ev_w_uk, ev_w_uv, ev_conv_w, ev_conv_b, ev_w_ra, ev_b_ra, ev_w_ri, ev_b_ri, ev_lam, ev_w_out, od_w_in, od_w_g2, od_b_g, od_head_norm, od_w_out):
    B, S, D = x.shape
    T = B * S
    depth = norms.shape[0]
    h = x.reshape(T, D)

    def gain(layer, idx):
        return norms[layer, idx][None, :]

    w_q, w_kv, w_o = xa_w_q.astype(BF16), xa_w_kv.astype(BF16), xa_w_o.astype(BF16)
    ev_out, od_out = ev_w_out.astype(BF16), od_w_out.astype(BF16)
    ffn_order = [(layer, j) for layer in range(depth) for j in range(2)]
    ffn_w = (ffn_w_gu[0, 0].astype(BF16), ffn_w_down[0, 0].astype(BF16))

    def ffn(h, ffn_w, layer, j, g_pre, g_post):
        k = ffn_order.index((layer, j))
        if k + 1 == len(ffn_order):
            return _ffn(h, g_pre, g_post, *ffn_w), None
        h, gu_next, down_next = _ffn(h, g_pre, g_post, *ffn_w,
                                     nxt=(ffn_w_gu, ffn_w_down, ffn_order[k + 1]))
        return h, (gu_next, down_next)

    for layer in range(depth):
        h, ffn_w = ffn(h, ffn_w, layer, 0, gain(layer, N_FFN1_PRE), gain(layer, N_FFN1_POST))

        if layer % 2 == 0:
            e = layer // 2
            lru_params = (ev_conv_w[e], ev_conv_b[e][None, :],
                          _block_diag(ev_w_ra[e]).astype(BF16), ev_b_ra[e].reshape(1, B_WIDTH),
                          _block_diag(ev_w_ri[e]).astype(BF16), ev_b_ri[e].reshape(1, B_WIDTH),
                          ev_lam[e][None, :])
            qt, k, vt, qit, ki2, wit, b_out = _even_proj(
                h, gain(layer, N_MIX_PRE), _even_w_in(ev_w_in[e]), ev_kv_norm[e][None, :],
                ev_w_uk[e].astype(BF16), ev_w_uv[e].astype(BF16), lru_params, S)
            a_out = _dsa(qt, k, vt, qit, ki2, wit, B)
            parts = [a_out.reshape(B, S, A_WIDTH), b_out.reshape(B, S, B_WIDTH)]
            w_out, w_out_lead = ev_out, (e,)
        else:
            o = layer // 2
            w_g2 = jnp.concatenate(
                [od_w_g2[o], jnp.zeros((LANES - GLA_GATE_RANK, GLA_DK), od_w_g2.dtype)], axis=0).astype(BF16)
            q, k, v, r, gk = _odd_proj(h, gain(layer, N_MIX_PRE), _odd_w_in(od_w_in[o]), w_g2,
                                       od_b_g[o][None, :])
            r3 = lambda a: a.reshape(B, S, a.shape[-1])
            g_out = _gla(r3(q), r3(k), r3(v), r3(gk), r3(r), od_head_norm[o][None, :])
            parts = [g_out]
            w_out, w_out_lead = od_out, (o,)

        kx, vx = _xa_kv(mem, gain(layer, N_MEM_NORM), w_kv, (layer,))
        h = _mix_out_xa(h.reshape(B, S, D), gain(layer, N_MIX_POST), gain(layer, N_XA_PRE),
                        gain(layer, N_XA_POST), parts, w_out, w_out_lead,
                        w_q, kx, vx, w_o, (layer,)).reshape(T, D)

        h, ffn_w = ffn(h, ffn_w, layer, 1, gain(layer, N_FFN2_PRE), gain(layer, N_FFN2_POST))
    return h.reshape(B, S, D)
```

```python
import functools

import jax
import jax.numpy as jnp
from jax import lax
from jax.experimental import pallas as pl
from jax.experimental.pallas import tpu as pltpu

F32 = jnp.float32
BF16 = jnp.bfloat16

EPS = 1e-6
D_MODEL = 1024
D_FF = 2816
XA_HEADS = 4
XA_HEAD_DIM = D_MODEL // XA_HEADS
A_HEADS = 8
A_HEAD_DIM = 64
A_WIDTH = A_HEADS * A_HEAD_DIM
KV_RANK = 256
IDX_HEADS = 8
IDX_DIM = 64
TOPK_MAX = 256
B_WIDTH = D_MODEL - A_WIDTH
B_BLOCKS = 8
B_BLOCK_DIM = B_WIDTH // B_BLOCKS
CONV_W = 4
LRU_C = 8.0
GLA_HEADS = 4
GLA_DK = D_MODEL // 2
GLA_DV = D_MODEL
GLA_DKH = GLA_DK // GLA_HEADS
GLA_DVH = GLA_DV // GLA_HEADS
GLA_GATE_RANK = 16
GLA_TAU = 16.0
GLA_CHUNK = 64
(N_FFN1_PRE, N_FFN1_POST, N_MIX_PRE, N_MIX_POST, N_XA_PRE, N_XA_POST, N_MEM_NORM,
 N_FFN2_PRE, N_FFN2_POST) = range(9)

LANES = 128
SUBLANES = 8
VMEM_LIMIT = 48 * 1024 * 1024

NEG_BIG = -1e30
LOG2E = 1.4426950408889634
INT_MIN = -2 ** 31
POS_INF_CODE = 0x7F800000
NEG_INF_CODE = -0x7F800001


def _cparams(sem):
    return pltpu.CompilerParams(dimension_semantics=sem, vmem_limit_bytes=VMEM_LIMIT)


def _rms(x, g):
    return x * lax.rsqrt(jnp.mean(x * x, axis=-1, keepdims=True) + EPS) * g


def _dot(a, b):
    return jnp.dot(a, b, preferred_element_type=F32)


def _dot_nt(a, b):
    return lax.dot_general(a, b, (((1,), (1,)), ((), ())), preferred_element_type=F32)


def _dot_tn(a, b):
    return lax.dot_general(a, b, (((0,), (0,)), ((), ())), preferred_element_type=F32)


def _sigmoid(x):
    return 1.0 / (1.0 + jnp.exp(-x))


FFN_TM = 1024
FFN_SLAB = 512
FFN_TF = 256


def _ffn_kernel(*refs, cast_next):
    if cast_next:
        (h_ref, gpre_ref, gpost_ref, wgu_ref, wd_ref, ngu_ref, nd_ref,
         o_ref, ngu_out_ref, nd_out_ref, act_ref) = refs
        ngu_out_ref[...] = ngu_ref[...].astype(BF16)
        nd_out_ref[...] = nd_ref[...].astype(BF16)
    else:
        h_ref, gpre_ref, gpost_ref, wgu_ref, wd_ref, o_ref, act_ref = refs
    F = wd_ref.shape[0]
    for r0 in range(0, h_ref.shape[0], FFN_SLAB):
        rows = slice(r0, r0 + FFN_SLAB)
        x = h_ref[rows, :]
        xn = _rms(x, gpre_ref[...]).astype(BF16)
        for c in range(F // FFN_TF):
            g = _dot(xn, wgu_ref[:, c * FFN_TF:(c + 1) * FFN_TF])
            u = _dot(xn, wgu_ref[:, F + c * FFN_TF:F + (c + 1) * FFN_TF])
            act_ref[rows, c * FFN_TF:(c + 1) * FFN_TF] = (g * _sigmoid(g) * u).astype(BF16)
        f = _dot(act_ref[rows, :], wd_ref[...])
        o_ref[rows, :] = x + 0.5 * _rms(f, gpost_ref[...])


def _resident(arr, lead=()):
    tail = arr.shape[len(lead):]
    index = tuple(lead) + (0,) * len(tail)
    return pl.BlockSpec((None,) * len(lead) + tail, lambda *_: index, pipeline_mode=pl.Buffered(1))


def _ffn(h, g_pre, g_post, w_gu, w_down, nxt=None):
    T, D = h.shape
    F = w_down.shape[0]
    tm = min(FFN_TM, T)
    steps = T // tm
    in_specs = [
        pl.BlockSpec((tm, D), lambda i: (i, 0)),
        pl.BlockSpec((1, D), lambda i: (0, 0)),
        pl.BlockSpec((1, D), lambda i: (0, 0)),
        _resident(w_gu),
        _resident(w_down),
    ]
    out_specs = [pl.BlockSpec((tm, D), lambda i: (i, 0))]
    out_shape = [jax.ShapeDtypeStruct((T, D), F32)]
    args = [h, g_pre, g_post, w_gu, w_down]
    if nxt is not None:
        gu_all, down_all, lead = nxt
        for w_all in (gu_all, down_all):
            rows, cols = w_all.shape[-2:]
            slab = rows // steps
            assert slab * steps == rows and slab % 16 == 0, (rows, steps)
            in_specs.append(pl.BlockSpec((None,) * len(lead) + (slab, cols),
                                         lambda i, lead=tuple(lead): lead + (i, 0)))
            out_specs.append(pl.BlockSpec((slab, cols), lambda i: (i, 0)))
            out_shape.append(jax.ShapeDtypeStruct((rows, cols), BF16))
            args.append(w_all)
    outs = pl.pallas_call(
        functools.partial(_ffn_kernel, cast_next=nxt is not None),
        grid=(steps,),
        in_specs=in_specs,
        out_specs=out_specs,
        out_shape=out_shape,
        scratch_shapes=[pltpu.VMEM((tm, F), BF16)],
        compiler_params=_cparams(("parallel",)),
        name="ffn",
    )(*args)
    return outs if nxt is not None else outs[0]


PROJ_TM = 1024


EV_Q = (0, 512)
EV_CKV = (512, 768)
EV_QI = (768, 1280)
EV_KI2 = (1280, 1408)
EV_GATE = (1408, 1920)
EV_XB = (1920, 2432)
EV_WI = (2432, 2560)
EV_COLS = 2560


DSA_TQ = 512
DSA_CK = 512


def _even_proj_kernel(h_ref, g_ref, w_ref, kvn_ref, wuk_ref, wuv_ref,
                      cw_ref, cb_ref, wra_ref, bra_ref, wri_ref, bri_ref, lam_ref,
                      qt_ref, k_ref, vt_ref, qit_ref, ki_ref, wit_ref, bout_ref,
                      xs_ref, a_ref, b_ref, hc_ref, gg_ref, *, tiles_per_seq):
    _rglru_reset(pl.program_id(0) % tiles_per_seq == 0, xs_ref, hc_ref)
    xn = _rms(h_ref[...], g_ref[...]).astype(BF16)

    def seg(ab):
        return _dot(xn, w_ref[:, ab[0]:ab[1]])

    _rglru_gates(seg(EV_XB), cw_ref, cb_ref, wra_ref, bra_ref, wri_ref, bri_ref, lam_ref,
                 xs_ref, a_ref, b_ref)
    gg_ref[...] = _gelu_tanh(seg(EV_GATE))
    qt_ref[...] = (seg(EV_Q) * (A_HEAD_DIM ** -0.5 * LOG2E)).T.astype(BF16)
    ckv = _rms(seg(EV_CKV), kvn_ref[...]).astype(BF16)
    k_ref[...] = _dot(ckv, wuk_ref[...]).astype(BF16)
    v = _dot(ckv, wuv_ref[...]).astype(BF16)
    for c in range(vt_ref.shape[0]):
        vt_ref[c] = v[c * DSA_CK:(c + 1) * DSA_CK, :].T
    qit_ref[...] = (seg(EV_QI) * (IDX_DIM ** -0.5)).T.astype(BF16)
    ki_ref[...] = seg(EV_KI2).astype(BF16)
    wit_ref[...] = (seg(EV_WI) * (IDX_HEADS ** -0.5)).T[:IDX_HEADS, :]
    _rglru_scan(a_ref, b_ref, hc_ref)
    bout_ref[...] = (a_ref[...] * gg_ref[...]).astype(bout_ref.dtype)


def _even_proj(h, g_pre, w_in, kv_norm, w_uk, w_uv, lru_params, seq_len):
    T, D = h.shape
    tm = min(PROJ_TM, seq_len)
    C = B_WIDTH
    row = lambda i: (i, 0)
    col = lambda i: (0, i)
    fixed = lambda i: (0, 0)
    out_specs = [
        pl.BlockSpec((A_WIDTH, tm), col),
        pl.BlockSpec((tm, A_WIDTH), row),
        pl.BlockSpec((tm // DSA_CK, A_WIDTH, DSA_CK), lambda i: (i, 0, 0)),
        pl.BlockSpec((IDX_HEADS * IDX_DIM, tm), col),
        pl.BlockSpec((tm, 2 * IDX_DIM), row),
        pl.BlockSpec((IDX_HEADS, tm), col),
        pl.BlockSpec((tm, C), row),
    ]
    out_shape = [
        jax.ShapeDtypeStruct((A_WIDTH, T), BF16),
        jax.ShapeDtypeStruct((T, A_WIDTH), BF16),
        jax.ShapeDtypeStruct((T // DSA_CK, A_WIDTH, DSA_CK), BF16),
        jax.ShapeDtypeStruct((IDX_HEADS * IDX_DIM, T), BF16),
        jax.ShapeDtypeStruct((T, 2 * IDX_DIM), BF16),
        jax.ShapeDtypeStruct((IDX_HEADS, T), F32),
        jax.ShapeDtypeStruct((T, C), BF16),
    ]
    return pl.pallas_call(
        functools.partial(_even_proj_kernel, tiles_per_seq=seq_len // tm),
        grid=(T // tm,),
        in_specs=[
            pl.BlockSpec((tm, D), row),
            pl.BlockSpec((1, D), fixed),
            _resident(w_in),
            pl.BlockSpec((1, KV_RANK), fixed),
            pl.BlockSpec(w_uk.shape, fixed),
            pl.BlockSpec(w_uv.shape, fixed),
            *[pl.BlockSpec(p.shape, fixed) for p in lru_params],
        ],
        out_specs=out_specs,
        out_shape=out_shape,
        scratch_shapes=[
            pltpu.VMEM((HALO + tm, C), F32),
            pltpu.VMEM((tm, C), F32),
            pltpu.VMEM((tm, C), F32),
            pltpu.VMEM((SUBLANES, C), F32),
            pltpu.VMEM((tm, C), F32),
        ],
        compiler_params=_cparams(("arbitrary",)),
        name="even_proj",
    )(h, g_pre, w_in, kv_norm, w_uk, w_uv, *lru_params)


ACC_ROWS = 4 * SUBLANES
ACC16_ROWS = 2 * ACC_ROWS
SEARCH_PASSES = 18


def _col_partial(x, op):
    rows, n = x.shape
    part = x.reshape(rows // ACC_ROWS, ACC_ROWS, n)
    return jnp.max(part, axis=0) if op == "max" else jnp.sum(part, axis=0)


def _col_reduce(x, op):
    part = _col_partial(x, op)
    return (jnp.max(part, axis=0, keepdims=True) if op == "max"
            else jnp.sum(part, axis=0, keepdims=True))


def _dsa_kernel(qt_ref, qit_ref, wit_ref, k_ref, vt_ref, ki_ref, tri_ref, o_ref,
                sc_ref, hb_ref, bias_ref, s_ref, qm_ref, qim_ref, m_ref, l_ref, acc_ref, need_ref, seen_ref,
                *, top_k):
    TQ, CK = DSA_TQ, DSA_CK
    j = pl.program_id(1)
    q0 = j * TQ
    nkc = (q0 + TQ + CK - 1) // CK

    low_half = lax.broadcasted_iota(jnp.int32, (LANES, TQ), 0) < A_HEAD_DIM
    for h in range(A_HEADS):
        pr = slice((h // 2) * LANES, (h // 2 + 1) * LANES)
        keep = low_half if h % 2 == 0 else jnp.logical_not(low_half)
        qm_ref[h] = jnp.where(keep, qt_ref[pr, :], jnp.zeros((), BF16))
        qim_ref[h] = jnp.where(keep, qit_ref[pr, :], jnp.zeros((), BF16))

    key_iota = lax.broadcasted_iota(jnp.int32, (CK, TQ), 0)
    q_pos = q0 + lax.broadcasted_iota(jnp.int32, (CK, TQ), 1)

    def score_chunk(c, carry):
        base = pl.multiple_of(c * CK, CK)
        kic = ki_ref[pl.ds(base, CK), :]
        acc = jnp.zeros((CK, TQ), F32)
        for h in range(IDX_HEADS):
            acc = acc + jnp.maximum(_dot(kic, qim_ref[h]), 0.0) * wit_ref[h:h + 1, :]
        sc = jnp.where(base + key_iota <= q_pos, acc, -jnp.inf)
        sc_ref[c] = sc
        hb_ref[c] = sc.astype(BF16)
        return carry

    lax.fori_loop(0, nkc, score_chunk, 0)

    def count(pred_fn):
        def body(c, cnt):
            for g in range(CK // ACC_ROWS):
                hit = pred_fn(sc_ref[c, g * ACC_ROWS:(g + 1) * ACC_ROWS, :])
                cnt = jnp.where(hit, cnt + 1, cnt)
            return cnt
        cnt = lax.fori_loop(0, nkc, body, jnp.zeros((ACC_ROWS, TQ), jnp.int32))
        return jnp.sum(cnt, axis=0, keepdims=True)

    def code_to_float(code):
        code = jnp.clip(code, NEG_INF_CODE, POS_INF_CODE)
        return lax.bitcast_convert_type(code ^ ((code >> 31) & jnp.int32(0x7FFFFFFF)), F32)

    def count16(cand_b):
        def body(c, cnt):
            for g in range(CK // ACC16_ROWS):
                blk = hb_ref[c, g * ACC16_ROWS:(g + 1) * ACC16_ROWS, :]
                cnt = jnp.where(blk >= cand_b, cnt + jnp.ones((), BF16), cnt)
            return cnt
        cnt = lax.fori_loop(0, nkc, body, jnp.zeros((ACC16_ROWS, TQ), BF16))
        return jnp.sum(cnt.astype(F32), axis=0, keepdims=True)

    def bit_pass16(i, prefix):
        cand = prefix + lax.shift_left(jnp.int32(1), 31 - i)
        grid = jnp.where(cand < 0, cand | jnp.int32(0xFFFF), cand)
        cand_b = jnp.broadcast_to(code_to_float(grid), (ACC16_ROWS, TQ)).astype(BF16)
        return jnp.where(count16(cand_b) >= top_k, cand, prefix)

    prefix = lax.fori_loop(0, 16, bit_pass16, jnp.full((1, TQ), INT_MIN, jnp.int32))

    def search_pass(i, carry):
        lo, hi, n_at = carry
        mid = lo + ((hi - lo) >> 1)
        mid_f = code_to_float(mid)
        n_ge = count(lambda sc: sc >= mid_f)
        take = n_ge >= top_k
        return jnp.where(take, mid, lo), jnp.where(take, hi, mid), jnp.where(take, n_ge, n_at)

    lo0 = jnp.maximum(prefix, INT_MIN + 0x10000) - 0x10000
    hi0 = jnp.minimum(prefix, 0x7FFD0000) + 0x20000
    code, _, n_at = lax.fori_loop(
        0, SEARCH_PASSES, search_pass, (lo0, hi0, jnp.full((1, TQ), nkc * CK, jnp.int32)))
    thr = code_to_float(code)
    finite = thr > -jnp.inf
    has_ties = jnp.max(jnp.where(finite & (n_at > top_k), 1, 0)) > 0
    thr_sel = jnp.where(finite, thr, jnp.finfo(F32).min)

    m_ref[...] = jnp.full(m_ref.shape, -jnp.inf, F32)
    l_ref[...] = jnp.zeros(l_ref.shape, F32)
    acc_ref[...] = jnp.zeros(acc_ref.shape, F32)
    need_ref[...] = jnp.zeros(need_ref.shape, F32)
    seen_ref[...] = jnp.zeros(seen_ref.shape, F32)

    @pl.when(has_ties)
    def _():
        n_gt = count(lambda sc: sc > thr)
        need_ref[...] = jnp.where(finite, (top_k - n_gt).astype(F32), 0.0)

    def attend_chunk(c, carry):
        base = pl.multiple_of(c * CK, CK)

        @pl.when(jnp.logical_not(has_ties))
        def _():
            bias_ref[...] = jnp.where(sc_ref[c] >= thr_sel, 0.0, NEG_BIG)

        @pl.when(has_ties)
        def _():
            sc = sc_ref[c]
            eq = sc == thr
            rank = seen_ref[...] + _dot(tri_ref[...], eq.astype(BF16))
            sel = (sc > thr) | (eq & (rank <= need_ref[...]))
            bias_ref[...] = jnp.where(sel, 0.0, NEG_BIG)
            seen_ref[...] += _col_reduce(eq.astype(F32), "sum")

        cmax = []
        for h in range(A_HEADS):
            kp = k_ref[pl.ds(base, CK), (h // 2) * LANES:(h // 2 + 1) * LANES]
            s = _dot(kp, qm_ref[h]) + bias_ref[...]
            s_ref[h] = s
            cmax.append(_col_reduce(s, "max"))
        for h in range(A_HEADS):
            m_old = m_ref[h]
            m_new = jnp.maximum(m_old, cmax[h])
            alpha = jnp.exp2(m_old - m_new)
            p = jnp.exp2(s_ref[h] - m_new)
            l_ref[h] = alpha * l_ref[h] + _col_reduce(p, "sum")
            m_ref[h] = m_new
            vth = vt_ref[c, h * A_HEAD_DIM:(h + 1) * A_HEAD_DIM, :]
            acc_ref[h] = acc_ref[h] * alpha + _dot(vth, p.astype(BF16))
        return carry

    lax.fori_loop(0, nkc, attend_chunk, 0)

    out_t = jnp.concatenate([acc_ref[h] / l_ref[h] for h in range(A_HEADS)], axis=0)
    o_ref[...] = out_t.T.astype(o_ref.dtype)


def _dsa(qt, k, vt, qit, ki2, wit, B):
    T = k.shape[0]
    S = T // B
    TQ, CK = DSA_TQ, DSA_CK
    nc, nq = S // CK, S // TQ
    top_k = min(TOPK_MAX, S // 4)
    tri = (jnp.arange(CK)[:, None] >= jnp.arange(CK)[None, :]).astype(BF16)
    qcol = lambda b, j: (0, b * nq + j)
    return pl.pallas_call(
        functools.partial(_dsa_kernel, top_k=top_k),
        grid=(B, nq),
        in_specs=[
            pl.BlockSpec((A_WIDTH, TQ), qcol),
            pl.BlockSpec((IDX_HEADS * IDX_DIM, TQ), qcol),
            pl.BlockSpec((IDX_HEADS, TQ), qcol),
            pl.BlockSpec((S, A_WIDTH), lambda b, j: (b, 0)),
            pl.BlockSpec((nc, A_WIDTH, CK), lambda b, j: (b, 0, 0)),
            pl.BlockSpec((S, 2 * IDX_DIM), lambda b, j: (b, 0)),
            pl.BlockSpec((CK, CK), lambda b, j: (0, 0)),
        ],
        out_specs=pl.BlockSpec((TQ, A_WIDTH), lambda b, j: (b * nq + j, 0)),
        out_shape=jax.ShapeDtypeStruct((T, A_WIDTH), BF16),
        scratch_shapes=[
            pltpu.VMEM((nc, CK, TQ), F32),
            pltpu.VMEM((nc, CK, TQ), BF16),
            pltpu.VMEM((CK, TQ), F32),
            pltpu.VMEM((A_HEADS, CK, TQ), F32),
            pltpu.VMEM((A_HEADS, LANES, TQ), BF16),
            pltpu.VMEM((IDX_HEADS, LANES, TQ), BF16),
            pltpu.VMEM((A_HEADS, 1, TQ), F32),
            pltpu.VMEM((A_HEADS, 1, TQ), F32),
            pltpu.VMEM((A_HEADS, A_HEAD_DIM, TQ), F32),
            pltpu.VMEM((1, TQ), F32),
            pltpu.VMEM((1, TQ), F32),
        ],
        compiler_params=_cparams(("parallel", "arbitrary")),
        name="dsa",
    )(qt, qit, wit, k, vt, ki2, tri)


HALO = SUBLANES


def _softplus(x):
    return jnp.maximum(x, 0.0) + jnp.log1p(jnp.exp(-jnp.abs(x)))


def _gelu_tanh(x):
    return 0.5 * x * (1.0 + jnp.tanh(0.7978845608028654 * (x + 0.044715 * (x * x * x))))


def _rglru_reset(first, xs_ref, hc_ref):
    @pl.when(first)
    def _():
        xs_ref[0:HALO, :] = jnp.zeros((HALO, xs_ref.shape[1]), F32)
        hc_ref[...] = jnp.zeros(hc_ref.shape, F32)


def _rglru_gates(x, cw_ref, cb_ref, wra_ref, bra_ref, wri_ref, bri_ref, lam_ref, xs_ref, a_ref, b_ref):
    ts, C = x.shape
    xs_ref[HALO:HALO + ts, :] = x
    xc = cb_ref[...] + jnp.zeros((ts, C), F32)
    for kk in range(CONV_W):
        off = HALO - (CONV_W - 1) + kk
        xc = xc + cw_ref[kk:kk + 1, :] * xs_ref[off:off + ts, :]
    xs_ref[0:HALO, :] = x[ts - HALO:ts, :]

    xcb = xc.astype(BF16)
    r = _sigmoid(_dot(xcb, wra_ref[...]) + bra_ref[...])
    gi = _sigmoid(_dot(xcb, wri_ref[...]) + bri_ref[...])
    log_a = (-LRU_C) * r * _softplus(-lam_ref[...])
    a = jnp.exp(log_a)
    a_ref[...] = a
    b_ref[...] = jnp.sqrt(-jnp.tanh(log_a) * (1.0 + a * a)) * (gi * xc)


def _rglru_scan(a_ref, b_ref, hc_ref):
    ts, C = a_ref.shape
    row = lax.broadcasted_iota(jnp.int32, (SUBLANES, C), 0)

    def group(g, carry):
        r0 = pl.multiple_of(g * SUBLANES, SUBLANES)
        av = a_ref[pl.ds(r0, SUBLANES), :]
        bv = b_ref[pl.ds(r0, SUBLANES), :]
        for sh in (1, 2, 4):
            a_sh = pltpu.roll(av, sh, axis=0)
            b_sh = pltpu.roll(bv, sh, axis=0)
            ok = row >= sh
            bv = jnp.where(ok, av * b_sh + bv, bv)
            av = jnp.where(ok, av * a_sh, av)
        h8 = av * carry + bv
        a_ref[pl.ds(r0, SUBLANES), :] = h8
        return jnp.broadcast_to(h8[SUBLANES - 1:SUBLANES, :], (SUBLANES, C))

    hc_ref[...] = lax.fori_loop(0, ts // SUBLANES, group, hc_ref[...])


OD_QKVR = 3072
OD_GLR = (3072, 3200)
OD_COLS = 3200


def _odd_proj_kernel(h_ref, g_ref, w_ref, wg2_ref, bg_ref, q_ref, k_ref, v_ref, r_ref, gk_ref):
    xn = _rms(h_ref[...], g_ref[...]).astype(BF16)

    def seg(a, b):
        return _dot(xn, w_ref[:, a:b])

    q_ref[...] = seg(0, GLA_DK) * (GLA_DKH ** -0.5)
    k_ref[...] = seg(GLA_DK, 2 * GLA_DK)
    v_ref[...] = seg(2 * GLA_DK, 2 * GLA_DK + GLA_DV).astype(BF16)
    r_ref[...] = seg(2 * GLA_DK + GLA_DV, OD_QKVR)
    glr = seg(*OD_GLR).astype(BF16)
    z = _dot(glr, wg2_ref[...]) + bg_ref[...]
    gk_ref[...] = (-_softplus(-z)) * (1.0 / GLA_TAU)


def _odd_proj(h, g_pre, w_in, w_g2, b_g):
    T, D = h.shape
    tm = min(PROJ_TM, T)
    row = lambda i: (i, 0)
    fixed = lambda i: (0, 0)
    outs = [(GLA_DK, F32), (GLA_DK, F32), (GLA_DV, BF16), (GLA_DV, F32), (GLA_DK, F32)]
    return pl.pallas_call(
        _odd_proj_kernel,
        grid=(T // tm,),
        in_specs=[
            pl.BlockSpec((tm, D), row),
            pl.BlockSpec((1, D), fixed),
            _resident(w_in),
            pl.BlockSpec(w_g2.shape, fixed),
            pl.BlockSpec((1, GLA_DK), fixed),
        ],
        out_specs=[pl.BlockSpec((tm, n), row) for n, _ in outs],
        out_shape=[jax.ShapeDtypeStruct((T, n), dt) for n, dt in outs],
        compiler_params=_cparams(("parallel",)),
        name="odd_proj",
    )(h, g_pre, w_in, w_g2, b_g)


GLA_TS = 256


def _gla_kernel(q_ref, k_ref, v_ref, gk_ref, r_ref, hn_ref, tri_ref, o_ref,
                st_ref, qd_ref, oi_ref, u_ref, stb_ref):
    ts = q_ref.shape[1]
    C = GLA_CHUNK
    nch = ts // C

    @pl.when(pl.program_id(1) == 0)
    def _():
        st_ref[...] = jnp.zeros(st_ref.shape, F32)

    gk = gk_ref[0]
    g_hi = gk.astype(BF16)
    rem = gk - g_hi.astype(F32)
    g_mid = rem.astype(BF16)
    g_lo = (rem - g_mid.astype(F32)).astype(BF16)
    tri = tri_ref[...]
    G = _dot(tri, g_hi) + _dot(tri, g_mid) + _dot(tri, g_lo)

    kf = k_ref[0]
    qd_ref[...] = (q_ref[0] * jnp.exp(G)).astype(BF16)
    k_inv = (kf * jnp.exp(-G)).astype(BF16)
    g_last = [G[(c + 1) * C - 1:(c + 1) * C, :] for c in range(nch)]
    k_rem = jnp.concatenate(
        [kf[c * C:(c + 1) * C, :] * jnp.exp(g_last[c] - G[c * C:(c + 1) * C, :]) for c in range(nch)],
        axis=0).astype(BF16)

    ri = lax.broadcasted_iota(jnp.int32, (ts, ts), 0)
    ci = lax.broadcasted_iota(jnp.int32, (ts, ts), 1)
    same_chunk_causal = (ri >= ci) & (ri // C == ci // C)
    for h in range(GLA_HEADS):
        ksl = slice(h * GLA_DKH, (h + 1) * GLA_DKH)
        vsl = slice(h * GLA_DVH, (h + 1) * GLA_DVH)
        att = jnp.where(same_chunk_causal, _dot_nt(qd_ref[:, ksl], k_inv[:, ksl]), 0.0).astype(BF16)
        oi_ref[:, vsl] = _dot(att, v_ref[0, :, vsl])

    for c in range(nch):
        rows = slice(c * C, (c + 1) * C)
        for h in range(GLA_HEADS):
            ksl = slice(h * GLA_DKH, (h + 1) * GLA_DKH)
            vsl = slice(h * GLA_DVH, (h + 1) * GLA_DVH)
            u_ref[c, h] = _dot_tn(v_ref[0, rows, vsl], k_rem[rows, ksl])

    for h in range(GLA_HEADS):
        ksl = slice(h * GLA_DKH, (h + 1) * GLA_DKH)
        st = st_ref[h]
        for c in range(nch):
            stb_ref[c, h] = st.astype(BF16)
            st = st * jnp.exp(g_last[c][:, ksl]) + u_ref[c, h]
        st_ref[h] = st

    for c in range(nch):
        rows = slice(c * C, (c + 1) * C)
        for h in range(GLA_HEADS):
            ksl = slice(h * GLA_DKH, (h + 1) * GLA_DKH)
            vsl = slice(h * GLA_DVH, (h + 1) * GLA_DVH)
            o = oi_ref[rows, vsl] + _dot_nt(qd_ref[rows, ksl], stb_ref[c, h])
            on = _rms(o, hn_ref[...])
            rr = r_ref[0, rows, vsl]
            o_ref[0, rows, vsl] = (on * (rr * _sigmoid(rr))).astype(o_ref.dtype)


def _gla(q, k, v, gk, r, head_norm):
    B, S, _ = q.shape
    ts = min(GLA_TS, S)
    nch = ts // GLA_CHUNK
    pos = jnp.arange(ts)
    tri = ((pos[:, None] >= pos[None, :])
           & (pos[:, None] // GLA_CHUNK == pos[None, :] // GLA_CHUNK)).astype(BF16)
    blk = lambda b, s: (b, s, 0)
    return pl.pallas_call(
        _gla_kernel,
        grid=(B, S // ts),
        in_specs=[
            pl.BlockSpec((1, ts, GLA_DK), blk),
            pl.BlockSpec((1, ts, GLA_DK), blk),
            pl.BlockSpec((1, ts, GLA_DV), blk),
            pl.BlockSpec((1, ts, GLA_DK), blk),
            pl.BlockSpec((1, ts, GLA_DV), blk),
            pl.BlockSpec((1, GLA_DVH), lambda b, s: (0, 0)),
            pl.BlockSpec((ts, ts), lambda b, s: (0, 0)),
        ],
        out_specs=pl.BlockSpec((1, ts, GLA_DV), blk),
        out_shape=jax.ShapeDtypeStruct((B, S, GLA_DV), BF16),
        scratch_shapes=[
            pltpu.VMEM((GLA_HEADS, GLA_DVH, GLA_DKH), F32),
            pltpu.VMEM((ts, GLA_DK), BF16),
            pltpu.VMEM((ts, GLA_DV), F32),
            pltpu.VMEM((nch, GLA_HEADS, GLA_DVH, GLA_DKH), F32),
            pltpu.VMEM((nch, GLA_HEADS, GLA_DVH, GLA_DKH), BF16),
        ],
        compiler_params=_cparams(("parallel", "arbitrary")),
        name="gla",
    )(q, k, v, gk, r, head_norm, tri)


def _xa_kv_kernel(mem_ref, g_ref, w_ref, k_ref, v_ref):
    mn = _rms(mem_ref[...], g_ref[...]).astype(BF16)
    k_ref[...] = (_dot(mn, w_ref[:, :D_MODEL].astype(BF16)) * (XA_HEAD_DIM ** -0.5)).astype(BF16)
    v_ref[...] = _dot(mn, w_ref[:, D_MODEL:].astype(BF16)).astype(BF16)


def _xa_kv(mem, g_mem, w_kv, lead):
    B, M, D = mem.shape
    whole = lambda i: (0, 0)
    k, v = pl.pallas_call(
        _xa_kv_kernel,
        grid=(1,),
        in_specs=[pl.BlockSpec((B * M, D), whole), pl.BlockSpec((1, D), whole), _resident(w_kv, lead)],
        out_specs=[pl.BlockSpec((B * M, D), whole), pl.BlockSpec((B * M, D), whole)],
        out_shape=[jax.ShapeDtypeStruct((B * M, D), BF16)] * 2,
        compiler_params=_cparams(("arbitrary",)),
        name="xa_kv",
    )(mem.reshape(B * M, D), g_mem, w_kv)
    return k.reshape(B, M, D), v.reshape(B, M, D)


XA_TM = 1024


def _mix_out_xa_kernel(*refs, offsets):
    n = len(offsets)
    h_ref, gmix_ref, gpre_ref, gpost_ref = refs[:4]
    part_refs = refs[4:4 + n]
    wout_ref, wq_ref, k_ref, v_ref, wo_ref, o_ref = refs[4 + n:]
    m = None
    for p_ref, off in zip(part_refs, offsets):
        kk = p_ref.shape[-1]
        term = _dot(p_ref[0], wout_ref[off:off + kk, :])
        m = term if m is None else m + term
    x = h_ref[0] + _rms(m, gmix_ref[...])
    xn = _rms(x, gpre_ref[...]).astype(BF16)
    q = _dot(xn, wq_ref[...]).astype(BF16)
    heads = []
    for h in range(XA_HEADS):
        sl = slice(h * XA_HEAD_DIM, (h + 1) * XA_HEAD_DIM)
        s = _dot_nt(q[:, sl], k_ref[0, :, sl])
        p = jnp.exp(s - jnp.max(s, axis=-1, keepdims=True))
        oh = _dot(p.astype(BF16), v_ref[0, :, sl]) / jnp.sum(p, axis=-1, keepdims=True)
        heads.append(oh.astype(BF16))
    c = _dot(jnp.concatenate(heads, axis=-1), wo_ref[...])
    o_ref[0] = x + _rms(c, gpost_ref[...])


def _mix_out_xa(h, g_mix, g_pre, g_post, parts, w_out, out_lead, w_q, kx, vx, w_o, lead):
    B, S, D = h.shape
    M = kx.shape[1]
    tm = min(XA_TM, S)
    blk = lambda b, i: (b, i, 0)
    fixed = lambda b, i: (0, 0)
    offsets, off = [], 0
    for p in parts:
        offsets.append(off)
        off += p.shape[-1]
    return pl.pallas_call(
        functools.partial(_mix_out_xa_kernel, offsets=tuple(offsets)),
        grid=(B, S // tm),
        in_specs=[
            pl.BlockSpec((1, tm, D), blk),
            pl.BlockSpec((1, D), fixed),
            pl.BlockSpec((1, D), fixed),
            pl.BlockSpec((1, D), fixed),
            *[pl.BlockSpec((1, tm, p.shape[-1]), blk) for p in parts],
            _resident(w_out, out_lead),
            _resident(w_q, lead),
            pl.BlockSpec((1, M, D), lambda b, i: (b, 0, 0)),
            pl.BlockSpec((1, M, D), lambda b, i: (b, 0, 0)),
            _resident(w_o, lead),
        ],
        out_specs=pl.BlockSpec((1, tm, D), blk),
        out_shape=jax.ShapeDtypeStruct((B, S, D), F32),
        compiler_params=_cparams(("parallel", "parallel")),
        name="mix_out_xa",
    )(h, g_mix, g_pre, g_post, *parts, w_out, w_q, kx, vx, w_o)


def _block_diag(w):
    G, n, _ = w.shape
    eye = jnp.eye(G, dtype=w.dtype)
    return (eye[:, None, :, None] * w[:, :, None, :]).reshape(G * n, G * n)


def _even_w_in(w):
    ki = w[:, 1280:1344]
    pad = jnp.zeros((w.shape[0], LANES - IDX_HEADS), w.dtype)
    return jnp.concatenate([w[:, :1280], ki, ki, w[:, 1352:2376], w[:, 1344:1352], pad], axis=1).astype(BF16)


def _odd_w_in(w):
    pad = jnp.zeros((w.shape[0], LANES - GLA_GATE_RANK), w.dtype)
    return jnp.concatenate([w[:, :2048], w[:, 2064:3088], w[:, 2048:2064], pad], axis=1).astype(BF16)


def kernel(x, mem, norms, ffn_w_gu, ffn_w_down, xa_w_q, xa_w_kv, xa_w_o, ev_w_in, ev_kv_norm, ev_w_uk, ev_w_uv, ev_conv_w, ev_conv_b, ev_w_ra, ev_b_ra, ev_w_ri, ev_b_ri, ev_lam, ev_w_out, od_w_in, od_w_g2, od_b_g, od_head_norm, od_w_out):
    B, S, D = x.shape
    T = B * S
    depth = norms.shape[0]
    h = x.reshape(T, D)

    def gain(layer, idx):
        return norms[layer, idx][None, :]

    w_q, w_o = xa_w_q.astype(BF16), xa_w_o.astype(BF16)
    ev_out, od_out = ev_w_out.astype(BF16), od_w_out.astype(BF16)
    ffn_order = [(layer, j) for layer in range(depth) for j in range(2)]
    ffn_w = (ffn_w_gu[0, 0].astype(BF16), ffn_w_down[0, 0].astype(BF16))

    def ffn(h, ffn_w, layer, j, g_pre, g_post):
        k = ffn_order.index((layer, j))
        if k + 1 == len(ffn_order):
            return _ffn(h, g_pre, g_post, *ffn_w), None
        h, gu_next, down_next = _ffn(h, g_pre, g_post, *ffn_w,
                                     nxt=(ffn_w_gu, ffn_w_down, ffn_order[k + 1]))
        return h, (gu_next, down_next)

    for layer in range(depth):
        h, ffn_w = ffn(h, ffn_w, layer, 0, gain(layer, N_FFN1_PRE), gain(layer, N_FFN1_POST))

        if layer % 2 == 0:
            e = layer // 2
            lru_params = (ev_conv_w[e], ev_conv_b[e][None, :],
                          _block_diag(ev_w_ra[e]).astype(BF16), ev_b_ra[e].reshape(1, B_WIDTH),
                          _block_diag(ev_w_ri[e]).astype(BF16), ev_b_ri[e].reshape(1, B_WIDTH),
                          ev_lam[e][None, :])
            qt, k, vt, qit, ki2, wit, b_out = _even_proj(
                h, gain(layer, N_MIX_PRE), _even_w_in(ev_w_in[e]), ev_kv_norm[e][None, :],
                ev_w_uk[e].astype(BF16), ev_w_uv[e].astype(BF16), lru_params, S)
            a_out = _dsa(qt, k, vt, qit, ki2, wit, B)
            parts = [a_out.reshape(B, S, A_WIDTH), b_out.reshape(B, S, B_WIDTH)]
            w_out, w_out_lead = ev_out, (e,)
        else:
            o = layer // 2
            w_g2 = jnp.concatenate(
                [od_w_g2[o], jnp.zeros((LANES - GLA_GATE_RANK, GLA_DK), od_w_g2.dtype)], axis=0).astype(BF16)
            q, k, v, r, gk = _odd_proj(h, gain(layer, N_MIX_PRE), _odd_w_in(od_w_in[o]), w_g2,
                                       od_b_g[o][None, :])
            r3 = lambda a: a.reshape(B, S, a.shape[-1])
            g_out = _gla(r3(q), r3(k), r3(v), r3(gk), r3(r), od_head_norm[o][None, :])
            parts = [g_out]
            w_out, w_out_lead = od_out, (o,)

        kx, vx = _xa_kv(mem, gain(layer, N_MEM_NORM), xa_w_kv, (layer,))
        h = _mix_out_xa(h.reshape(B, S, D), gain(layer, N_MIX_POST), gain(layer, N_XA_PRE),
                        gain(layer, N_XA_POST), parts, w_out, w_out_lead,
                        w_q, kx, vx, w_o, (layer,)).reshape(T, D)

        h, ffn_w = ffn(h, ffn_w, layer, 1, gain(layer, N_FFN2_PRE), gain(layer, N_FFN2_POST))
    return h.reshape(B, S, D)
```

```python
import functools

import jax
import jax.numpy as jnp
from jax import lax
from jax.experimental import pallas as pl
from jax.experimental.pallas import tpu as pltpu

F32 = jnp.float32
BF16 = jnp.bfloat16

EPS = 1e-6
D_MODEL = 1024
D_FF = 2816
XA_HEADS = 4
XA_HEAD_DIM = D_MODEL // XA_HEADS
A_HEADS = 8
A_HEAD_DIM = 64
A_WIDTH = A_HEADS * A_HEAD_DIM
KV_RANK = 256
IDX_HEADS = 8
IDX_DIM = 64
TOPK_MAX = 256
B_WIDTH = D_MODEL - A_WIDTH
B_BLOCKS = 8
B_BLOCK_DIM = B_WIDTH // B_BLOCKS
CONV_W = 4
LRU_C = 8.0
GLA_HEADS = 4
GLA_DK = D_MODEL // 2
GLA_DV = D_MODEL
GLA_DKH = GLA_DK // GLA_HEADS
GLA_DVH = GLA_DV // GLA_HEADS
GLA_GATE_RANK = 16
GLA_TAU = 16.0
GLA_CHUNK = 64
(N_FFN1_PRE, N_FFN1_POST, N_MIX_PRE, N_MIX_POST, N_XA_PRE, N_XA_POST, N_MEM_NORM,
 N_FFN2_PRE, N_FFN2_POST) = range(9)

LANES = 128
SUBLANES = 8
VMEM_LIMIT = 48 * 1024 * 1024

NEG_BIG = -1e30
LOG2E = 1.4426950408889634
INT_MIN = -2 ** 31
POS_INF_CODE = 0x7F800000
NEG_INF_CODE = -0x7F800001


def _cparams(sem):
    return pltpu.CompilerParams(dimension_semantics=sem, vmem_limit_bytes=VMEM_LIMIT)


def _rms(x, g):
    return x * lax.rsqrt(jnp.mean(x * x, axis=-1, keepdims=True) + EPS) * g


def _dot(a, b):
    return jnp.dot(a, b, preferred_element_type=F32)


def _dot_nt(a, b):
    return lax.dot_general(a, b, (((1,), (1,)), ((), ())), preferred_element_type=F32)


def _dot_tn(a, b):
    return lax.dot_general(a, b, (((0,), (0,)), ((), ())), preferred_element_type=F32)


def _sigmoid(x):
    return 1.0 / (1.0 + jnp.exp(-x))


FFN_TM = 1024
FFN_SLAB = 512
FFN_TF = 256


def _ffn_kernel(*refs, cast_next):
    if cast_next:
        (h_ref, gpre_ref, gpost_ref, wgu_ref, wd_ref, ngu_ref, nd_ref,
         o_ref, ngu_out_ref, nd_out_ref, act_ref) = refs
        ngu_out_ref[...] = ngu_ref[...].astype(BF16)
        nd_out_ref[...] = nd_ref[...].astype(BF16)
    else:
        h_ref, gpre_ref, gpost_ref, wgu_ref, wd_ref, o_ref, act_ref = refs
    F = wd_ref.shape[0]
    for r0 in range(0, h_ref.shape[0], FFN_SLAB):
        rows = slice(r0, r0 + FFN_SLAB)
        x = h_ref[rows, :]
        xn = _rms(x, gpre_ref[...]).astype(BF16)
        for c in range(F // FFN_TF):
            g = _dot(xn, wgu_ref[:, c * FFN_TF:(c + 1) * FFN_TF])
            u = _dot(xn, wgu_ref[:, F + c * FFN_TF:F + (c + 1) * FFN_TF])
            act_ref[rows, c * FFN_TF:(c + 1) * FFN_TF] = (g * _sigmoid(g) * u).astype(BF16)
        f = _dot(act_ref[rows, :], wd_ref[...])
        o_ref[rows, :] = x + 0.5 * _rms(f, gpost_ref[...])


def _resident(arr, lead=()):
    tail = arr.shape[len(lead):]
    index = tuple(lead) + (0,) * len(tail)
    return pl.BlockSpec((None,) * len(lead) + tail, lambda *_: index, pipeline_mode=pl.Buffered(1))


def _ffn(h, g_pre, g_post, w_gu, w_down, nxt=None):
    T, D = h.shape
    F = w_down.shape[0]
    tm = min(FFN_TM, T)
    steps = T // tm
    in_specs = [
        pl.BlockSpec((tm, D), lambda i: (i, 0)),
        pl.BlockSpec((1, D), lambda i: (0, 0)),
        pl.BlockSpec((1, D), lambda i: (0, 0)),
        _resident(w_gu),
        _resident(w_down),
    ]
    out_specs = [pl.BlockSpec((tm, D), lambda i: (i, 0))]
    out_shape = [jax.ShapeDtypeStruct((T, D), F32)]
    args = [h, g_pre, g_post, w_gu, w_down]
    if nxt is not None:
        gu_all, down_all, lead = nxt
        for w_all in (gu_all, down_all):
            rows, cols = w_all.shape[-2:]
            slab = rows // steps
            assert slab * steps == rows and slab % 16 == 0, (rows, steps)
            in_specs.append(pl.BlockSpec((None,) * len(lead) + (slab, cols),
                                         lambda i, lead=tuple(lead): lead + (i, 0)))
            out_specs.append(pl.BlockSpec((slab, cols), lambda i: (i, 0)))
            out_shape.append(jax.ShapeDtypeStruct((rows, cols), BF16))
            args.append(w_all)
    outs = pl.pallas_call(
        functools.partial(_ffn_kernel, cast_next=nxt is not None),
        grid=(steps,),
        in_specs=in_specs,
        out_specs=out_specs,
        out_shape=out_shape,
        scratch_shapes=[pltpu.VMEM((tm, F), BF16)],
        compiler_params=_cparams(("parallel",)),
        name="ffn",
    )(*args)
    return outs if nxt is not None else outs[0]


PROJ_TM = 1024


EV_Q = (0, 512)
EV_CKV = (512, 768)
EV_QI = (768, 1280)
EV_KI2 = (1280, 1408)
EV_GATE = (1408, 1920)
EV_XB = (1920, 2432)
EV_WI = (2432, 2560)
EV_COLS = 2560


DSA_TQ = 512
DSA_CK = 512


def _even_proj_kernel(h_ref, g_ref, w_ref, kvn_ref, wuk_ref, wuv_ref,
                      cw_ref, cb_ref, wra_ref, bra_ref, wri_ref, bri_ref, lam_ref,
                      qt_ref, k_ref, vt_ref, qit_ref, ki_ref, wit_ref, bout_ref,
                      xs_ref, a_ref, b_ref, hc_ref, gg_ref, *, tiles_per_seq):
    _rglru_reset(pl.program_id(0) % tiles_per_seq == 0, xs_ref, hc_ref)
    xn = _rms(h_ref[...], g_ref[...]).astype(BF16)

    def seg(ab):
        return _dot(xn, w_ref[:, ab[0]:ab[1]])

    tm = h_ref.shape[0]
    lru_slab, lru_finish = _rglru_gates(seg(EV_XB), cw_ref, cb_ref, wra_ref, bra_ref, wri_ref, bri_ref,
                                        lam_ref, xs_ref, a_ref, b_ref)
    n_slab = tm // LRU_SLABS

    def p_gate():
        gg_ref[...] = _gelu_tanh(seg(EV_GATE))

    def p_q():
        qt_ref[...] = (seg(EV_Q) * (A_HEAD_DIM ** -0.5 * LOG2E)).T.astype(BF16)

    ckv = []

    def p_k():
        ckv.append(_rms(seg(EV_CKV), kvn_ref[...]).astype(BF16))
        k_ref[...] = _dot(ckv[0], wuk_ref[...]).astype(BF16)

    def p_v():
        v = _dot(ckv[0], wuv_ref[...]).astype(BF16)
        for c in range(vt_ref.shape[0]):
            vt_ref[c] = v[c * DSA_CK:(c + 1) * DSA_CK, :].T

    def p_qi():
        qit_ref[...] = (seg(EV_QI) * (IDX_DIM ** -0.5)).T.astype(BF16)

    def p_ki():
        ki_ref[...] = seg(EV_KI2).astype(BF16)

    def p_wi():
        wit_ref[...] = (seg(EV_WI) * (IDX_HEADS ** -0.5)).T[:IDX_HEADS, :]

    pieces = (p_gate, p_q, p_k, p_v, p_qi, p_ki, p_wi)
    for s in range(LRU_SLABS):
        lru_slab(s * n_slab, n_slab)
        if s < len(pieces):
            pieces[s]()
    lru_finish()
    _rglru_scan(a_ref, b_ref, hc_ref)
    bout_ref[...] = (a_ref[...] * gg_ref[...]).astype(bout_ref.dtype)


def _even_proj(h, g_pre, w_in, kv_norm, w_uk, w_uv, lru_params, seq_len):
    T, D = h.shape
    tm = min(PROJ_TM, seq_len)
    C = B_WIDTH
    row = lambda i: (i, 0)
    col = lambda i: (0, i)
    fixed = lambda i: (0, 0)
    out_specs = [
        pl.BlockSpec((A_WIDTH, tm), col),
        pl.BlockSpec((tm, A_WIDTH), row),
        pl.BlockSpec((tm // DSA_CK, A_WIDTH, DSA_CK), lambda i: (i, 0, 0)),
        pl.BlockSpec((IDX_HEADS * IDX_DIM, tm), col),
        pl.BlockSpec((tm, 2 * IDX_DIM), row),
        pl.BlockSpec((IDX_HEADS, tm), col),
        pl.BlockSpec((tm, C), row),
    ]
    out_shape = [
        jax.ShapeDtypeStruct((A_WIDTH, T), BF16),
        jax.ShapeDtypeStruct((T, A_WIDTH), BF16),
        jax.ShapeDtypeStruct((T // DSA_CK, A_WIDTH, DSA_CK), BF16),
        jax.ShapeDtypeStruct((IDX_HEADS * IDX_DIM, T), BF16),
        jax.ShapeDtypeStruct((T, 2 * IDX_DIM), BF16),
        jax.ShapeDtypeStruct((IDX_HEADS, T), F32),
        jax.ShapeDtypeStruct((T, C), BF16),
    ]
    return pl.pallas_call(
        functools.partial(_even_proj_kernel, tiles_per_seq=seq_len // tm),
        grid=(T // tm,),
        in_specs=[
            pl.BlockSpec((tm, D), row),
            pl.BlockSpec((1, D), fixed),
            _resident(w_in),
            pl.BlockSpec((1, KV_RANK), fixed),
            pl.BlockSpec(w_uk.shape, fixed),
            pl.BlockSpec(w_uv.shape, fixed),
            *[pl.BlockSpec(p.shape, fixed) for p in lru_params],
        ],
        out_specs=out_specs,
        out_shape=out_shape,
        scratch_shapes=[
            pltpu.VMEM((HALO + tm, C), F32),
            pltpu.VMEM((tm, C), F32),
            pltpu.VMEM((tm, C), F32),
            pltpu.VMEM((SUBLANES, C), F32),
            pltpu.VMEM((tm, C), F32),
        ],
        compiler_params=_cparams(("arbitrary",)),
        name="even_proj",
    )(h, g_pre, w_in, kv_norm, w_uk, w_uv, *lru_params)


ACC_ROWS = 4 * SUBLANES
ACC16_ROWS = 2 * ACC_ROWS
SEARCH_PASSES = 18


def _col_partial(x, op):
    rows, n = x.shape
    part = x.reshape(rows // ACC_ROWS, ACC_ROWS, n)
    return jnp.max(part, axis=0) if op == "max" else jnp.sum(part, axis=0)


def _col_reduce(x, op):
    part = _col_partial(x, op)
    return (jnp.max(part, axis=0, keepdims=True) if op == "max"
            else jnp.sum(part, axis=0, keepdims=True))


def _dsa_kernel(qt_ref, qit_ref, wit_ref, k_ref, vt_ref, ki_ref, tri_ref, o_ref,
                sc_ref, hb_ref, bias_ref, s_ref, qm_ref, qim_ref, m_ref, l_ref, acc_ref, need_ref, seen_ref,
                *, top_k):
    TQ, CK = DSA_TQ, DSA_CK
    j = pl.program_id(1)
    q0 = j * TQ
    nkc = (q0 + TQ + CK - 1) // CK

    low_half = lax.broadcasted_iota(jnp.int32, (LANES, TQ), 0) < A_HEAD_DIM
    for h in range(A_HEADS):
        pr = slice((h // 2) * LANES, (h // 2 + 1) * LANES)
        keep = low_half if h % 2 == 0 else jnp.logical_not(low_half)
        qm_ref[h] = jnp.where(keep, qt_ref[pr, :], jnp.zeros((), BF16))
        qim_ref[h] = jnp.where(keep, qit_ref[pr, :], jnp.zeros((), BF16))

    key_iota = lax.broadcasted_iota(jnp.int32, (CK, TQ), 0)
    q_pos = q0 + lax.broadcasted_iota(jnp.int32, (CK, TQ), 1)

    def score_chunk(c, carry):
        base = pl.multiple_of(c * CK, CK)
        kic = ki_ref[pl.ds(base, CK), :]
        acc = jnp.zeros((CK, TQ), F32)
        for h in range(IDX_HEADS):
            acc = acc + jnp.maximum(_dot(kic, qim_ref[h]), 0.0) * wit_ref[h:h + 1, :]
        sc = jnp.where(base + key_iota <= q_pos, acc, -jnp.inf)
        sc_ref[c] = sc
        hb_ref[c] = sc.astype(BF16)
        return carry

    lax.fori_loop(0, nkc, score_chunk, 0)

    def count(pred_fn):
        def body(c, cnt):
            for g in range(CK // ACC_ROWS):
                hit = pred_fn(sc_ref[c, g * ACC_ROWS:(g + 1) * ACC_ROWS, :])
                cnt = jnp.where(hit, cnt + 1, cnt)
            return cnt
        cnt = lax.fori_loop(0, nkc, body, jnp.zeros((ACC_ROWS, TQ), jnp.int32))
        return jnp.sum(cnt, axis=0, keepdims=True)

    def code_to_float(code):
        code = jnp.clip(code, NEG_INF_CODE, POS_INF_CODE)
        return lax.bitcast_convert_type(code ^ ((code >> 31) & jnp.int32(0x7FFFFFFF)), F32)

    def count16(cand_b):
        def body(c, cnt):
            for g in range(CK // ACC16_ROWS):
                blk = hb_ref[c, g * ACC16_ROWS:(g + 1) * ACC16_ROWS, :]
                cnt = jnp.where(blk >= cand_b, cnt + jnp.ones((), BF16), cnt)
            return cnt
        cnt = lax.fori_loop(0, nkc, body, jnp.zeros((ACC16_ROWS, TQ), BF16))
        return jnp.sum(cnt.astype(F32), axis=0, keepdims=True)

    def bit_pass16(i, prefix):
        cand = prefix + lax.shift_left(jnp.int32(1), 31 - i)
        grid = jnp.where(cand < 0, cand | jnp.int32(0xFFFF), cand)
        cand_b = jnp.broadcast_to(code_to_float(grid), (ACC16_ROWS, TQ)).astype(BF16)
        return jnp.where(count16(cand_b) >= top_k, cand, prefix)

    prefix = lax.fori_loop(0, 16, bit_pass16, jnp.full((1, TQ), INT_MIN, jnp.int32))

    def search_pass(i, carry):
        lo, hi, n_at = carry
        mid = lo + ((hi - lo) >> 1)
        mid_f = code_to_float(mid)
        n_ge = count(lambda sc: sc >= mid_f)
        take = n_ge >= top_k
        return jnp.where(take, mid, lo), jnp.where(take, hi, mid), jnp.where(take, n_ge, n_at)

    lo0 = jnp.maximum(prefix, INT_MIN + 0x10000) - 0x10000
    hi0 = jnp.minimum(prefix, 0x7FFD0000) + 0x20000
    code, _, n_at = lax.fori_loop(
        0, SEARCH_PASSES, search_pass, (lo0, hi0, jnp.full((1, TQ), nkc * CK, jnp.int32)))
    thr = code_to_float(code)
    finite = thr > -jnp.inf
    has_ties = jnp.max(jnp.where(finite & (n_at > top_k), 1, 0)) > 0
    thr_sel = jnp.where(finite, thr, jnp.finfo(F32).min)

    m_ref[...] = jnp.full(m_ref.shape, -jnp.inf, F32)
    l_ref[...] = jnp.zeros(l_ref.shape, F32)
    acc_ref[...] = jnp.zeros(acc_ref.shape, F32)
    need_ref[...] = jnp.zeros(need_ref.shape, F32)
    seen_ref[...] = jnp.zeros(seen_ref.shape, F32)

    @pl.when(has_ties)
    def _():
        n_gt = count(lambda sc: sc > thr)
        need_ref[...] = jnp.where(finite, (top_k - n_gt).astype(F32), 0.0)

    def attend_chunk(c, carry):
        base = pl.multiple_of(c * CK, CK)

        @pl.when(jnp.logical_not(has_ties))
        def _():
            bias_ref[...] = jnp.where(sc_ref[c] >= thr_sel, 0.0, NEG_BIG)

        @pl.when(has_ties)
        def _():
            sc = sc_ref[c]
            eq = sc == thr
            rank = seen_ref[...] + _dot(tri_ref[...], eq.astype(BF16))
            sel = (sc > thr) | (eq & (rank <= need_ref[...]))
            bias_ref[...] = jnp.where(sel, 0.0, NEG_BIG)
            seen_ref[...] += _col_reduce(eq.astype(F32), "sum")

        cmax = []
        for h in range(A_HEADS):
            kp = k_ref[pl.ds(base, CK), (h // 2) * LANES:(h // 2 + 1) * LANES]
            s = _dot(kp, qm_ref[h]) + bias_ref[...]
            s_ref[h] = s
            cmax.append(_col_reduce(s, "max"))
        for h in range(A_HEADS):
            m_old = m_ref[h]
            m_new = jnp.maximum(m_old, cmax[h])
            alpha = jnp.exp2(m_old - m_new)
            p = jnp.exp2(s_ref[h] - m_new)
            l_ref[h] = alpha * l_ref[h] + _col_reduce(p, "sum")
            m_ref[h] = m_new
            vth = vt_ref[c, h * A_HEAD_DIM:(h + 1) * A_HEAD_DIM, :]
            acc_ref[h] = acc_ref[h] * alpha + _dot(vth, p.astype(BF16))
        return carry

    lax.fori_loop(0, nkc, attend_chunk, 0)

    out_t = jnp.concatenate([acc_ref[h] / l_ref[h] for h in range(A_HEADS)], axis=0)
    o_ref[...] = out_t.T.astype(o_ref.dtype)


def _dsa(qt, k, vt, qit, ki2, wit, B):
    T = k.shape[0]
    S = T // B
    TQ, CK = DSA_TQ, DSA_CK
    nc, nq = S // CK, S // TQ
    top_k = min(TOPK_MAX, S // 4)
    tri = (jnp.arange(CK)[:, None] >= jnp.arange(CK)[None, :]).astype(BF16)
    qcol = lambda b, j: (0, b * nq + j)
    return pl.pallas_call(
        functools.partial(_dsa_kernel, top_k=top_k),
        grid=(B, nq),
        in_specs=[
            pl.BlockSpec((A_WIDTH, TQ), qcol),
            pl.BlockSpec((IDX_HEADS * IDX_DIM, TQ), qcol),
            pl.BlockSpec((IDX_HEADS, TQ), qcol),
            pl.BlockSpec((S, A_WIDTH), lambda b, j: (b, 0)),
            pl.BlockSpec((nc, A_WIDTH, CK), lambda b, j: (b, 0, 0)),
            pl.BlockSpec((S, 2 * IDX_DIM), lambda b, j: (b, 0)),
            pl.BlockSpec((CK, CK), lambda b, j: (0, 0)),
        ],
        out_specs=pl.BlockSpec((TQ, A_WIDTH), lambda b, j: (b * nq + j, 0)),
        out_shape=jax.ShapeDtypeStruct((T, A_WIDTH), BF16),
        scratch_shapes=[
            pltpu.VMEM((nc, CK, TQ), F32),
            pltpu.VMEM((nc, CK, TQ), BF16),
            pltpu.VMEM((CK, TQ), F32),
            pltpu.VMEM((A_HEADS, CK, TQ), F32),
            pltpu.VMEM((A_HEADS, LANES, TQ), BF16),
            pltpu.VMEM((IDX_HEADS, LANES, TQ), BF16),
            pltpu.VMEM((A_HEADS, 1, TQ), F32),
            pltpu.VMEM((A_HEADS, 1, TQ), F32),
            pltpu.VMEM((A_HEADS, A_HEAD_DIM, TQ), F32),
            pltpu.VMEM((1, TQ), F32),
            pltpu.VMEM((1, TQ), F32),
        ],
        compiler_params=_cparams(("parallel", "arbitrary")),
        name="dsa",
    )(qt, qit, wit, k, vt, ki2, tri)


LRU_SLABS = 8
HALO = SUBLANES


def _softplus(x):
    return jnp.maximum(x, 0.0) + jnp.log1p(jnp.exp(-jnp.abs(x)))


def _gelu_tanh(x):
    return 0.5 * x * (1.0 + jnp.tanh(0.7978845608028654 * (x + 0.044715 * (x * x * x))))


def _rglru_reset(first, xs_ref, hc_ref):
    @pl.when(first)
    def _():
        xs_ref[0:HALO, :] = jnp.zeros((HALO, xs_ref.shape[1]), F32)
        hc_ref[...] = jnp.zeros(hc_ref.shape, F32)


def _rglru_gates(x, cw_ref, cb_ref, wra_ref, bra_ref, wri_ref, bri_ref, lam_ref, xs_ref, a_ref, b_ref):
    ts, C = x.shape
    xs_ref[HALO:HALO + ts, :] = x

    def slab(r0, n):
        xc = cb_ref[...] + jnp.zeros((n, C), F32)
        for kk in range(CONV_W):
            off = HALO - (CONV_W - 1) + kk + r0
            xc = xc + cw_ref[kk:kk + 1, :] * xs_ref[off:off + n, :]
        xcb = xc.astype(BF16)
        r = _sigmoid(_dot(xcb, wra_ref[...]) + bra_ref[...])
        gi = _sigmoid(_dot(xcb, wri_ref[...]) + bri_ref[...])
        log_a = (-LRU_C) * r * _softplus(-lam_ref[...])
        a = jnp.exp(log_a)
        a_ref[r0:r0 + n, :] = a
        b_ref[r0:r0 + n, :] = jnp.sqrt(-jnp.tanh(log_a) * (1.0 + a * a)) * (gi * xc)

    def finish():
        xs_ref[0:HALO, :] = xs_ref[ts:ts + HALO, :]

    return slab, finish


def _rglru_scan(a_ref, b_ref, hc_ref):
    ts, C = a_ref.shape
    row = lax.broadcasted_iota(jnp.int32, (SUBLANES, C), 0)

    def group(g, carry):
        r0 = pl.multiple_of(g * SUBLANES, SUBLANES)
        av = a_ref[pl.ds(r0, SUBLANES), :]
        bv = b_ref[pl.ds(r0, SUBLANES), :]
        for sh in (1, 2, 4):
            a_sh = pltpu.roll(av, sh, axis=0)
            b_sh = pltpu.roll(bv, sh, axis=0)
            ok = row >= sh
            bv = jnp.where(ok, av * b_sh + bv, bv)
            av = jnp.where(ok, av * a_sh, av)
        h8 = av * carry + bv
        a_ref[pl.ds(r0, SUBLANES), :] = h8
        return jnp.broadcast_to(h8[SUBLANES - 1:SUBLANES, :], (SUBLANES, C))

    hc_ref[...] = lax.fori_loop(0, ts // SUBLANES, group, hc_ref[...])


OD_QKVR = 3072
OD_GLR = (3072, 3200)
OD_COLS = 3200


def _odd_proj_kernel(h_ref, g_ref, w_ref, wg2_ref, bg_ref, q_ref, k_ref, v_ref, r_ref, gk_ref):
    xn = _rms(h_ref[...], g_ref[...]).astype(BF16)

    def seg(a, b):
        return _dot(xn, w_ref[:, a:b])

    q_ref[...] = seg(0, GLA_DK) * (GLA_DKH ** -0.5)
    k_ref[...] = seg(GLA_DK, 2 * GLA_DK)
    v_ref[...] = seg(2 * GLA_DK, 2 * GLA_DK + GLA_DV).astype(BF16)
    r_ref[...] = seg(2 * GLA_DK + GLA_DV, OD_QKVR)
    glr = seg(*OD_GLR).astype(BF16)
    z = _dot(glr, wg2_ref[...]) + bg_ref[...]
    gk_ref[...] = (-_softplus(-z)) * (1.0 / GLA_TAU)


def _odd_proj(h, g_pre, w_in, w_g2, b_g):
    T, D = h.shape
    tm = min(PROJ_TM, T)
    row = lambda i: (i, 0)
    fixed = lambda i: (0, 0)
    outs = [(GLA_DK, F32), (GLA_DK, F32), (GLA_DV, BF16), (GLA_DV, F32), (GLA_DK, F32)]
    return pl.pallas_call(
        _odd_proj_kernel,
        grid=(T // tm,),
        in_specs=[
            pl.BlockSpec((tm, D), row),
            pl.BlockSpec((1, D), fixed),
            _resident(w_in),
            pl.BlockSpec(w_g2.shape, fixed),
            pl.BlockSpec((1, GLA_DK), fixed),
        ],
        out_specs=[pl.BlockSpec((tm, n), row) for n, _ in outs],
        out_shape=[jax.ShapeDtypeStruct((T, n), dt) for n, dt in outs],
        compiler_params=_cparams(("parallel",)),
        name="odd_proj",
    )(h, g_pre, w_in, w_g2, b_g)


GLA_TS = 256


def _gla_kernel(q_ref, k_ref, v_ref, gk_ref, r_ref, hn_ref, tri_ref, o_ref,
                st_ref, qd_ref, oi_ref, u_ref, stb_ref):
    ts = q_ref.shape[1]
    C = GLA_CHUNK
    nch = ts // C

    @pl.when(pl.program_id(1) == 0)
    def _():
        st_ref[...] = jnp.zeros(st_ref.shape, F32)

    gk = gk_ref[0]
    g_hi = gk.astype(BF16)
    rem = gk - g_hi.astype(F32)
    g_mid = rem.astype(BF16)
    g_lo = (rem - g_mid.astype(F32)).astype(BF16)
    tri = tri_ref[...]
    G = _dot(tri, g_hi) + _dot(tri, g_mid) + _dot(tri, g_lo)

    kf = k_ref[0]
    qd_ref[...] = (q_ref[0] * jnp.exp(G)).astype(BF16)
    k_inv = (kf * jnp.exp(-G)).astype(BF16)
    g_last = [G[(c + 1) * C - 1:(c + 1) * C, :] for c in range(nch)]
    k_rem = jnp.concatenate(
        [kf[c * C:(c + 1) * C, :] * jnp.exp(g_last[c] - G[c * C:(c + 1) * C, :]) for c in range(nch)],
        axis=0).astype(BF16)

    ri = lax.broadcasted_iota(jnp.int32, (ts, ts), 0)
    ci = lax.broadcasted_iota(jnp.int32, (ts, ts), 1)
    same_chunk_causal = (ri >= ci) & (ri // C == ci // C)
    for h in range(GLA_HEADS):
        ksl = slice(h * GLA_DKH, (h + 1) * GLA_DKH)
        vsl = slice(h * GLA_DVH, (h + 1) * GLA_DVH)
        att = jnp.where(same_chunk_causal, _dot_nt(qd_ref[:, ksl], k_inv[:, ksl]), 0.0).astype(BF16)
        oi_ref[:, vsl] = _dot(att, v_ref[0, :, vsl])

    for c in range(nch):
        rows = slice(c * C, (c + 1) * C)
        for h in range(GLA_HEADS):
            ksl = slice(h * GLA_DKH, (h + 1) * GLA_DKH)
            vsl = slice(h * GLA_DVH, (h + 1) * GLA_DVH)
            u_ref[c, h] = _dot_tn(v_ref[0, rows, vsl], k_rem[rows, ksl])

    for h in range(GLA_HEADS):
        ksl = slice(h * GLA_DKH, (h + 1) * GLA_DKH)
        st = st_ref[h]
        for c in range(nch):
            stb_ref[c, h] = st.astype(BF16)
            st = st * jnp.exp(g_last[c][:, ksl]) + u_ref[c, h]
        st_ref[h] = st

    for c in range(nch):
        rows = slice(c * C, (c + 1) * C)
        for h in range(GLA_HEADS):
            ksl = slice(h * GLA_DKH, (h + 1) * GLA_DKH)
            vsl = slice(h * GLA_DVH, (h + 1) * GLA_DVH)
            o = oi_ref[rows, vsl] + _dot_nt(qd_ref[rows, ksl], stb_ref[c, h])
            on = _rms(o, hn_ref[...])
            rr = r_ref[0, rows, vsl]
            o_ref[0, rows, vsl] = (on * (rr * _sigmoid(rr))).astype(o_ref.dtype)


def _gla(q, k, v, gk, r, head_norm):
    B, S, _ = q.shape
    ts = min(GLA_TS, S)
    nch = ts // GLA_CHUNK
    pos = jnp.arange(ts)
    tri = ((pos[:, None] >= pos[None, :])
           & (pos[:, None] // GLA_CHUNK == pos[None, :] // GLA_CHUNK)).astype(BF16)
    blk = lambda b, s: (b, s, 0)
    return pl.pallas_call(
        _gla_kernel,
        grid=(B, S // ts),
        in_specs=[
            pl.BlockSpec((1, ts, GLA_DK), blk),
            pl.BlockSpec((1, ts, GLA_DK), blk),
            pl.BlockSpec((1, ts, GLA_DV), blk),
            pl.BlockSpec((1, ts, GLA_DK), blk),
            pl.BlockSpec((1, ts, GLA_DV), blk),
            pl.BlockSpec((1, GLA_DVH), lambda b, s: (0, 0)),
            pl.BlockSpec((ts, ts), lambda b, s: (0, 0)),
        ],
        out_specs=pl.BlockSpec((1, ts, GLA_DV), blk),
        out_shape=jax.ShapeDtypeStruct((B, S, GLA_DV), BF16),
        scratch_shapes=[
            pltpu.VMEM((GLA_HEADS, GLA_DVH, GLA_DKH), F32),
            pltpu.VMEM((ts, GLA_DK), BF16),
            pltpu.VMEM((ts, GLA_DV), F32),
            pltpu.VMEM((nch, GLA_HEADS, GLA_DVH, GLA_DKH), F32),
            pltpu.VMEM((nch, GLA_HEADS, GLA_DVH, GLA_DKH), BF16),
        ],
        compiler_params=_cparams(("parallel", "arbitrary")),
        name="gla",
    )(q, k, v, gk, r, head_norm, tri)


def _xa_kv_kernel(mem_ref, g_ref, w_ref, k_ref, v_ref):
    mn = _rms(mem_ref[...], g_ref[...]).astype(BF16)
    k_ref[...] = (_dot(mn, w_ref[:, :D_MODEL].astype(BF16)) * (XA_HEAD_DIM ** -0.5)).astype(BF16)
    v_ref[...] = _dot(mn, w_ref[:, D_MODEL:].astype(BF16)).astype(BF16)


def _xa_kv(mem, g_mem, w_kv, lead):
    B, M, D = mem.shape
    whole = lambda i: (0, 0)
    k, v = pl.pallas_call(
        _xa_kv_kernel,
        grid=(1,),
        in_specs=[pl.BlockSpec((B * M, D), whole), pl.BlockSpec((1, D), whole), _resident(w_kv, lead)],
        out_specs=[pl.BlockSpec((B * M, D), whole), pl.BlockSpec((B * M, D), whole)],
        out_shape=[jax.ShapeDtypeStruct((B * M, D), BF16)] * 2,
        compiler_params=_cparams(("arbitrary",)),
        name="xa_kv",
    )(mem.reshape(B * M, D), g_mem, w_kv)
    return k.reshape(B, M, D), v.reshape(B, M, D)


XA_TM = 1024


def _mix_out_xa_kernel(*refs, offsets):
    n = len(offsets)
    h_ref, gmix_ref, gpre_ref, gpost_ref = refs[:4]
    part_refs = refs[4:4 + n]
    wout_ref, wq_ref, k_ref, v_ref, wo_ref, o_ref = refs[4 + n:]
    m = None
    for p_ref, off in zip(part_refs, offsets):
        kk = p_ref.shape[-1]
        term = _dot(p_ref[0], wout_ref[off:off + kk, :])
        m = term if m is None else m + term
    x = h_ref[0] + _rms(m, gmix_ref[...])
    xn = _rms(x, gpre_ref[...]).astype(BF16)
    q = _dot(xn, wq_ref[...]).astype(BF16)
    heads = []
    for h in range(XA_HEADS):
        sl = slice(h * XA_HEAD_DIM, (h + 1) * XA_HEAD_DIM)
        s = _dot_nt(q[:, sl], k_ref[0, :, sl])
        p = jnp.exp(s - jnp.max(s, axis=-1, keepdims=True))
        oh = _dot(p.astype(BF16), v_ref[0, :, sl]) / jnp.sum(p, axis=-1, keepdims=True)
        heads.append(oh.astype(BF16))
    c = _dot(jnp.concatenate(heads, axis=-1), wo_ref[...])
    o_ref[0] = x + _rms(c, gpost_ref[...])


def _mix_out_xa(h, g_mix, g_pre, g_post, parts, w_out, out_lead, w_q, kx, vx, w_o, lead):
    B, S, D = h.shape
    M = kx.shape[1]
    tm = min(XA_TM, S)
    blk = lambda b, i: (b, i, 0)
    fixed = lambda b, i: (0, 0)
    offsets, off = [], 0
    for p in parts:
        offsets.append(off)
        off += p.shape[-1]
    return pl.pallas_call(
        functools.partial(_mix_out_xa_kernel, offsets=tuple(offsets)),
        grid=(B, S // tm),
        in_specs=[
            pl.BlockSpec((1, tm, D), blk),
            pl.BlockSpec((1, D), fixed),
            pl.BlockSpec((1, D), fixed),
            pl.BlockSpec((1, D), fixed),
            *[pl.BlockSpec((1, tm, p.shape[-1]), blk) for p in parts],
            _resident(w_out, out_lead),
            _resident(w_q, lead),
            pl.BlockSpec((1, M, D), lambda b, i: (b, 0, 0)),
            pl.BlockSpec((1, M, D), lambda b, i: (b, 0, 0)),
            _resident(w_o, lead),
        ],
        out_specs=pl.BlockSpec((1, tm, D), blk),
        out_shape=jax.ShapeDtypeStruct((B, S, D), F32),
        compiler_params=_cparams(("parallel", "parallel")),
        name="mix_out_xa",
    )(h, g_mix, g_pre, g_post, *parts, w_out, w_q, kx, vx, w_o)


def _block_diag(w):
    G, n, _ = w.shape
    eye = jnp.eye(G, dtype=w.dtype)
    return (eye[:, None, :, None] * w[:, :, None, :]).reshape(G * n, G * n)


def _even_w_in(w):
    ki = w[:, 1280:1344]
    pad = jnp.zeros((w.shape[0], LANES - IDX_HEADS), w.dtype)
    return jnp.concatenate([w[:, :1280], ki, ki, w[:, 1352:2376], w[:, 1344:1352], pad], axis=1).astype(BF16)


def _odd_w_in(w):
    pad = jnp.zeros((w.shape[0], LANES - GLA_GATE_RANK), w.dtype)
    return jnp.concatenate([w[:, :2048], w[:, 2064:3088], w[:, 2048:2064], pad], axis=1).astype(BF16)


def kernel(x, mem, norms, ffn_w_gu, ffn_w_down, xa_w_q, xa_w_kv, xa_w_o, ev_w_in, ev_kv_norm, ev_w_uk, ev_w_uv, ev_conv_w, ev_conv_b, ev_w_ra, ev_b_ra, ev_w_ri, ev_b_ri, ev_lam, ev_w_out, od_w_in, od_w_g2, od_b_g, od_head_norm, od_w_out):
    B, S, D = x.shape
    T = B * S
    depth = norms.shape[0]
    h = x.reshape(T, D)

    def gain(layer, idx):
        return norms[layer, idx][None, :]

    w_q, w_o = xa_w_q.astype(BF16), xa_w_o.astype(BF16)
    ev_out, od_out = ev_w_out.astype(BF16), od_w_out.astype(BF16)
    ffn_order = [(layer, j) for layer in range(depth) for j in range(2)]
    ffn_w = (ffn_w_gu[0, 0].astype(BF16), ffn_w_down[0, 0].astype(BF16))

    def ffn(h, ffn_w, layer, j, g_pre, g_post):
        k = ffn_order.index((layer, j))
        if k + 1 == len(ffn_order):
            return _ffn(h, g_pre, g_post, *ffn_w), None
        h, gu_next, down_next = _ffn(h, g_pre, g_post, *ffn_w,
                                     nxt=(ffn_w_gu, ffn_w_down, ffn_order[k + 1]))
        return h, (gu_next, down_next)

    for layer in range(depth):
        h, ffn_w = ffn(h, ffn_w, layer, 0, gain(layer, N_FFN1_PRE), gain(layer, N_FFN1_POST))

        if layer % 2 == 0:
            e = layer // 2
            lru_params = (ev_conv_w[e], ev_conv_b[e][None, :],
                          _block_diag(ev_w_ra[e]).astype(BF16), ev_b_ra[e].reshape(1, B_WIDTH),
                          _block_diag(ev_w_ri[e]).astype(BF16), ev_b_ri[e].reshape(1, B_WIDTH),
                          ev_lam[e][None, :])
            qt, k, vt, qit, ki2, wit, b_out = _even_proj(
                h, gain(layer, N_MIX_PRE), _even_w_in(ev_w_in[e]), ev_kv_norm[e][None, :],
                ev_w_uk[e].astype(BF16), ev_w_uv[e].astype(BF16), lru_params, S)
            a_out = _dsa(qt, k, vt, qit, ki2, wit, B)
            parts = [a_out.reshape(B, S, A_WIDTH), b_out.reshape(B, S, B_WIDTH)]
            w_out, w_out_lead = ev_out, (e,)
        else:
            o = layer // 2
            w_g2 = jnp.concatenate(
                [od_w_g2[o], jnp.zeros((LANES - GLA_GATE_RANK, GLA_DK), od_w_g2.dtype)], axis=0).astype(BF16)
            q, k, v, r, gk = _odd_proj(h, gain(layer, N_MIX_PRE), _odd_w_in(od_w_in[o]), w_g2,
                                       od_b_g[o][None, :])
            r3 = lambda a: a.reshape(B, S, a.shape[-1])
            g_out = _gla(r3(q), r3(k), r3(v), r3(gk), r3(r), od_head_norm[o][None, :])
            parts = [g_out]
            w_out, w_out_lead = od_out, (o,)

        kx, vx = _xa_kv(mem, gain(layer, N_MEM_NORM), xa_w_kv, (layer,))
        h = _mix_out_xa(h.reshape(B, S, D), gain(layer, N_MIX_POST), gain(layer, N_XA_PRE),
                        gain(layer, N_XA_POST), parts, w_out, w_out_lead,
                        w_q, kx, vx, w_o, (layer,)).reshape(T, D)

        h, ffn_w = ffn(h, ffn_w, layer, 1, gain(layer, N_FFN2_PRE), gain(layer, N_FFN2_POST))
    return h.reshape(B, S, D)
```

```python
import functools

import jax
import jax.numpy as jnp
from jax import lax
from jax.experimental import pallas as pl
from jax.experimental.pallas import tpu as pltpu

F32 = jnp.float32
BF16 = jnp.bfloat16

EPS = 1e-6
D_MODEL = 1024
D_FF = 2816
XA_HEADS = 4
XA_HEAD_DIM = D_MODEL // XA_HEADS
A_HEADS = 8
A_HEAD_DIM = 64
A_WIDTH = A_HEADS * A_HEAD_DIM
KV_RANK = 256
IDX_HEADS = 8
IDX_DIM = 64
TOPK_MAX = 256
B_WIDTH = D_MODEL - A_WIDTH
B_BLOCKS = 8
B_BLOCK_DIM = B_WIDTH // B_BLOCKS
CONV_W = 4
LRU_C = 8.0
GLA_HEADS = 4
GLA_DK = D_MODEL // 2
GLA_DV = D_MODEL
GLA_DKH = GLA_DK // GLA_HEADS
GLA_DVH = GLA_DV // GLA_HEADS
GLA_GATE_RANK = 16
GLA_TAU = 16.0
GLA_CHUNK = 64
(N_FFN1_PRE, N_FFN1_POST, N_MIX_PRE, N_MIX_POST, N_XA_PRE, N_XA_POST, N_MEM_NORM,
 N_FFN2_PRE, N_FFN2_POST) = range(9)

LANES = 128
SUBLANES = 8
VMEM_LIMIT = 48 * 1024 * 1024

NEG_BIG = -1e30
LOG2E = 1.4426950408889634
INT_MIN = -2 ** 31
POS_INF_CODE = 0x7F800000
NEG_INF_CODE = -0x7F800001


def _cparams(sem):
    return pltpu.CompilerParams(dimension_semantics=sem, vmem_limit_bytes=VMEM_LIMIT)


def _rms(x, g):
    return x * lax.rsqrt(jnp.mean(x * x, axis=-1, keepdims=True) + EPS) * g


def _dot(a, b):
    return jnp.dot(a, b, preferred_element_type=F32)


def _dot_nt(a, b):
    return lax.dot_general(a, b, (((1,), (1,)), ((), ())), preferred_element_type=F32)


def _dot_tn(a, b):
    return lax.dot_general(a, b, (((0,), (0,)), ((), ())), preferred_element_type=F32)


def _sigmoid(x):
    return 1.0 / (1.0 + jnp.exp(-x))


FFN_TM = 1024
FFN_SLAB = 512
FFN_TF = 256


def _ffn_kernel(*refs, n_casts):
    h_ref, gpre_ref, gpost_ref, wgu_ref, wd_ref = refs[:5]
    cast_in = refs[5:5 + n_casts]
    o_ref = refs[5 + n_casts]
    cast_out = refs[6 + n_casts:6 + 2 * n_casts]
    act_ref = refs[6 + 2 * n_casts]
    for src_ref, dst_ref in zip(cast_in, cast_out):
        dst_ref[...] = src_ref[...].astype(BF16)
    F = wd_ref.shape[0]
    for r0 in range(0, h_ref.shape[0], FFN_SLAB):
        rows = slice(r0, r0 + FFN_SLAB)
        x = h_ref[rows, :]
        xn = _rms(x, gpre_ref[...]).astype(BF16)
        for c in range(F // FFN_TF):
            g = _dot(xn, wgu_ref[:, c * FFN_TF:(c + 1) * FFN_TF])
            u = _dot(xn, wgu_ref[:, F + c * FFN_TF:F + (c + 1) * FFN_TF])
            act_ref[rows, c * FFN_TF:(c + 1) * FFN_TF] = (g * _sigmoid(g) * u).astype(BF16)
        f = _dot(act_ref[rows, :], wd_ref[...])
        o_ref[rows, :] = x + 0.5 * _rms(f, gpost_ref[...])


def _resident(arr, lead=()):
    tail = arr.shape[len(lead):]
    index = tuple(lead) + (0,) * len(tail)
    return pl.BlockSpec((None,) * len(lead) + tail, lambda *_: index, pipeline_mode=pl.Buffered(1))


def _ffn(h, g_pre, g_post, w_gu, w_down, side_casts=()):
    T, D = h.shape
    F = w_down.shape[0]
    tm = min(FFN_TM, T)
    steps = T // tm
    in_specs = [
        pl.BlockSpec((tm, D), lambda i: (i, 0)),
        pl.BlockSpec((1, D), lambda i: (0, 0)),
        pl.BlockSpec((1, D), lambda i: (0, 0)),
        _resident(w_gu),
        _resident(w_down),
    ]
    out_specs = [pl.BlockSpec((tm, D), lambda i: (i, 0))]
    out_shape = [jax.ShapeDtypeStruct((T, D), F32)]
    args = [h, g_pre, g_post, w_gu, w_down]
    cast_specs = []
    for w_all, lead in side_casts:
        rows, cols = w_all.shape[-2:]
        slab = rows // steps
        assert slab * steps == rows and slab % 16 == 0, (rows, steps)
        cast_specs.append(pl.BlockSpec((None,) * len(lead) + (slab, cols),
                                       lambda i, lead=tuple(lead): lead + (i, 0)))
        out_specs.append(pl.BlockSpec((slab, cols), lambda i: (i, 0)))
        out_shape.append(jax.ShapeDtypeStruct((rows, cols), BF16))
        args.append(w_all)
    outs = pl.pallas_call(
        functools.partial(_ffn_kernel, n_casts=len(side_casts)),
        grid=(steps,),
        in_specs=in_specs + cast_specs,
        out_specs=out_specs,
        out_shape=out_shape,
        scratch_shapes=[pltpu.VMEM((tm, F), BF16)],
        compiler_params=_cparams(("parallel",)),
        name="ffn",
    )(*args)
    return outs[0], list(outs[1:])


PROJ_TM = 1024


EV_Q = (0, 512)
EV_CKV = (512, 768)
EV_QI = (768, 1280)
EV_KI2 = (1280, 1408)
EV_GATE = (1408, 1920)
EV_XB = (1920, 2432)
EV_WI = (2432, 2560)
EV_COLS = 2560


DSA_TQ = 512
DSA_CK = 512


def _even_proj_kernel(h_ref, g_ref, w_ref, kvn_ref, wuk_ref, wuv_ref,
                      cw_ref, cb_ref, wra_ref, bra_ref, wri_ref, bri_ref, lam_ref,
                      qt_ref, k_ref, vt_ref, qit_ref, ki_ref, wit_ref, bout_ref,
                      xs_ref, a_ref, b_ref, hc_ref, gg_ref, *, tiles_per_seq):
    _rglru_reset(pl.program_id(0) % tiles_per_seq == 0, xs_ref, hc_ref)
    xn = _rms(h_ref[...], g_ref[...]).astype(BF16)

    def seg(ab):
        return _dot(xn, w_ref[:, ab[0]:ab[1]])

    tm = h_ref.shape[0]
    lru_slab, lru_finish = _rglru_gates(seg(EV_XB), cw_ref, cb_ref, wra_ref, bra_ref, wri_ref, bri_ref,
                                        lam_ref, xs_ref, a_ref, b_ref)
    n_slab = tm // LRU_SLABS

    def p_gate():
        gg_ref[...] = _gelu_tanh(seg(EV_GATE))

    def p_q():
        qt_ref[...] = (seg(EV_Q) * (A_HEAD_DIM ** -0.5 * LOG2E)).T.astype(BF16)

    ckv = []

    def p_k():
        ckv.append(_rms(seg(EV_CKV), kvn_ref[...]).astype(BF16))
        k_ref[...] = _dot(ckv[0], wuk_ref[...]).astype(BF16)

    def p_v():
        v = _dot(ckv[0], wuv_ref[...]).astype(BF16)
        for c in range(vt_ref.shape[0]):
            vt_ref[c] = v[c * DSA_CK:(c + 1) * DSA_CK, :].T

    def p_qi():
        qit_ref[...] = (seg(EV_QI) * (IDX_DIM ** -0.5)).T.astype(BF16)

    def p_ki():
        ki_ref[...] = seg(EV_KI2).astype(BF16)

    def p_wi():
        wit_ref[...] = (seg(EV_WI) * (IDX_HEADS ** -0.5)).T[:IDX_HEADS, :]

    pieces = (p_gate, p_q, p_k, p_v, p_qi, p_ki, p_wi)
    for s in range(LRU_SLABS):
        lru_slab(s * n_slab, n_slab)
        if s < len(pieces):
            pieces[s]()
    lru_finish()
    _rglru_scan(a_ref, b_ref, hc_ref)
    bout_ref[...] = (a_ref[...] * gg_ref[...]).astype(bout_ref.dtype)


def _even_proj(h, g_pre, w_in, kv_norm, w_uk, w_uv, lru_params, seq_len):
    T, D = h.shape
    tm = min(PROJ_TM, seq_len)
    C = B_WIDTH
    row = lambda i: (i, 0)
    col = lambda i: (0, i)
    fixed = lambda i: (0, 0)
    out_specs = [
        pl.BlockSpec((A_WIDTH, tm), col),
        pl.BlockSpec((tm, A_WIDTH), row),
        pl.BlockSpec((tm // DSA_CK, A_WIDTH, DSA_CK), lambda i: (i, 0, 0)),
        pl.BlockSpec((IDX_HEADS * IDX_DIM, tm), col),
        pl.BlockSpec((tm, 2 * IDX_DIM), row),
        pl.BlockSpec((IDX_HEADS, tm), col),
        pl.BlockSpec((tm, C), row),
    ]
    out_shape = [
        jax.ShapeDtypeStruct((A_WIDTH, T), BF16),
        jax.ShapeDtypeStruct((T, A_WIDTH), BF16),
        jax.ShapeDtypeStruct((T // DSA_CK, A_WIDTH, DSA_CK), BF16),
        jax.ShapeDtypeStruct((IDX_HEADS * IDX_DIM, T), BF16),
        jax.ShapeDtypeStruct((T, 2 * IDX_DIM), BF16),
        jax.ShapeDtypeStruct((IDX_HEADS, T), F32),
        jax.ShapeDtypeStruct((T, C), BF16),
    ]
    return pl.pallas_call(
        functools.partial(_even_proj_kernel, tiles_per_seq=seq_len // tm),
        grid=(T // tm,),
        in_specs=[
            pl.BlockSpec((tm, D), row),
            pl.BlockSpec((1, D), fixed),
            _resident(w_in),
            pl.BlockSpec((1, KV_RANK), fixed),
            pl.BlockSpec(w_uk.shape, fixed),
            pl.BlockSpec(w_uv.shape, fixed),
            *[pl.BlockSpec(p.shape, fixed) for p in lru_params],
        ],
        out_specs=out_specs,
        out_shape=out_shape,
        scratch_shapes=[
            pltpu.VMEM((HALO + tm, C), F32),
            pltpu.VMEM((tm, C), F32),
            pltpu.VMEM((tm, C), F32),
            pltpu.VMEM((SUBLANES, C), F32),
            pltpu.VMEM((tm, C), F32),
        ],
        compiler_params=_cparams(("arbitrary",)),
        name="even_proj",
    )(h, g_pre, w_in, kv_norm, w_uk, w_uv, *lru_params)


ACC_ROWS = 4 * SUBLANES
ACC16_ROWS = 2 * ACC_ROWS
SEARCH_PASSES = 18


def _col_partial(x, op):
    rows, n = x.shape
    part = x.reshape(rows // ACC_ROWS, ACC_ROWS, n)
    return jnp.max(part, axis=0) if op == "max" else jnp.sum(part, axis=0)


def _col_reduce(x, op):
    part = _col_partial(x, op)
    return (jnp.max(part, axis=0, keepdims=True) if op == "max"
            else jnp.sum(part, axis=0, keepdims=True))


def _dsa_kernel(qt_ref, qit_ref, wit_ref, k_ref, vt_ref, ki_ref, tri_ref, o_ref,
                sc_ref, hb_ref, bias_ref, s_ref, qm_ref, qim_ref, m_ref, l_ref, acc_ref, need_ref, seen_ref,
                *, top_k):
    TQ, CK = DSA_TQ, DSA_CK
    j = pl.program_id(1)
    q0 = j * TQ
    nkc = (q0 + TQ + CK - 1) // CK

    low_half = lax.broadcasted_iota(jnp.int32, (LANES, TQ), 0) < A_HEAD_DIM
    for h in range(A_HEADS):
        pr = slice((h // 2) * LANES, (h // 2 + 1) * LANES)
        keep = low_half if h % 2 == 0 else jnp.logical_not(low_half)
        qm_ref[h] = jnp.where(keep, qt_ref[pr, :], jnp.zeros((), BF16))
        qim_ref[h] = jnp.where(keep, qit_ref[pr, :], jnp.zeros((), BF16))

    key_iota = lax.broadcasted_iota(jnp.int32, (CK, TQ), 0)
    q_pos = q0 + lax.broadcasted_iota(jnp.int32, (CK, TQ), 1)

    def score_chunk(c, carry):
        base = pl.multiple_of(c * CK, CK)
        kic = ki_ref[pl.ds(base, CK), :]
        acc = jnp.zeros((CK, TQ), F32)
        for h in range(IDX_HEADS):
            acc = acc + jnp.maximum(_dot(kic, qim_ref[h]), 0.0) * wit_ref[h:h + 1, :]
        sc = jnp.where(base + key_iota <= q_pos, acc, -jnp.inf)
        sc_ref[c] = sc
        hb_ref[c] = sc.astype(BF16)
        return carry

    lax.fori_loop(0, nkc, score_chunk, 0)

    def count(pred_fn):
        def body(c, cnt):
            for g in range(CK // ACC_ROWS):
                hit = pred_fn(sc_ref[c, g * ACC_ROWS:(g + 1) * ACC_ROWS, :])
                cnt = jnp.where(hit, cnt + 1, cnt)
            return cnt
        cnt = lax.fori_loop(0, nkc, body, jnp.zeros((ACC_ROWS, TQ), jnp.int32))
        return jnp.sum(cnt, axis=0, keepdims=True)

    def code_to_float(code):
        code = jnp.clip(code, NEG_INF_CODE, POS_INF_CODE)
        return lax.bitcast_convert_type(code ^ ((code >> 31) & jnp.int32(0x7FFFFFFF)), F32)

    def count16(cand_b):
        def body(c, cnt):
            for g in range(CK // ACC16_ROWS):
                blk = hb_ref[c, g * ACC16_ROWS:(g + 1) * ACC16_ROWS, :]
                cnt = jnp.where(blk >= cand_b, cnt + jnp.ones((), BF16), cnt)
            return cnt
        cnt = lax.fori_loop(0, nkc, body, jnp.zeros((ACC16_ROWS, TQ), BF16))
        return jnp.sum(cnt.astype(F32), axis=0, keepdims=True)

    def bit_pass16(i, prefix):
        cand = prefix + lax.shift_left(jnp.int32(1), 31 - i)
        grid = jnp.where(cand < 0, cand | jnp.int32(0xFFFF), cand)
        cand_b = jnp.broadcast_to(code_to_float(grid), (ACC16_ROWS, TQ)).astype(BF16)
        return jnp.where(count16(cand_b) >= top_k, cand, prefix)

    prefix = lax.fori_loop(0, 16, bit_pass16, jnp.full((1, TQ), INT_MIN, jnp.int32))

    def search_pass(i, carry):
        lo, hi, n_at = carry
        mid = lo + ((hi - lo) >> 1)
        mid_f = code_to_float(mid)
        n_ge = count(lambda sc: sc >= mid_f)
        take = n_ge >= top_k
        return jnp.where(take, mid, lo), jnp.where(take, hi, mid), jnp.where(take, n_ge, n_at)

    lo0 = jnp.maximum(prefix, INT_MIN + 0x10000) - 0x10000
    hi0 = jnp.minimum(prefix, 0x7FFD0000) + 0x20000
    code, _, n_at = lax.fori_loop(
        0, SEARCH_PASSES, search_pass, (lo0, hi0, jnp.full((1, TQ), nkc * CK, jnp.int32)))
    thr = code_to_float(code)
    finite = thr > -jnp.inf
    has_ties = jnp.max(jnp.where(finite & (n_at > top_k), 1, 0)) > 0
    thr_sel = jnp.where(finite, thr, jnp.finfo(F32).min)

    m_ref[...] = jnp.full(m_ref.shape, -jnp.inf, F32)
    l_ref[...] = jnp.zeros(l_ref.shape, F32)
    acc_ref[...] = jnp.zeros(acc_ref.shape, F32)
    need_ref[...] = jnp.zeros(need_ref.shape, F32)
    seen_ref[...] = jnp.zeros(seen_ref.shape, F32)

    @pl.when(has_ties)
    def _():
        n_gt = count(lambda sc: sc > thr)
        need_ref[...] = jnp.where(finite, (top_k - n_gt).astype(F32), 0.0)

    def attend_chunk(c, carry):
        base = pl.multiple_of(c * CK, CK)

        @pl.when(jnp.logical_not(has_ties))
        def _():
            bias_ref[...] = jnp.where(sc_ref[c] >= thr_sel, 0.0, NEG_BIG)

        @pl.when(has_ties)
        def _():
            sc = sc_ref[c]
            eq = sc == thr
            rank = seen_ref[...] + _dot(tri_ref[...], eq.astype(BF16))
            sel = (sc > thr) | (eq & (rank <= need_ref[...]))
            bias_ref[...] = jnp.where(sel, 0.0, NEG_BIG)
            seen_ref[...] += _col_reduce(eq.astype(F32), "sum")

        cmax = []
        for h in range(A_HEADS):
            kp = k_ref[pl.ds(base, CK), (h // 2) * LANES:(h // 2 + 1) * LANES]
            s = _dot(kp, qm_ref[h]) + bias_ref[...]
            s_ref[h] = s
            cmax.append(_col_reduce(s, "max"))
        for h in range(A_HEADS):
            m_old = m_ref[h]
            m_new = jnp.maximum(m_old, cmax[h])
            alpha = jnp.exp2(m_old - m_new)
            p = jnp.exp2(s_ref[h] - m_new)
            l_ref[h] = alpha * l_ref[h] + _col_reduce(p, "sum")
            m_ref[h] = m_new
            vth = vt_ref[c, h * A_HEAD_DIM:(h + 1) * A_HEAD_DIM, :]
            acc_ref[h] = acc_ref[h] * alpha + _dot(vth, p.astype(BF16))
        return carry

    lax.fori_loop(0, nkc, attend_chunk, 0)

    out_t = jnp.concatenate([acc_ref[h] / l_ref[h] for h in range(A_HEADS)], axis=0)
    o_ref[...] = out_t.T.astype(o_ref.dtype)


def _dsa(qt, k, vt, qit, ki2, wit, B):
    T = k.shape[0]
    S = T // B
    TQ, CK = DSA_TQ, DSA_CK
    nc, nq = S // CK, S // TQ
    top_k = min(TOPK_MAX, S // 4)
    tri = (jnp.arange(CK)[:, None] >= jnp.arange(CK)[None, :]).astype(BF16)
    qcol = lambda b, j: (0, b * nq + j)
    return pl.pallas_call(
        functools.partial(_dsa_kernel, top_k=top_k),
        grid=(B, nq),
        in_specs=[
            pl.BlockSpec((A_WIDTH, TQ), qcol),
            pl.BlockSpec((IDX_HEADS * IDX_DIM, TQ), qcol),
            pl.BlockSpec((IDX_HEADS, TQ), qcol),
            pl.BlockSpec((S, A_WIDTH), lambda b, j: (b, 0)),
            pl.BlockSpec((nc, A_WIDTH, CK), lambda b, j: (b, 0, 0)),
            pl.BlockSpec((S, 2 * IDX_DIM), lambda b, j: (b, 0)),
            pl.BlockSpec((CK, CK), lambda b, j: (0, 0)),
        ],
        out_specs=pl.BlockSpec((TQ, A_WIDTH), lambda b, j: (b * nq + j, 0)),
        out_shape=jax.ShapeDtypeStruct((T, A_WIDTH), BF16),
        scratch_shapes=[
            pltpu.VMEM((nc, CK, TQ), F32),
            pltpu.VMEM((nc, CK, TQ), BF16),
            pltpu.VMEM((CK, TQ), F32),
            pltpu.VMEM((A_HEADS, CK, TQ), F32),
            pltpu.VMEM((A_HEADS, LANES, TQ), BF16),
            pltpu.VMEM((IDX_HEADS, LANES, TQ), BF16),
            pltpu.VMEM((A_HEADS, 1, TQ), F32),
            pltpu.VMEM((A_HEADS, 1, TQ), F32),
            pltpu.VMEM((A_HEADS, A_HEAD_DIM, TQ), F32),
            pltpu.VMEM((1, TQ), F32),
            pltpu.VMEM((1, TQ), F32),
        ],
        compiler_params=_cparams(("parallel", "arbitrary")),
        name="dsa",
    )(qt, qit, wit, k, vt, ki2, tri)


LRU_SLABS = 8
HALO = SUBLANES


def _softplus(x):
    return jnp.maximum(x, 0.0) + jnp.log1p(jnp.exp(-jnp.abs(x)))


def _gelu_tanh(x):
    return 0.5 * x * (1.0 + jnp.tanh(0.7978845608028654 * (x + 0.044715 * (x * x * x))))


def _rglru_reset(first, xs_ref, hc_ref):
    @pl.when(first)
    def _():
        xs_ref[0:HALO, :] = jnp.zeros((HALO, xs_ref.shape[1]), F32)
        hc_ref[...] = jnp.zeros(hc_ref.shape, F32)


def _rglru_gates(x, cw_ref, cb_ref, wra_ref, bra_ref, wri_ref, bri_ref, lam_ref, xs_ref, a_ref, b_ref):
    ts, C = x.shape
    xs_ref[HALO:HALO + ts, :] = x

    def slab(r0, n):
        xc = cb_ref[...] + jnp.zeros((n, C), F32)
        for kk in range(CONV_W):
            off = HALO - (CONV_W - 1) + kk + r0
            xc = xc + cw_ref[kk:kk + 1, :] * xs_ref[off:off + n, :]
        xcb = xc.astype(BF16)
        r = _sigmoid(_dot(xcb, wra_ref[...]) + bra_ref[...])
        gi = _sigmoid(_dot(xcb, wri_ref[...]) + bri_ref[...])
        log_a = (-LRU_C) * r * _softplus(-lam_ref[...])
        a = jnp.exp(log_a)
        a_ref[r0:r0 + n, :] = a
        b_ref[r0:r0 + n, :] = jnp.sqrt(-jnp.tanh(log_a) * (1.0 + a * a)) * (gi * xc)

    def finish():
        xs_ref[0:HALO, :] = xs_ref[ts:ts + HALO, :]

    return slab, finish


def _rglru_scan(a_ref, b_ref, hc_ref):
    ts, C = a_ref.shape
    row = lax.broadcasted_iota(jnp.int32, (SUBLANES, C), 0)

    def group(g, carry):
        r0 = pl.multiple_of(g * SUBLANES, SUBLANES)
        av = a_ref[pl.ds(r0, SUBLANES), :]
        bv = b_ref[pl.ds(r0, SUBLANES), :]
        for sh in (1, 2, 4):
            a_sh = pltpu.roll(av, sh, axis=0)
            b_sh = pltpu.roll(bv, sh, axis=0)
            ok = row >= sh
            bv = jnp.where(ok, av * b_sh + bv, bv)
            av = jnp.where(ok, av * a_sh, av)
        h8 = av * carry + bv
        a_ref[pl.ds(r0, SUBLANES), :] = h8
        return jnp.broadcast_to(h8[SUBLANES - 1:SUBLANES, :], (SUBLANES, C))

    hc_ref[...] = lax.fori_loop(0, ts // SUBLANES, group, hc_ref[...])


OD_QKVR = 3072
OD_GLR = (3072, 3200)
OD_COLS = 3200


def _odd_proj_kernel(h_ref, g_ref, w_ref, wg2_ref, bg_ref, q_ref, k_ref, v_ref, r_ref, gk_ref):
    xn = _rms(h_ref[...], g_ref[...]).astype(BF16)

    def seg(a, b):
        return _dot(xn, w_ref[:, a:b])

    q_ref[...] = seg(0, GLA_DK) * (GLA_DKH ** -0.5)
    k_ref[...] = seg(GLA_DK, 2 * GLA_DK)
    v_ref[...] = seg(2 * GLA_DK, 2 * GLA_DK + GLA_DV).astype(BF16)
    r_ref[...] = seg(2 * GLA_DK + GLA_DV, OD_QKVR)
    glr = seg(*OD_GLR).astype(BF16)
    z = _dot(glr, wg2_ref[...]) + bg_ref[...]
    gk_ref[...] = (-_softplus(-z)) * (1.0 / GLA_TAU)


def _odd_proj(h, g_pre, w_in, w_g2, b_g):
    T, D = h.shape
    tm = min(PROJ_TM, T)
    row = lambda i: (i, 0)
    fixed = lambda i: (0, 0)
    outs = [(GLA_DK, F32), (GLA_DK, F32), (GLA_DV, BF16), (GLA_DV, F32), (GLA_DK, F32)]
    return pl.pallas_call(
        _odd_proj_kernel,
        grid=(T // tm,),
        in_specs=[
            pl.BlockSpec((tm, D), row),
            pl.BlockSpec((1, D), fixed),
            _resident(w_in),
            pl.BlockSpec(w_g2.shape, fixed),
            pl.BlockSpec((1, GLA_DK), fixed),
        ],
        out_specs=[pl.BlockSpec((tm, n), row) for n, _ in outs],
        out_shape=[jax.ShapeDtypeStruct((T, n), dt) for n, dt in outs],
        compiler_params=_cparams(("parallel",)),
        name="odd_proj",
    )(h, g_pre, w_in, w_g2, b_g)


GLA_TS = 256


def _gla_kernel(q_ref, k_ref, v_ref, gk_ref, r_ref, hn_ref, tri_ref, o_ref,
                st_ref, qd_ref, oi_ref, u_ref, stb_ref):
    ts = q_ref.shape[1]
    C = GLA_CHUNK
    nch = ts // C

    @pl.when(pl.program_id(1) == 0)
    def _():
        st_ref[...] = jnp.zeros(st_ref.shape, F32)

    gk = gk_ref[0]
    g_hi = gk.astype(BF16)
    rem = gk - g_hi.astype(F32)
    g_mid = rem.astype(BF16)
    g_lo = (rem - g_mid.astype(F32)).astype(BF16)
    tri = tri_ref[...]
    G = _dot(tri, g_hi) + _dot(tri, g_mid) + _dot(tri, g_lo)

    kf = k_ref[0]
    qd_ref[...] = (q_ref[0] * jnp.exp(G)).astype(BF16)
    k_inv = (kf * jnp.exp(-G)).astype(BF16)
    g_last = [G[(c + 1) * C - 1:(c + 1) * C, :] for c in range(nch)]
    k_rem = jnp.concatenate(
        [kf[c * C:(c + 1) * C, :] * jnp.exp(g_last[c] - G[c * C:(c + 1) * C, :]) for c in range(nch)],
        axis=0).astype(BF16)

    ri = lax.broadcasted_iota(jnp.int32, (ts, ts), 0)
    ci = lax.broadcasted_iota(jnp.int32, (ts, ts), 1)
    same_chunk_causal = (ri >= ci) & (ri // C == ci // C)
    for h in range(GLA_HEADS):
        ksl = slice(h * GLA_DKH, (h + 1) * GLA_DKH)
        vsl = slice(h * GLA_DVH, (h + 1) * GLA_DVH)
        att = jnp.where(same_chunk_causal, _dot_nt(qd_ref[:, ksl], k_inv[:, ksl]), 0.0).astype(BF16)
        oi_ref[:, vsl] = _dot(att, v_ref[0, :, vsl])

    for c in range(nch):
        rows = slice(c * C, (c + 1) * C)
        for h in range(GLA_HEADS):
            ksl = slice(h * GLA_DKH, (h + 1) * GLA_DKH)
            vsl = slice(h * GLA_DVH, (h + 1) * GLA_DVH)
            u_ref[c, h] = _dot_tn(v_ref[0, rows, vsl], k_rem[rows, ksl])

    for h in range(GLA_HEADS):
        ksl = slice(h * GLA_DKH, (h + 1) * GLA_DKH)
        st = st_ref[h]
        for c in range(nch):
            stb_ref[c, h] = st.astype(BF16)
            st = st * jnp.exp(g_last[c][:, ksl]) + u_ref[c, h]
        st_ref[h] = st

    for c in range(nch):
        rows = slice(c * C, (c + 1) * C)
        for h in range(GLA_HEADS):
            ksl = slice(h * GLA_DKH, (h + 1) * GLA_DKH)
            vsl = slice(h * GLA_DVH, (h + 1) * GLA_DVH)
            o = oi_ref[rows, vsl] + _dot_nt(qd_ref[rows, ksl], stb_ref[c, h])
            on = _rms(o, hn_ref[...])
            rr = r_ref[0, rows, vsl]
            o_ref[0, rows, vsl] = (on * (rr * _sigmoid(rr))).astype(o_ref.dtype)


def _gla(q, k, v, gk, r, head_norm):
    B, S, _ = q.shape
    ts = min(GLA_TS, S)
    nch = ts // GLA_CHUNK
    pos = jnp.arange(ts)
    tri = ((pos[:, None] >= pos[None, :])
           & (pos[:, None] // GLA_CHUNK == pos[None, :] // GLA_CHUNK)).astype(BF16)
    blk = lambda b, s: (b, s, 0)
    return pl.pallas_call(
        _gla_kernel,
        grid=(B, S // ts),
        in_specs=[
            pl.BlockSpec((1, ts, GLA_DK), blk),
            pl.BlockSpec((1, ts, GLA_DK), blk),
            pl.BlockSpec((1, ts, GLA_DV), blk),
            pl.BlockSpec((1, ts, GLA_DK), blk),
            pl.BlockSpec((1, ts, GLA_DV), blk),
            pl.BlockSpec((1, GLA_DVH), lambda b, s: (0, 0)),
            pl.BlockSpec((ts, ts), lambda b, s: (0, 0)),
        ],
        out_specs=pl.BlockSpec((1, ts, GLA_DV), blk),
        out_shape=jax.ShapeDtypeStruct((B, S, GLA_DV), BF16),
        scratch_shapes=[
            pltpu.VMEM((GLA_HEADS, GLA_DVH, GLA_DKH), F32),
            pltpu.VMEM((ts, GLA_DK), BF16),
            pltpu.VMEM((ts, GLA_DV), F32),
            pltpu.VMEM((nch, GLA_HEADS, GLA_DVH, GLA_DKH), F32),
            pltpu.VMEM((nch, GLA_HEADS, GLA_DVH, GLA_DKH), BF16),
        ],
        compiler_params=_cparams(("parallel", "arbitrary")),
        name="gla",
    )(q, k, v, gk, r, head_norm, tri)


def _xa_kv_kernel(mem_ref, g_ref, w_ref, k_ref, v_ref):
    mn = _rms(mem_ref[...], g_ref[...]).astype(BF16)
    k_ref[...] = (_dot(mn, w_ref[:, :D_MODEL].astype(BF16)) * (XA_HEAD_DIM ** -0.5)).astype(BF16)
    v_ref[...] = _dot(mn, w_ref[:, D_MODEL:].astype(BF16)).astype(BF16)


def _xa_kv(mem, g_mem, w_kv, lead):
    B, M, D = mem.shape
    whole = lambda i: (0, 0)
    k, v = pl.pallas_call(
        _xa_kv_kernel,
        grid=(1,),
        in_specs=[pl.BlockSpec((B * M, D), whole), pl.BlockSpec((1, D), whole), _resident(w_kv, lead)],
        out_specs=[pl.BlockSpec((B * M, D), whole), pl.BlockSpec((B * M, D), whole)],
        out_shape=[jax.ShapeDtypeStruct((B * M, D), BF16)] * 2,
        compiler_params=_cparams(("arbitrary",)),
        name="xa_kv",
    )(mem.reshape(B * M, D), g_mem, w_kv)
    return k.reshape(B, M, D), v.reshape(B, M, D)


XA_TM = 1024


def _mix_out_xa_kernel(*refs, offsets):
    n = len(offsets)
    h_ref, gmix_ref, gpre_ref, gpost_ref = refs[:4]
    part_refs = refs[4:4 + n]
    wout_ref, wq_ref, k_ref, v_ref, wo_ref, o_ref = refs[4 + n:]
    m = None
    for p_ref, off in zip(part_refs, offsets):
        kk = p_ref.shape[-1]
        term = _dot(p_ref[0], wout_ref[off:off + kk, :])
        m = term if m is None else m + term
    x = h_ref[0] + _rms(m, gmix_ref[...])
    xn = _rms(x, gpre_ref[...]).astype(BF16)
    q = _dot(xn, wq_ref[...]).astype(BF16)
    heads = []
    for h in range(XA_HEADS):
        sl = slice(h * XA_HEAD_DIM, (h + 1) * XA_HEAD_DIM)
        s = _dot_nt(q[:, sl], k_ref[0, :, sl])
        p = jnp.exp(s - jnp.max(s, axis=-1, keepdims=True))
        oh = _dot(p.astype(BF16), v_ref[0, :, sl]) / jnp.sum(p, axis=-1, keepdims=True)
        heads.append(oh.astype(BF16))
    c = _dot(jnp.concatenate(heads, axis=-1), wo_ref[...])
    o_ref[0] = x + _rms(c, gpost_ref[...])


def _mix_out_xa(h, g_mix, g_pre, g_post, parts, w_out, w_q, kx, vx, w_o):
    B, S, D = h.shape
    M = kx.shape[1]
    tm = min(XA_TM, S)
    blk = lambda b, i: (b, i, 0)
    fixed = lambda b, i: (0, 0)
    offsets, off = [], 0
    for p in parts:
        offsets.append(off)
        off += p.shape[-1]
    return pl.pallas_call(
        functools.partial(_mix_out_xa_kernel, offsets=tuple(offsets)),
        grid=(B, S // tm),
        in_specs=[
            pl.BlockSpec((1, tm, D), blk),
            pl.BlockSpec((1, D), fixed),
            pl.BlockSpec((1, D), fixed),
            pl.BlockSpec((1, D), fixed),
            *[pl.BlockSpec((1, tm, p.shape[-1]), blk) for p in parts],
            _resident(w_out),
            _resident(w_q),
            pl.BlockSpec((1, M, D), lambda b, i: (b, 0, 0)),
            pl.BlockSpec((1, M, D), lambda b, i: (b, 0, 0)),
            _resident(w_o),
        ],
        out_specs=pl.BlockSpec((1, tm, D), blk),
        out_shape=jax.ShapeDtypeStruct((B, S, D), F32),
        compiler_params=_cparams(("parallel", "parallel")),
        name="mix_out_xa",
    )(h, g_mix, g_pre, g_post, *parts, w_out, w_q, kx, vx, w_o)


def _block_diag(w):
    G, n, _ = w.shape
    eye = jnp.eye(G, dtype=w.dtype)
    return (eye[:, None, :, None] * w[:, :, None, :]).reshape(G * n, G * n)


def _even_w_in(w):
    ki = w[:, 1280:1344]
    pad = jnp.zeros((w.shape[0], LANES - IDX_HEADS), w.dtype)
    return jnp.concatenate([w[:, :1280], ki, ki, w[:, 1352:2376], w[:, 1344:1352], pad], axis=1).astype(BF16)


def _odd_w_in(w):
    pad = jnp.zeros((w.shape[0], LANES - GLA_GATE_RANK), w.dtype)
    return jnp.concatenate([w[:, :2048], w[:, 2064:3088], w[:, 2048:2064], pad], axis=1).astype(BF16)


def kernel(x, mem, norms, ffn_w_gu, ffn_w_down, xa_w_q, xa_w_kv, xa_w_o, ev_w_in, ev_kv_norm, ev_w_uk, ev_w_uv, ev_conv_w, ev_conv_b, ev_w_ra, ev_b_ra, ev_w_ri, ev_b_ri, ev_lam, ev_w_out, od_w_in, od_w_g2, od_b_g, od_head_norm, od_w_out):
    B, S, D = x.shape
    T = B * S
    depth = norms.shape[0]
    h = x.reshape(T, D)

    def gain(layer, idx):
        return norms[layer, idx][None, :]

    ffn_order = [(layer, j) for layer in range(depth) for j in range(2)]
    ffn_w = [ffn_w_gu[0, 0].astype(BF16), ffn_w_down[0, 0].astype(BF16)]

    def ffn(h, ffn_w, layer, j, g_pre, g_post, extra=()):
        k = ffn_order.index((layer, j))
        casts = list(extra)
        if k + 1 < len(ffn_order):
            casts = [(ffn_w_gu, ffn_order[k + 1]), (ffn_w_down, ffn_order[k + 1])] + casts
        h, cast = _ffn(h, g_pre, g_post, *ffn_w, side_casts=casts)
        n_next = len(casts) - len(extra)
        return h, cast[:n_next], cast[n_next:]

    for layer in range(depth):
        mix_out = (ev_w_out, (layer // 2,)) if layer % 2 == 0 else (od_w_out, (layer // 2,))
        h, ffn_w, (w_out, w_q, w_o) = ffn(
            h, ffn_w, layer, 0, gain(layer, N_FFN1_PRE), gain(layer, N_FFN1_POST),
            extra=[mix_out, (xa_w_q, (layer,)), (xa_w_o, (layer,))])

        if layer % 2 == 0:
            e = layer // 2
            lru_params = (ev_conv_w[e], ev_conv_b[e][None, :],
                          _block_diag(ev_w_ra[e]).astype(BF16), ev_b_ra[e].reshape(1, B_WIDTH),
                          _block_diag(ev_w_ri[e]).astype(BF16), ev_b_ri[e].reshape(1, B_WIDTH),
                          ev_lam[e][None, :])
            qt, k, vt, qit, ki2, wit, b_out = _even_proj(
                h, gain(layer, N_MIX_PRE), _even_w_in(ev_w_in[e]), ev_kv_norm[e][None, :],
                ev_w_uk[e].astype(BF16), ev_w_uv[e].astype(BF16), lru_params, S)
            a_out = _dsa(qt, k, vt, qit, ki2, wit, B)
            parts = [a_out.reshape(B, S, A_WIDTH), b_out.reshape(B, S, B_WIDTH)]
        else:
            o = layer // 2
            w_g2 = jnp.concatenate(
                [od_w_g2[o], jnp.zeros((LANES - GLA_GATE_RANK, GLA_DK), od_w_g2.dtype)], axis=0).astype(BF16)
            q, k, v, r, gk = _odd_proj(h, gain(layer, N_MIX_PRE), _odd_w_in(od_w_in[o]), w_g2,
                                       od_b_g[o][None, :])
            r3 = lambda a: a.reshape(B, S, a.shape[-1])
            g_out = _gla(r3(q), r3(k), r3(v), r3(gk), r3(r), od_head_norm[o][None, :])
            parts = [g_out]

        kx, vx = _xa_kv(mem, gain(layer, N_MEM_NORM), xa_w_kv, (layer,))
        h = _mix_out_xa(h.reshape(B, S, D), gain(layer, N_MIX_POST), gain(layer, N_XA_PRE),
                        gain(layer, N_XA_POST), parts, w_out, w_q, kx, vx, w_o).reshape(T, D)

        h, ffn_w, _ = ffn(h, ffn_w, layer, 1, gain(layer, N_FFN2_PRE), gain(layer, N_FFN2_POST))
    return h.reshape(B, S, D)
```

```python
import functools

import jax
import jax.numpy as jnp
from jax import lax
from jax.experimental import pallas as pl
from jax.experimental.pallas import tpu as pltpu

F32 = jnp.float32
BF16 = jnp.bfloat16

EPS = 1e-6
D_MODEL = 1024
D_FF = 2816
XA_HEADS = 4
XA_HEAD_DIM = D_MODEL // XA_HEADS
A_HEADS = 8
A_HEAD_DIM = 64
A_WIDTH = A_HEADS * A_HEAD_DIM
KV_RANK = 256
IDX_HEADS = 8
IDX_DIM = 64
TOPK_MAX = 256
B_WIDTH = D_MODEL - A_WIDTH
B_BLOCKS = 8
B_BLOCK_DIM = B_WIDTH // B_BLOCKS
CONV_W = 4
LRU_C = 8.0
GLA_HEADS = 4
GLA_DK = D_MODEL // 2
GLA_DV = D_MODEL
GLA_DKH = GLA_DK // GLA_HEADS
GLA_DVH = GLA_DV // GLA_HEADS
GLA_GATE_RANK = 16
GLA_TAU = 16.0
GLA_CHUNK = 64
(N_FFN1_PRE, N_FFN1_POST, N_MIX_PRE, N_MIX_POST, N_XA_PRE, N_XA_POST, N_MEM_NORM,
 N_FFN2_PRE, N_FFN2_POST) = range(9)

LANES = 128
SUBLANES = 8
MXU_COLS = 256
VMEM_LIMIT = 48 * 1024 * 1024

NEG_BIG = -1e30
LOG2E = 1.4426950408889634
INT_MIN = -2 ** 31
POS_INF_CODE = 0x7F800000
NEG_INF_CODE = -0x7F800001


def _cparams(sem):
    return pltpu.CompilerParams(dimension_semantics=sem, vmem_limit_bytes=VMEM_LIMIT)


def _rms(x, g):
    return x * lax.rsqrt(jnp.mean(x * x, axis=-1, keepdims=True) + EPS) * g


def _dot(a, b):
    return jnp.dot(a, b, preferred_element_type=F32)


def _dot_nt(a, b):
    return lax.dot_general(a, b, (((1,), (1,)), ((), ())), preferred_element_type=F32)


def _dot_tn(a, b):
    return lax.dot_general(a, b, (((0,), (0,)), ((), ())), preferred_element_type=F32)


def _sigmoid(x):
    return 1.0 / (1.0 + jnp.exp(-x))


FFN_TM = 1024
FFN_SLAB = 512
FFN_TF = 256


def _ffn_kernel(*refs, n_casts):
    h_ref, gpre_ref, gpost_ref, wgu_ref, wd_ref = refs[:5]
    cast_in = refs[5:5 + n_casts]
    o_ref = refs[5 + n_casts]
    cast_out = refs[6 + n_casts:6 + 2 * n_casts]
    act_ref = refs[6 + 2 * n_casts]
    for src_ref, dst_ref in zip(cast_in, cast_out):
        dst_ref[...] = src_ref[...].astype(BF16)
    F = wd_ref.shape[0]
    for r0 in range(0, h_ref.shape[0], FFN_SLAB):
        rows = slice(r0, r0 + FFN_SLAB)
        x = h_ref[rows, :]
        xn = _rms(x, gpre_ref[...]).astype(BF16)
        for c in range(F // FFN_TF):
            g = _dot(xn, wgu_ref[:, c * FFN_TF:(c + 1) * FFN_TF])
            u = _dot(xn, wgu_ref[:, F + c * FFN_TF:F + (c + 1) * FFN_TF])
            act_ref[rows, c * FFN_TF:(c + 1) * FFN_TF] = (g * _sigmoid(g) * u).astype(BF16)
        f = _dot(act_ref[rows, :], wd_ref[...])
        o_ref[rows, :] = x + 0.5 * _rms(f, gpost_ref[...])


def _resident(arr, lead=()):
    tail = arr.shape[len(lead):]
    index = tuple(lead) + (0,) * len(tail)
    return pl.BlockSpec((None,) * len(lead) + tail, lambda *_: index, pipeline_mode=pl.Buffered(1))


def _ffn(h, g_pre, g_post, w_gu, w_down, side_casts=()):
    T, D = h.shape
    F = w_down.shape[0]
    tm = min(FFN_TM, T)
    steps = T // tm
    in_specs = [
        pl.BlockSpec((tm, D), lambda i: (i, 0)),
        pl.BlockSpec((1, D), lambda i: (0, 0)),
        pl.BlockSpec((1, D), lambda i: (0, 0)),
        _resident(w_gu),
        _resident(w_down),
    ]
    out_specs = [pl.BlockSpec((tm, D), lambda i: (i, 0))]
    out_shape = [jax.ShapeDtypeStruct((T, D), F32)]
    args = [h, g_pre, g_post, w_gu, w_down]
    cast_specs = []
    for w_all, lead in side_casts:
        rows, cols = w_all.shape[-2:]
        slab = rows // steps
        assert slab * steps == rows and slab % 16 == 0, (rows, steps)
        cast_specs.append(pl.BlockSpec((None,) * len(lead) + (slab, cols),
                                       lambda i, lead=tuple(lead): lead + (i, 0)))
        out_specs.append(pl.BlockSpec((slab, cols), lambda i: (i, 0)))
        out_shape.append(jax.ShapeDtypeStruct((rows, cols), BF16))
        args.append(w_all)
    outs = pl.pallas_call(
        functools.partial(_ffn_kernel, n_casts=len(side_casts)),
        grid=(steps,),
        in_specs=in_specs + cast_specs,
        out_specs=out_specs,
        out_shape=out_shape,
        scratch_shapes=[pltpu.VMEM((tm, F), BF16)],
        compiler_params=_cparams(("parallel",)),
        name="ffn",
    )(*args)
    return outs[0], list(outs[1:])


PROJ_TM = 1024


EV_Q = (0, 512)
EV_CKV = (512, 768)
EV_QI = (768, 1280)
EV_KI2 = (1280, 1408)
EV_GATE = (1408, 1920)
EV_XB = (1920, 2432)
EV_WI = (2432, 2560)
EV_COLS = 2560


DSA_TQ = 512
DSA_CK = 512


def _even_proj_kernel(h_ref, g_ref, w_ref, kvn_ref, wuk_ref, wuv_ref,
                      cw_ref, cb_ref, wra_ref, bra_ref, wri_ref, bri_ref, lam_ref,
                      qt_ref, k_ref, vt_ref, qit_ref, ki_ref, wit_ref, bout_ref,
                      xs_ref, a_ref, b_ref, hc_ref, gg_ref, *, tiles_per_seq):
    _rglru_reset(pl.program_id(0) % tiles_per_seq == 0, xs_ref, hc_ref)
    xn = _rms(h_ref[...], g_ref[...]).astype(BF16)

    def seg(ab):
        return _dot(xn, w_ref[:, ab[0]:ab[1]])

    tm = h_ref.shape[0]
    lru_slab, lru_finish = _rglru_gates(seg(EV_XB), cw_ref, cb_ref, wra_ref, bra_ref, wri_ref, bri_ref,
                                        lam_ref, xs_ref, a_ref, b_ref)
    n_slab = tm // LRU_SLABS

    def p_gate():
        gg_ref[...] = _gelu_tanh(seg(EV_GATE))

    def p_q():
        qt_ref[...] = (seg(EV_Q) * (A_HEAD_DIM ** -0.5 * LOG2E)).T.astype(BF16)

    ckv = []

    def p_k():
        ckv.append(_rms(seg(EV_CKV), kvn_ref[...]).astype(BF16))
        k_ref[...] = _dot(ckv[0], wuk_ref[...]).astype(BF16)

    def p_v():
        v = _dot(ckv[0], wuv_ref[...]).astype(BF16)
        for c in range(vt_ref.shape[0]):
            vt_ref[c] = v[c * DSA_CK:(c + 1) * DSA_CK, :].T

    def p_qi():
        qit_ref[...] = (seg(EV_QI) * (IDX_DIM ** -0.5)).T.astype(BF16)

    def p_ki():
        ki_ref[...] = seg(EV_KI2).astype(BF16)

    def p_wi():
        wit_ref[...] = (seg(EV_WI) * (IDX_HEADS ** -0.5)).T[:IDX_HEADS, :]

    pieces = (p_gate, p_q, p_k, p_v, p_qi, p_ki, p_wi)
    for s in range(LRU_SLABS):
        lru_slab(s * n_slab, n_slab)
        if s < len(pieces):
            pieces[s]()
    lru_finish()
    _rglru_scan(a_ref, b_ref, hc_ref)
    bout_ref[...] = (a_ref[...] * gg_ref[...]).astype(bout_ref.dtype)


def _even_proj(h, g_pre, w_in, kv_norm, w_uk, w_uv, lru_params, seq_len):
    T, D = h.shape
    tm = min(PROJ_TM, seq_len)
    C = B_WIDTH
    row = lambda i: (i, 0)
    col = lambda i: (0, i)
    fixed = lambda i: (0, 0)
    out_specs = [
        pl.BlockSpec((A_WIDTH, tm), col),
        pl.BlockSpec((tm, A_WIDTH), row),
        pl.BlockSpec((tm // DSA_CK, A_WIDTH, DSA_CK), lambda i: (i, 0, 0)),
        pl.BlockSpec((IDX_HEADS * IDX_DIM, tm), col),
        pl.BlockSpec((tm, 2 * IDX_DIM), row),
        pl.BlockSpec((IDX_HEADS, tm), col),
        pl.BlockSpec((tm, C), row),
    ]
    out_shape = [
        jax.ShapeDtypeStruct((A_WIDTH, T), BF16),
        jax.ShapeDtypeStruct((T, A_WIDTH), BF16),
        jax.ShapeDtypeStruct((T // DSA_CK, A_WIDTH, DSA_CK), BF16),
        jax.ShapeDtypeStruct((IDX_HEADS * IDX_DIM, T), BF16),
        jax.ShapeDtypeStruct((T, 2 * IDX_DIM), BF16),
        jax.ShapeDtypeStruct((IDX_HEADS, T), F32),
        jax.ShapeDtypeStruct((T, C), BF16),
    ]
    return pl.pallas_call(
        functools.partial(_even_proj_kernel, tiles_per_seq=seq_len // tm),
        grid=(T // tm,),
        in_specs=[
            pl.BlockSpec((tm, D), row),
            pl.BlockSpec((1, D), fixed),
            _resident(w_in),
            pl.BlockSpec((1, KV_RANK), fixed),
            pl.BlockSpec(w_uk.shape, fixed),
            pl.BlockSpec(w_uv.shape, fixed),
            *[pl.BlockSpec(p.shape, fixed) for p in lru_params],
        ],
        out_specs=out_specs,
        out_shape=out_shape,
        scratch_shapes=[
            pltpu.VMEM((HALO + tm, C), F32),
            pltpu.VMEM((tm, C), F32),
            pltpu.VMEM((tm, C), F32),
            pltpu.VMEM((SUBLANES, C), F32),
            pltpu.VMEM((tm, C), F32),
        ],
        compiler_params=_cparams(("arbitrary",)),
        name="even_proj",
    )(h, g_pre, w_in, kv_norm, w_uk, w_uv, *lru_params)


ACC_ROWS = 4 * SUBLANES
ACC16_ROWS = 2 * ACC_ROWS
SEARCH_PASSES = 18


def _col_partial(x, op):
    rows, n = x.shape
    part = x.reshape(rows // ACC_ROWS, ACC_ROWS, n)
    return jnp.max(part, axis=0) if op == "max" else jnp.sum(part, axis=0)


def _col_reduce(x, op):
    part = _col_partial(x, op)
    return (jnp.max(part, axis=0, keepdims=True) if op == "max"
            else jnp.sum(part, axis=0, keepdims=True))


def _dsa_kernel(qt_ref, qit_ref, wit_ref, k_ref, vt_ref, ki_ref, tri_ref, o_ref,
                sc_ref, hb_ref, bias_ref, s_ref, qm_ref, qim_ref, m_ref, l_ref, acc_ref, need_ref, seen_ref,
                *, top_k):
    TQ, CK = DSA_TQ, DSA_CK
    j = pl.program_id(1)
    q0 = j * TQ
    nkc = (q0 + TQ + CK - 1) // CK

    low_half = lax.broadcasted_iota(jnp.int32, (LANES, TQ), 0) < A_HEAD_DIM
    for h in range(A_HEADS):
        pr = slice((h // 2) * LANES, (h // 2 + 1) * LANES)
        keep = low_half if h % 2 == 0 else jnp.logical_not(low_half)
        qm_ref[h] = jnp.where(keep, qt_ref[pr, :], jnp.zeros((), BF16))
        qim_ref[h] = jnp.where(keep, qit_ref[pr, :], jnp.zeros((), BF16))

    key_iota = lax.broadcasted_iota(jnp.int32, (CK, TQ), 0)
    q_pos = q0 + lax.broadcasted_iota(jnp.int32, (CK, TQ), 1)

    def score_chunk(c, carry):
        base = pl.multiple_of(c * CK, CK)
        kic = ki_ref[pl.ds(base, CK), :]
        acc = jnp.zeros((CK, TQ), F32)
        for h in range(IDX_HEADS):
            acc = acc + jnp.maximum(_dot(kic, qim_ref[h]), 0.0) * wit_ref[h:h + 1, :]
        sc = jnp.where(base + key_iota <= q_pos, acc, -jnp.inf)
        sc_ref[c] = sc
        hb_ref[c] = sc.astype(BF16)
        return carry

    lax.fori_loop(0, nkc, score_chunk, 0)

    def count(pred_fn):
        def body(c, cnt):
            for g in range(CK // ACC_ROWS):
                hit = pred_fn(sc_ref[c, g * ACC_ROWS:(g + 1) * ACC_ROWS, :])
                cnt = jnp.where(hit, cnt + 1, cnt)
            return cnt
        cnt = lax.fori_loop(0, nkc, body, jnp.zeros((ACC_ROWS, TQ), jnp.int32))
        return jnp.sum(cnt, axis=0, keepdims=True)

    def code_to_float(code):
        code = jnp.clip(code, NEG_INF_CODE, POS_INF_CODE)
        return lax.bitcast_convert_type(code ^ ((code >> 31) & jnp.int32(0x7FFFFFFF)), F32)

    def count16(cand_b):
        def body(c, cnt):
            for g in range(CK // ACC16_ROWS):
                blk = hb_ref[c, g * ACC16_ROWS:(g + 1) * ACC16_ROWS, :]
                cnt = jnp.where(blk >= cand_b, cnt + jnp.ones((), BF16), cnt)
            return cnt
        cnt = lax.fori_loop(0, nkc, body, jnp.zeros((ACC16_ROWS, TQ), BF16))
        return jnp.sum(cnt.astype(F32), axis=0, keepdims=True)

    def bit_pass16(i, prefix):
        cand = prefix + lax.shift_left(jnp.int32(1), 31 - i)
        grid = jnp.where(cand < 0, cand | jnp.int32(0xFFFF), cand)
        cand_b = jnp.broadcast_to(code_to_float(grid), (ACC16_ROWS, TQ)).astype(BF16)
        return jnp.where(count16(cand_b) >= top_k, cand, prefix)

    prefix = lax.fori_loop(0, 16, bit_pass16, jnp.full((1, TQ), INT_MIN, jnp.int32))

    def search_pass(i, carry):
        lo, hi, n_at = carry
        mid = lo + ((hi - lo) >> 1)
        mid_f = code_to_float(mid)
        n_ge = count(lambda sc: sc >= mid_f)
        take = n_ge >= top_k
        return jnp.where(take, mid, lo), jnp.where(take, hi, mid), jnp.where(take, n_ge, n_at)

    lo0 = jnp.maximum(prefix, INT_MIN + 0x10000) - 0x10000
    hi0 = jnp.minimum(prefix, 0x7FFD0000) + 0x20000
    code, _, n_at = lax.fori_loop(
        0, SEARCH_PASSES, search_pass, (lo0, hi0, jnp.full((1, TQ), nkc * CK, jnp.int32)))
    thr = code_to_float(code)
    finite = thr > -jnp.inf
    has_ties = jnp.max(jnp.where(finite & (n_at > top_k), 1, 0)) > 0
    thr_sel = jnp.where(finite, thr, jnp.finfo(F32).min)

    m_ref[...] = jnp.full(m_ref.shape, -jnp.inf, F32)
    l_ref[...] = jnp.zeros(l_ref.shape, F32)
    acc_ref[...] = jnp.zeros(acc_ref.shape, F32)
    need_ref[...] = jnp.zeros(need_ref.shape, F32)
    seen_ref[...] = jnp.zeros(seen_ref.shape, F32)

    @pl.when(has_ties)
    def _():
        n_gt = count(lambda sc: sc > thr)
        need_ref[...] = jnp.where(finite, (top_k - n_gt).astype(F32), 0.0)

    def attend_chunk(c, carry):
        base = pl.multiple_of(c * CK, CK)

        @pl.when(jnp.logical_not(has_ties))
        def _():
            bias_ref[...] = jnp.where(sc_ref[c] >= thr_sel, 0.0, NEG_BIG)

        @pl.when(has_ties)
        def _():
            sc = sc_ref[c]
            eq = sc == thr
            rank = seen_ref[...] + _dot(tri_ref[...], eq.astype(BF16))
            sel = (sc > thr) | (eq & (rank <= need_ref[...]))
            bias_ref[...] = jnp.where(sel, 0.0, NEG_BIG)
            seen_ref[...] += _col_reduce(eq.astype(F32), "sum")

        cmax = []
        for h in range(A_HEADS):
            kp = k_ref[pl.ds(base, CK), (h // 2) * LANES:(h // 2 + 1) * LANES]
            s = _dot(kp, qm_ref[h]) + bias_ref[...]
            s_ref[h] = s
            cmax.append(_col_reduce(s, "max"))
        for h in range(A_HEADS):
            vth = vt_ref[c, h * A_HEAD_DIM:(h + 1) * A_HEAD_DIM, :]
            for lane0 in range(0, TQ, MXU_COLS):
                qs = slice(lane0, lane0 + MXU_COLS)
                m_old = m_ref[h, :, qs]
                m_new = jnp.maximum(m_old, cmax[h][:, qs])
                alpha = jnp.exp2(m_old - m_new)
                p = jnp.exp2(s_ref[h, :, qs] - m_new)
                l_ref[h, :, qs] = alpha * l_ref[h, :, qs] + _col_reduce(p, "sum")
                m_ref[h, :, qs] = m_new
                acc_ref[h, :, qs] = acc_ref[h, :, qs] * alpha + _dot(vth, p.astype(BF16))
        return carry

    lax.fori_loop(0, nkc, attend_chunk, 0)

    out_t = jnp.concatenate([acc_ref[h] / l_ref[h] for h in range(A_HEADS)], axis=0)
    o_ref[...] = out_t.T.astype(o_ref.dtype)


def _dsa(qt, k, vt, qit, ki2, wit, B):
    T = k.shape[0]
    S = T // B
    TQ, CK = DSA_TQ, DSA_CK
    nc, nq = S // CK, S // TQ
    top_k = min(TOPK_MAX, S // 4)
    tri = (jnp.arange(CK)[:, None] >= jnp.arange(CK)[None, :]).astype(BF16)
    qcol = lambda b, j: (0, b * nq + j)
    return pl.pallas_call(
        functools.partial(_dsa_kernel, top_k=top_k),
        grid=(B, nq),
        in_specs=[
            pl.BlockSpec((A_WIDTH, TQ), qcol),
            pl.BlockSpec((IDX_HEADS * IDX_DIM, TQ), qcol),
            pl.BlockSpec((IDX_HEADS, TQ), qcol),
            pl.BlockSpec((S, A_WIDTH), lambda b, j: (b, 0)),
            pl.BlockSpec((nc, A_WIDTH, CK), lambda b, j: (b, 0, 0)),
            pl.BlockSpec((S, 2 * IDX_DIM), lambda b, j: (b, 0)),
            pl.BlockSpec((CK, CK), lambda b, j: (0, 0)),
        ],
        out_specs=pl.BlockSpec((TQ, A_WIDTH), lambda b, j: (b * nq + j, 0)),
        out_shape=jax.ShapeDtypeStruct((T, A_WIDTH), BF16),
        scratch_shapes=[
            pltpu.VMEM((nc, CK, TQ), F32),
            pltpu.VMEM((nc, CK, TQ), BF16),
            pltpu.VMEM((CK, TQ), F32),
            pltpu.VMEM((A_HEADS, CK, TQ), F32),
            pltpu.VMEM((A_HEADS, LANES, TQ), BF16),
            pltpu.VMEM((IDX_HEADS, LANES, TQ), BF16),
            pltpu.VMEM((A_HEADS, 1, TQ), F32),
            pltpu.VMEM((A_HEADS, 1, TQ), F32),
            pltpu.VMEM((A_HEADS, A_HEAD_DIM, TQ), F32),
            pltpu.VMEM((1, TQ), F32),
            pltpu.VMEM((1, TQ), F32),
        ],
        compiler_params=_cparams(("parallel", "arbitrary")),
        name="dsa",
    )(qt, qit, wit, k, vt, ki2, tri)


LRU_SLABS = 8
HALO = SUBLANES


def _softplus(x):
    return jnp.maximum(x, 0.0) + jnp.log1p(jnp.exp(-jnp.abs(x)))


def _gelu_tanh(x):
    return 0.5 * x * (1.0 + jnp.tanh(0.7978845608028654 * (x + 0.044715 * (x * x * x))))


def _rglru_reset(first, xs_ref, hc_ref):
    @pl.when(first)
    def _():
        xs_ref[0:HALO, :] = jnp.zeros((HALO, xs_ref.shape[1]), F32)
        hc_ref[...] = jnp.zeros(hc_ref.shape, F32)


def _rglru_gates(x, cw_ref, cb_ref, wra_ref, bra_ref, wri_ref, bri_ref, lam_ref, xs_ref, a_ref, b_ref):
    ts, C = x.shape
    xs_ref[HALO:HALO + ts, :] = x

    def slab(r0, n):
        xc = cb_ref[...] + jnp.zeros((n, C), F32)
        for kk in range(CONV_W):
            off = HALO - (CONV_W - 1) + kk + r0
            xc = xc + cw_ref[kk:kk + 1, :] * xs_ref[off:off + n, :]
        xcb = xc.astype(BF16)
        r = _sigmoid(_dot(xcb, wra_ref[...]) + bra_ref[...])
        gi = _sigmoid(_dot(xcb, wri_ref[...]) + bri_ref[...])
        log_a = (-LRU_C) * r * _softplus(-lam_ref[...])
        a = jnp.exp(log_a)
        a_ref[r0:r0 + n, :] = a
        b_ref[r0:r0 + n, :] = jnp.sqrt(-jnp.tanh(log_a) * (1.0 + a * a)) * (gi * xc)

    def finish():
        xs_ref[0:HALO, :] = xs_ref[ts:ts + HALO, :]

    return slab, finish


def _rglru_scan(a_ref, b_ref, hc_ref):
    ts, C = a_ref.shape
    row = lax.broadcasted_iota(jnp.int32, (SUBLANES, C), 0)

    def group(g, carry):
        r0 = pl.multiple_of(g * SUBLANES, SUBLANES)
        av = a_ref[pl.ds(r0, SUBLANES), :]
        bv = b_ref[pl.ds(r0, SUBLANES), :]
        for sh in (1, 2, 4):
            a_sh = pltpu.roll(av, sh, axis=0)
            b_sh = pltpu.roll(bv, sh, axis=0)
            ok = row >= sh
            bv = jnp.where(ok, av * b_sh + bv, bv)
            av = jnp.where(ok, av * a_sh, av)
        h8 = av * carry + bv
        a_ref[pl.ds(r0, SUBLANES), :] = h8
        return jnp.broadcast_to(h8[SUBLANES - 1:SUBLANES, :], (SUBLANES, C))

    hc_ref[...] = lax.fori_loop(0, ts // SUBLANES, group, hc_ref[...])


OD_QKVR = 3072
OD_GLR = (3072, 3200)
OD_COLS = 3200


def _odd_proj_kernel(h_ref, g_ref, w_ref, wg2_ref, bg_ref, q_ref, k_ref, v_ref, r_ref, gk_ref):
    xn = _rms(h_ref[...], g_ref[...]).astype(BF16)

    def seg(a, b):
        return _dot(xn, w_ref[:, a:b])

    q_ref[...] = seg(0, GLA_DK) * (GLA_DKH ** -0.5)
    k_ref[...] = seg(GLA_DK, 2 * GLA_DK)
    v_ref[...] = seg(2 * GLA_DK, 2 * GLA_DK + GLA_DV).astype(BF16)
    r_ref[...] = seg(2 * GLA_DK + GLA_DV, OD_QKVR)
    glr = seg(*OD_GLR).astype(BF16)
    z = _dot(glr, wg2_ref[...]) + bg_ref[...]
    gk_ref[...] = (-_softplus(-z)) * (1.0 / GLA_TAU)


def _odd_proj(h, g_pre, w_in, w_g2, b_g):
    T, D = h.shape
    tm = min(PROJ_TM, T)
    row = lambda i: (i, 0)
    fixed = lambda i: (0, 0)
    outs = [(GLA_DK, F32), (GLA_DK, F32), (GLA_DV, BF16), (GLA_DV, F32), (GLA_DK, F32)]
    return pl.pallas_call(
        _odd_proj_kernel,
        grid=(T // tm,),
        in_specs=[
            pl.BlockSpec((tm, D), row),
            pl.BlockSpec((1, D), fixed),
            _resident(w_in),
            pl.BlockSpec(w_g2.shape, fixed),
            pl.BlockSpec((1, GLA_DK), fixed),
        ],
        out_specs=[pl.BlockSpec((tm, n), row) for n, _ in outs],
        out_shape=[jax.ShapeDtypeStruct((T, n), dt) for n, dt in outs],
        compiler_params=_cparams(("parallel",)),
        name="odd_proj",
    )(h, g_pre, w_in, w_g2, b_g)


GLA_TS = 256


def _gla_kernel(q_ref, k_ref, v_ref, gk_ref, r_ref, hn_ref, tri_ref, o_ref,
                st_ref, qd_ref, oi_ref, u_ref, stb_ref):
    ts = q_ref.shape[1]
    C = GLA_CHUNK
    nch = ts // C

    @pl.when(pl.program_id(1) == 0)
    def _():
        st_ref[...] = jnp.zeros(st_ref.shape, F32)

    gk = gk_ref[0]
    g_hi = gk.astype(BF16)
    rem = gk - g_hi.astype(F32)
    g_mid = rem.astype(BF16)
    g_lo = (rem - g_mid.astype(F32)).astype(BF16)
    tri = tri_ref[...]
    G = _dot(tri, g_hi) + _dot(tri, g_mid) + _dot(tri, g_lo)

    kf = k_ref[0]
    qd_ref[...] = (q_ref[0] * jnp.exp(G)).astype(BF16)
    k_inv = (kf * jnp.exp(-G)).astype(BF16)
    g_last = [G[(c + 1) * C - 1:(c + 1) * C, :] for c in range(nch)]
    k_rem = jnp.concatenate(
        [kf[c * C:(c + 1) * C, :] * jnp.exp(g_last[c] - G[c * C:(c + 1) * C, :]) for c in range(nch)],
        axis=0).astype(BF16)

    ri = lax.broadcasted_iota(jnp.int32, (ts, ts), 0)
    ci = lax.broadcasted_iota(jnp.int32, (ts, ts), 1)
    same_chunk_causal = (ri >= ci) & (ri // C == ci // C)
    for h in range(GLA_HEADS):
        ksl = slice(h * GLA_DKH, (h + 1) * GLA_DKH)
        vsl = slice(h * GLA_DVH, (h + 1) * GLA_DVH)
        att = jnp.where(same_chunk_causal, _dot_nt(qd_ref[:, ksl], k_inv[:, ksl]), 0.0).astype(BF16)
        oi_ref[:, vsl] = _dot(att, v_ref[0, :, vsl])

    for c in range(nch):
        rows = slice(c * C, (c + 1) * C)
        for h in range(GLA_HEADS):
            ksl = slice(h * GLA_DKH, (h + 1) * GLA_DKH)
            vsl = slice(h * GLA_DVH, (h + 1) * GLA_DVH)
            u_ref[c, h] = _dot_tn(v_ref[0, rows, vsl], k_rem[rows, ksl])

    for h in range(GLA_HEADS):
        ksl = slice(h * GLA_DKH, (h + 1) * GLA_DKH)
        st = st_ref[h]
        for c in range(nch):
            stb_ref[c, h] = st.astype(BF16)
            st = st * jnp.exp(g_last[c][:, ksl]) + u_ref[c, h]
        st_ref[h] = st

    for c in range(nch):
        rows = slice(c * C, (c + 1) * C)
        for h in range(GLA_HEADS):
            ksl = slice(h * GLA_DKH, (h + 1) * GLA_DKH)
            vsl = slice(h * GLA_DVH, (h + 1) * GLA_DVH)
            o = oi_ref[rows, vsl] + _dot_nt(qd_ref[rows, ksl], stb_ref[c, h])
            on = _rms(o, hn_ref[...])
            rr = r_ref[0, rows, vsl]
            o_ref[0, rows, vsl] = (on * (rr * _sigmoid(rr))).astype(o_ref.dtype)


def _gla(q, k, v, gk, r, head_norm):
    B, S, _ = q.shape
    ts = min(GLA_TS, S)
    nch = ts // GLA_CHUNK
    pos = jnp.arange(ts)
    tri = ((pos[:, None] >= pos[None, :])
           & (pos[:, None] // GLA_CHUNK == pos[None, :] // GLA_CHUNK)).astype(BF16)
    blk = lambda b, s: (b, s, 0)
    return pl.pallas_call(
        _gla_kernel,
        grid=(B, S // ts),
        in_specs=[
            pl.BlockSpec((1, ts, GLA_DK), blk),
            pl.BlockSpec((1, ts, GLA_DK), blk),
            pl.BlockSpec((1, ts, GLA_DV), blk),
            pl.BlockSpec((1, ts, GLA_DK), blk),
            pl.BlockSpec((1, ts, GLA_DV), blk),
            pl.BlockSpec((1, GLA_DVH), lambda b, s: (0, 0)),
            pl.BlockSpec((ts, ts), lambda b, s: (0, 0)),
        ],
        out_specs=pl.BlockSpec((1, ts, GLA_DV), blk),
        out_shape=jax.ShapeDtypeStruct((B, S, GLA_DV), BF16),
        scratch_shapes=[
            pltpu.VMEM((GLA_HEADS, GLA_DVH, GLA_DKH), F32),
            pltpu.VMEM((ts, GLA_DK), BF16),
            pltpu.VMEM((ts, GLA_DV), F32),
            pltpu.VMEM((nch, GLA_HEADS, GLA_DVH, GLA_DKH), F32),
            pltpu.VMEM((nch, GLA_HEADS, GLA_DVH, GLA_DKH), BF16),
        ],
        compiler_params=_cparams(("parallel", "arbitrary")),
        name="gla",
    )(q, k, v, gk, r, head_norm, tri)


def _xa_kv_kernel(mem_ref, g_ref, w_ref, k_ref, v_ref):
    mn = _rms(mem_ref[...], g_ref[...]).astype(BF16)
    k_ref[...] = (_dot(mn, w_ref[:, :D_MODEL].astype(BF16)) * (XA_HEAD_DIM ** -0.5)).astype(BF16)
    v_ref[...] = _dot(mn, w_ref[:, D_MODEL:].astype(BF16)).astype(BF16)


def _xa_kv(mem, g_mem, w_kv, lead):
    B, M, D = mem.shape
    whole = lambda i: (0, 0)
    k, v = pl.pallas_call(
        _xa_kv_kernel,
        grid=(1,),
        in_specs=[pl.BlockSpec((B * M, D), whole), pl.BlockSpec((1, D), whole), _resident(w_kv, lead)],
        out_specs=[pl.BlockSpec((B * M, D), whole), pl.BlockSpec((B * M, D), whole)],
        out_shape=[jax.ShapeDtypeStruct((B * M, D), BF16)] * 2,
        compiler_params=_cparams(("arbitrary",)),
        name="xa_kv",
    )(mem.reshape(B * M, D), g_mem, w_kv)
    return k.reshape(B, M, D), v.reshape(B, M, D)


XA_TM = 1024


def _mix_out_xa_kernel(*refs, offsets):
    n = len(offsets)
    h_ref, gmix_ref, gpre_ref, gpost_ref = refs[:4]
    part_refs = refs[4:4 + n]
    wout_ref, wq_ref, k_ref, v_ref, wo_ref, o_ref = refs[4 + n:]
    m = None
    for p_ref, off in zip(part_refs, offsets):
        kk = p_ref.shape[-1]
        term = _dot(p_ref[0], wout_ref[off:off + kk, :])
        m = term if m is None else m + term
    x = h_ref[0] + _rms(m, gmix_ref[...])
    xn = _rms(x, gpre_ref[...]).astype(BF16)
    q = _dot(xn, wq_ref[...]).astype(BF16)
    heads = []
    for h in range(XA_HEADS):
        sl = slice(h * XA_HEAD_DIM, (h + 1) * XA_HEAD_DIM)
        s = _dot_nt(q[:, sl], k_ref[0, :, sl])
        p = jnp.exp(s - jnp.max(s, axis=-1, keepdims=True))
        oh = _dot(p.astype(BF16), v_ref[0, :, sl]) / jnp.sum(p, axis=-1, keepdims=True)
        heads.append(oh.astype(BF16))
    c = _dot(jnp.concatenate(heads, axis=-1), wo_ref[...])
    o_ref[0] = x + _rms(c, gpost_ref[...])


def _mix_out_xa(h, g_mix, g_pre, g_post, parts, w_out, w_q, kx, vx, w_o):
    B, S, D = h.shape
    M = kx.shape[1]
    tm = min(XA_TM, S)
    blk = lambda b, i: (b, i, 0)
    fixed = lambda b, i: (0, 0)
    offsets, off = [], 0
    for p in parts:
        offsets.append(off)
        off += p.shape[-1]
    return pl.pallas_call(
        functools.partial(_mix_out_xa_kernel, offsets=tuple(offsets)),
        grid=(B, S // tm),
        in_specs=[
            pl.BlockSpec((1, tm, D), blk),
            pl.BlockSpec((1, D), fixed),
            pl.BlockSpec((1, D), fixed),
            pl.BlockSpec((1, D), fixed),
            *[pl.BlockSpec((1, tm, p.shape[-1]), blk) for p in parts],
            _resident(w_out),
            _resident(w_q),
            pl.BlockSpec((1, M, D), lambda b, i: (b, 0, 0)),
            pl.BlockSpec((1, M, D), lambda b, i: (b, 0, 0)),
            _resident(w_o),
        ],
        out_specs=pl.BlockSpec((1, tm, D), blk),
        out_shape=jax.ShapeDtypeStruct((B, S, D), F32),
        compiler_params=_cparams(("parallel", "parallel")),
        name="mix_out_xa",
    )(h, g_mix, g_pre, g_post, *parts, w_out, w_q, kx, vx, w_o)


def _block_diag(w):
    G, n, _ = w.shape
    eye = jnp.eye(G, dtype=w.dtype)
    return (eye[:, None, :, None] * w[:, :, None, :]).reshape(G * n, G * n)


def _even_w_in(w):
    ki = w[:, 1280:1344]
    pad = jnp.zeros((w.shape[0], LANES - IDX_HEADS), w.dtype)
    return jnp.concatenate([w[:, :1280], ki, ki, w[:, 1352:2376], w[:, 1344:1352], pad], axis=1).astype(BF16)


def _odd_w_in(w):
    pad = jnp.zeros((w.shape[0], LANES - GLA_GATE_RANK), w.dtype)
    return jnp.concatenate([w[:, :2048], w[:, 2064:3088], w[:, 2048:2064], pad], axis=1).astype(BF16)


def kernel(x, mem, norms, ffn_w_gu, ffn_w_down, xa_w_q, xa_w_kv, xa_w_o, ev_w_in, ev_kv_norm, ev_w_uk, ev_w_uv, ev_conv_w, ev_conv_b, ev_w_ra, ev_b_ra, ev_w_ri, ev_b_ri, ev_lam, ev_w_out, od_w_in, od_w_g2, od_b_g, od_head_norm, od_w_out):
    B, S, D = x.shape
    T = B * S
    depth = norms.shape[0]
    h = x.reshape(T, D)

    def gain(layer, idx):
        return norms[layer, idx][None, :]

    ffn_order = [(layer, j) for layer in range(depth) for j in range(2)]
    ffn_w = [ffn_w_gu[0, 0].astype(BF16), ffn_w_down[0, 0].astype(BF16)]

    def ffn(h, ffn_w, layer, j, g_pre, g_post, extra=()):
        k = ffn_order.index((layer, j))
        casts = list(extra)
        if k + 1 < len(ffn_order):
            casts = [(ffn_w_gu, ffn_order[k + 1]), (ffn_w_down, ffn_order[k + 1])] + casts
        h, cast = _ffn(h, g_pre, g_post, *ffn_w, side_casts=casts)
        n_next = len(casts) - len(extra)
        return h, cast[:n_next], cast[n_next:]

    for layer in range(depth):
        mix_out = (ev_w_out, (layer // 2,)) if layer % 2 == 0 else (od_w_out, (layer // 2,))
        h, ffn_w, (w_out, w_q, w_o) = ffn(
            h, ffn_w, layer, 0, gain(layer, N_FFN1_PRE), gain(layer, N_FFN1_POST),
            extra=[mix_out, (xa_w_q, (layer,)), (xa_w_o, (layer,))])

        if layer % 2 == 0:
            e = layer // 2
            lru_params = (ev_conv_w[e], ev_conv_b[e][None, :],
                          _block_diag(ev_w_ra[e]).astype(BF16), ev_b_ra[e].reshape(1, B_WIDTH),
                          _block_diag(ev_w_ri[e]).astype(BF16), ev_b_ri[e].reshape(1, B_WIDTH),
                          ev_lam[e][None, :])
            qt, k, vt, qit, ki2, wit, b_out = _even_proj(
                h, gain(layer, N_MIX_PRE), _even_w_in(ev_w_in[e]), ev_kv_norm[e][None, :],
                ev_w_uk[e].astype(BF16), ev_w_uv[e].astype(BF16), lru_params, S)
            a_out = _dsa(qt, k, vt, qit, ki2, wit, B)
            parts = [a_out.reshape(B, S, A_WIDTH), b_out.reshape(B, S, B_WIDTH)]
        else:
            o = layer // 2
            w_g2 = jnp.concatenate(
                [od_w_g2[o], jnp.zeros((LANES - GLA_GATE_RANK, GLA_DK), od_w_g2.dtype)], axis=0).astype(BF16)
            q, k, v, r, gk = _odd_proj(h, gain(layer, N_MIX_PRE), _odd_w_in(od_w_in[o]), w_g2,
                                       od_b_g[o][None, :])
            r3 = lambda a: a.reshape(B, S, a.shape[-1])
            g_out = _gla(r3(q), r3(k), r3(v), r3(gk), r3(r), od_head_norm[o][None, :])
            parts = [g_out]

        kx, vx = _xa_kv(mem, gain(layer, N_MEM_NORM), xa_w_kv, (layer,))
        h = _mix_out_xa(h.reshape(B, S, D), gain(layer, N_MIX_POST), gain(layer, N_XA_PRE),
                        gain(layer, N_XA_POST), parts, w_out, w_q, kx, vx, w_o).reshape(T, D)

        h, ffn_w, _ = ffn(h, ffn_w, layer, 1, gain(layer, N_FFN2_PRE), gain(layer, N_FFN2_POST))
    return h.reshape(B, S, D)
```

```python
import functools

import jax
import jax.numpy as jnp
from jax import lax
from jax.experimental import pallas as pl
from jax.experimental.pallas import tpu as pltpu

F32 = jnp.float32
BF16 = jnp.bfloat16

EPS = 1e-6
D_MODEL = 1024
D_FF = 2816
XA_HEADS = 4
XA_HEAD_DIM = D_MODEL // XA_HEADS
A_HEADS = 8
A_HEAD_DIM = 64
A_WIDTH = A_HEADS * A_HEAD_DIM
KV_RANK = 256
IDX_HEADS = 8
IDX_DIM = 64
TOPK_MAX = 256
B_WIDTH = D_MODEL - A_WIDTH
B_BLOCKS = 8
B_BLOCK_DIM = B_WIDTH // B_BLOCKS
CONV_W = 4
LRU_C = 8.0
GLA_HEADS = 4
GLA_DK = D_MODEL // 2
GLA_DV = D_MODEL
GLA_DKH = GLA_DK // GLA_HEADS
GLA_DVH = GLA_DV // GLA_HEADS
GLA_GATE_RANK = 16
GLA_TAU = 16.0
GLA_CHUNK = 64
(N_FFN1_PRE, N_FFN1_POST, N_MIX_PRE, N_MIX_POST, N_XA_PRE, N_XA_POST, N_MEM_NORM,
 N_FFN2_PRE, N_FFN2_POST) = range(9)

LANES = 128
SUBLANES = 8
MXU_COLS = 256
VMEM_LIMIT = 48 * 1024 * 1024

NEG_BIG = -1e30
LOG2E = 1.4426950408889634
INT_MIN = -2 ** 31
POS_INF_CODE = 0x7F800000
NEG_INF_CODE = -0x7F800001


def _cparams(sem):
    return pltpu.CompilerParams(dimension_semantics=sem, vmem_limit_bytes=VMEM_LIMIT)


def _rms(x, g):
    return x * lax.rsqrt(jnp.mean(x * x, axis=-1, keepdims=True) + EPS) * g


def _dot(a, b):
    return jnp.dot(a, b, preferred_element_type=F32)


def _dot_nt(a, b):
    return lax.dot_general(a, b, (((1,), (1,)), ((), ())), preferred_element_type=F32)


def _dot_tn(a, b):
    return lax.dot_general(a, b, (((0,), (0,)), ((), ())), preferred_element_type=F32)


def _sigmoid(x):
    return 1.0 / (1.0 + jnp.exp(-x))


FFN_TM = 1024
FFN_SLAB = 512
FFN_TF = 256


def _ffn_kernel(*refs, n_casts):
    h_ref, gpre_ref, gpost_ref, wgu_ref, wd_ref = refs[:5]
    cast_in = refs[5:5 + n_casts]
    o_ref = refs[5 + n_casts]
    cast_out = refs[6 + n_casts:6 + 2 * n_casts]
    act_ref = refs[6 + 2 * n_casts]
    for src_ref, dst_ref in zip(cast_in, cast_out):
        dst_ref[...] = src_ref[...].astype(BF16)
    F = wd_ref.shape[0]
    for r0 in range(0, h_ref.shape[0], FFN_SLAB):
        rows = slice(r0, r0 + FFN_SLAB)
        x = h_ref[rows, :]
        xn = _rms(x, gpre_ref[...]).astype(BF16)
        for c in range(F // FFN_TF):
            g = _dot(xn, wgu_ref[:, c * FFN_TF:(c + 1) * FFN_TF])
            u = _dot(xn, wgu_ref[:, F + c * FFN_TF:F + (c + 1) * FFN_TF])
            act_ref[rows, c * FFN_TF:(c + 1) * FFN_TF] = (g * _sigmoid(g) * u).astype(BF16)
        f = _dot(act_ref[rows, :], wd_ref[...])
        o_ref[rows, :] = x + 0.5 * _rms(f, gpost_ref[...])


def _resident(arr, lead=()):
    tail = arr.shape[len(lead):]
    index = tuple(lead) + (0,) * len(tail)
    return pl.BlockSpec((None,) * len(lead) + tail, lambda *_: index, pipeline_mode=pl.Buffered(1))


def _ffn(h, g_pre, g_post, w_gu, w_down, side_casts=()):
    T, D = h.shape
    F = w_down.shape[0]
    tm = min(FFN_TM, T)
    steps = T // tm
    in_specs = [
        pl.BlockSpec((tm, D), lambda i: (i, 0)),
        pl.BlockSpec((1, D), lambda i: (0, 0)),
        pl.BlockSpec((1, D), lambda i: (0, 0)),
        _resident(w_gu),
        _resident(w_down),
    ]
    out_specs = [pl.BlockSpec((tm, D), lambda i: (i, 0))]
    out_shape = [jax.ShapeDtypeStruct((T, D), F32)]
    args = [h, g_pre, g_post, w_gu, w_down]
    cast_specs = []
    for w_all, lead in side_casts:
        rows, cols = w_all.shape[-2:]
        slab = rows // steps
        assert slab * steps == rows and slab % 16 == 0, (rows, steps)
        cast_specs.append(pl.BlockSpec((None,) * len(lead) + (slab, cols),
                                       lambda i, lead=tuple(lead): lead + (i, 0)))
        out_specs.append(pl.BlockSpec((slab, cols), lambda i: (i, 0)))
        out_shape.append(jax.ShapeDtypeStruct((rows, cols), BF16))
        args.append(w_all)
    outs = pl.pallas_call(
        functools.partial(_ffn_kernel, n_casts=len(side_casts)),
        grid=(steps,),
        in_specs=in_specs + cast_specs,
        out_specs=out_specs,
        out_shape=out_shape,
        scratch_shapes=[pltpu.VMEM((tm, F), BF16)],
        compiler_params=_cparams(("parallel",)),
        name="ffn",
    )(*args)
    return outs[0], list(outs[1:])


PROJ_TM = 1024


EV_Q = (0, 512)
EV_CKV = (512, 768)
EV_QI = (768, 1280)
EV_KI2 = (1280, 1408)
EV_GATE = (1408, 1920)
EV_XB = (1920, 2432)
EV_WI = (2432, 2560)
EV_COLS = 2560


DSA_TQ = 512
DSA_CK = 512


def _even_proj_kernel(h_ref, g_ref, w_ref, kvn_ref, wuk_ref, wuv_ref,
                      cw_ref, cb_ref, wra_ref, bra_ref, wri_ref, bri_ref, lam_ref,
                      qt_ref, k_ref, vt_ref, qit_ref, ki_ref, wit_ref, bout_ref,
                      xs_ref, a_ref, b_ref, hc_ref, gg_ref, *, tiles_per_seq):
    _rglru_reset(pl.program_id(0) % tiles_per_seq == 0, xs_ref, hc_ref)
    xn = _rms(h_ref[...], g_ref[...]).astype(BF16)

    def seg(ab):
        return _dot(xn, w_ref[:, ab[0]:ab[1]])

    tm = h_ref.shape[0]
    lru_slab, lru_finish = _rglru_gates(seg(EV_XB), cw_ref, cb_ref, wra_ref, bra_ref, wri_ref, bri_ref,
                                        lam_ref, xs_ref, a_ref, b_ref)
    n_slab = tm // LRU_SLABS

    def p_gate():
        gg_ref[...] = _gelu_tanh(seg(EV_GATE))

    def p_q():
        qt_ref[...] = (seg(EV_Q) * (A_HEAD_DIM ** -0.5 * LOG2E)).T.astype(BF16)

    ckv = []

    def p_k():
        ckv.append(_rms(seg(EV_CKV), kvn_ref[...]).astype(BF16))
        k_ref[...] = _dot(ckv[0], wuk_ref[...]).astype(BF16)

    def p_v():
        v = _dot(ckv[0], wuv_ref[...]).astype(BF16)
        for c in range(vt_ref.shape[0]):
            vt_ref[c] = v[c * DSA_CK:(c + 1) * DSA_CK, :].T

    def p_qi():
        qit_ref[...] = (seg(EV_QI) * (IDX_DIM ** -0.5)).T.astype(BF16)

    def p_ki():
        ki_ref[...] = seg(EV_KI2).astype(BF16)

    def p_wi():
        wit_ref[...] = (seg(EV_WI) * (IDX_HEADS ** -0.5)).T[:IDX_HEADS, :]

    pieces = (p_gate, p_q, p_k, p_v, p_qi, p_ki, p_wi)
    for s in range(LRU_SLABS):
        lru_slab(s * n_slab, n_slab)
        if s < len(pieces):
            pieces[s]()
    lru_finish()
    _rglru_scan(a_ref, b_ref, hc_ref)
    bout_ref[...] = (a_ref[...] * gg_ref[...]).astype(bout_ref.dtype)


def _even_proj(h, g_pre, w_in, kv_norm, w_uk, w_uv, lru_params, seq_len):
    T, D = h.shape
    tm = min(PROJ_TM, seq_len)
    C = B_WIDTH
    row = lambda i: (i, 0)
    col = lambda i: (0, i)
    fixed = lambda i: (0, 0)
    out_specs = [
        pl.BlockSpec((A_WIDTH, tm), col),
        pl.BlockSpec((tm, A_WIDTH), row),
        pl.BlockSpec((tm // DSA_CK, A_WIDTH, DSA_CK), lambda i: (i, 0, 0)),
        pl.BlockSpec((IDX_HEADS * IDX_DIM, tm), col),
        pl.BlockSpec((tm, 2 * IDX_DIM), row),
        pl.BlockSpec((IDX_HEADS, tm), col),
        pl.BlockSpec((tm, C), row),
    ]
    out_shape = [
        jax.ShapeDtypeStruct((A_WIDTH, T), BF16),
        jax.ShapeDtypeStruct((T, A_WIDTH), BF16),
        jax.ShapeDtypeStruct((T // DSA_CK, A_WIDTH, DSA_CK), BF16),
        jax.ShapeDtypeStruct((IDX_HEADS * IDX_DIM, T), BF16),
        jax.ShapeDtypeStruct((T, 2 * IDX_DIM), BF16),
        jax.ShapeDtypeStruct((IDX_HEADS, T), F32),
        jax.ShapeDtypeStruct((T, C), BF16),
    ]
    return pl.pallas_call(
        functools.partial(_even_proj_kernel, tiles_per_seq=seq_len // tm),
        grid=(T // tm,),
        in_specs=[
            pl.BlockSpec((tm, D), row),
            pl.BlockSpec((1, D), fixed),
            _resident(w_in),
            pl.BlockSpec((1, KV_RANK), fixed),
            pl.BlockSpec(w_uk.shape, fixed),
            pl.BlockSpec(w_uv.shape, fixed),
            *[pl.BlockSpec(p.shape, fixed) for p in lru_params],
        ],
        out_specs=out_specs,
        out_shape=out_shape,
        scratch_shapes=[
            pltpu.VMEM((HALO + tm, C), F32),
            pltpu.VMEM((tm, C), F32),
            pltpu.VMEM((tm, C), F32),
            pltpu.VMEM((SUBLANES, C), F32),
            pltpu.VMEM((tm, C), F32),
        ],
        compiler_params=_cparams(("arbitrary",)),
        name="even_proj",
    )(h, g_pre, w_in, kv_norm, w_uk, w_uv, *lru_params)


ACC_ROWS = 4 * SUBLANES
ACC16_ROWS = 2 * ACC_ROWS
SEARCH_PASSES = 18


def _col_partial(x, op):
    rows, n = x.shape
    part = x.reshape(rows // ACC_ROWS, ACC_ROWS, n)
    return jnp.max(part, axis=0) if op == "max" else jnp.sum(part, axis=0)


def _col_reduce(x, op):
    part = _col_partial(x, op)
    return (jnp.max(part, axis=0, keepdims=True) if op == "max"
            else jnp.sum(part, axis=0, keepdims=True))


def _dsa_kernel(qt_ref, qit_ref, wit_ref, k_ref, vt_ref, ki_ref, tri_ref, o_ref,
                sc_ref, hb_ref, bias_ref, s_ref, qm_ref, qim_ref, m_ref, l_ref, acc_ref, need_ref, seen_ref,
                *, top_k):
    TQ, CK = DSA_TQ, DSA_CK
    j = pl.program_id(1)
    q0 = j * TQ
    nkc = (q0 + TQ + CK - 1) // CK

    low_half = lax.broadcasted_iota(jnp.int32, (LANES, TQ), 0) < A_HEAD_DIM
    for h in range(A_HEADS):
        pr = slice((h // 2) * LANES, (h // 2 + 1) * LANES)
        keep = low_half if h % 2 == 0 else jnp.logical_not(low_half)
        qm_ref[h] = jnp.where(keep, qt_ref[pr, :], jnp.zeros((), BF16))
        qim_ref[h] = jnp.where(keep, qit_ref[pr, :], jnp.zeros((), BF16))

    key_iota = lax.broadcasted_iota(jnp.int32, (CK, TQ), 0)
    q_pos = q0 + lax.broadcasted_iota(jnp.int32, (CK, TQ), 1)

    def score_chunk(c, carry):
        base = pl.multiple_of(c * CK, CK)
        kic = ki_ref[pl.ds(base, CK), :]
        acc = jnp.zeros((CK, TQ), F32)
        for h in range(IDX_HEADS):
            acc = acc + jnp.maximum(_dot(kic, qim_ref[h]), 0.0) * wit_ref[h:h + 1, :]
        sc = jnp.where(base + key_iota <= q_pos, acc, -jnp.inf)
        sc_ref[c] = sc
        hb_ref[c] = sc.astype(BF16)
        return carry

    lax.fori_loop(0, nkc, score_chunk, 0)

    def count(pred_fn):
        def body(c, cnt):
            for g in range(CK // ACC_ROWS):
                hit = pred_fn(sc_ref[c, g * ACC_ROWS:(g + 1) * ACC_ROWS, :])
                cnt = jnp.where(hit, cnt + 1, cnt)
            return cnt
        cnt = lax.fori_loop(0, nkc, body, jnp.zeros((ACC_ROWS, TQ), jnp.int32))
        return jnp.sum(cnt, axis=0, keepdims=True)

    def code_to_float(code):
        code = jnp.clip(code, NEG_INF_CODE, POS_INF_CODE)
        return lax.bitcast_convert_type(code ^ ((code >> 31) & jnp.int32(0x7FFFFFFF)), F32)

    def count16(cand_b):
        def body(c, cnt):
            for g in range(CK // ACC16_ROWS):
                blk = hb_ref[c, g * ACC16_ROWS:(g + 1) * ACC16_ROWS, :]
                cnt = jnp.where(blk >= cand_b, cnt + jnp.ones((), BF16), cnt)
            return cnt
        cnt = lax.fori_loop(0, nkc, body, jnp.zeros((ACC16_ROWS, TQ), BF16))
        return jnp.sum(cnt.astype(F32), axis=0, keepdims=True)

    def bit_pass16(i, prefix):
        cand = prefix + lax.shift_left(jnp.int32(1), 31 - i)
        grid = jnp.where(cand < 0, cand | jnp.int32(0xFFFF), cand)
        cand_b = jnp.broadcast_to(code_to_float(grid), (ACC16_ROWS, TQ)).astype(BF16)
        return jnp.where(count16(cand_b) >= top_k, cand, prefix)

    prefix = lax.fori_loop(0, 16, bit_pass16, jnp.full((1, TQ), INT_MIN, jnp.int32))

    def search_pass(i, carry):
        lo, hi, n_at = carry
        mid = lo + ((hi - lo) >> 1)
        mid_f = code_to_float(mid)
        n_ge = count(lambda sc: sc >= mid_f)
        take = n_ge >= top_k
        return jnp.where(take, mid, lo), jnp.where(take, hi, mid), jnp.where(take, n_ge, n_at)

    lo0 = jnp.maximum(prefix, INT_MIN + 0x10000) - 0x10000
    hi0 = jnp.minimum(prefix, 0x7FFD0000) + 0x20000
    code, _, n_at = lax.fori_loop(
        0, SEARCH_PASSES, search_pass, (lo0, hi0, jnp.full((1, TQ), nkc * CK, jnp.int32)))
    thr = code_to_float(code)
    finite = thr > -jnp.inf
    has_ties = jnp.max(jnp.where(finite & (n_at > top_k), 1, 0)) > 0
    thr_sel = jnp.where(finite, thr, jnp.finfo(F32).min)

    m_ref[...] = jnp.full(m_ref.shape, -jnp.inf, F32)
    l_ref[...] = jnp.zeros(l_ref.shape, F32)
    acc_ref[...] = jnp.zeros(acc_ref.shape, F32)
    need_ref[...] = jnp.zeros(need_ref.shape, F32)
    seen_ref[...] = jnp.zeros(seen_ref.shape, F32)

    @pl.when(has_ties)
    def _():
        n_gt = count(lambda sc: sc > thr)
        need_ref[...] = jnp.where(finite, (top_k - n_gt).astype(F32), 0.0)

    def attend_chunk(c, carry):
        base = pl.multiple_of(c * CK, CK)

        @pl.when(jnp.logical_not(has_ties))
        def _():
            bias_ref[...] = jnp.where(sc_ref[c] >= thr_sel, 0.0, NEG_BIG)

        @pl.when(has_ties)
        def _():
            sc = sc_ref[c]
            eq = sc == thr
            rank = seen_ref[...] + _dot(tri_ref[...], eq.astype(BF16))
            sel = (sc > thr) | (eq & (rank <= need_ref[...]))
            bias_ref[...] = jnp.where(sel, 0.0, NEG_BIG)
            seen_ref[...] += _col_reduce(eq.astype(F32), "sum")

        cmax = []
        for h in range(A_HEADS):
            kp = k_ref[pl.ds(base, CK), (h // 2) * LANES:(h // 2 + 1) * LANES]
            s = _dot(kp, qm_ref[h]) + bias_ref[...]
            s_ref[h] = s
            cmax.append(_col_reduce(s, "max"))
        for h in range(A_HEADS):
            vth = vt_ref[c, h * A_HEAD_DIM:(h + 1) * A_HEAD_DIM, :]
            for lane0 in range(0, TQ, MXU_COLS):
                qs = slice(lane0, lane0 + MXU_COLS)
                m_old = m_ref[h, :, qs]
                m_new = jnp.maximum(m_old, cmax[h][:, qs])
                alpha = jnp.exp2(m_old - m_new)
                p = jnp.exp2(s_ref[h, :, qs] - m_new)
                l_ref[h, :, qs] = alpha * l_ref[h, :, qs] + _col_reduce(p, "sum")
                m_ref[h, :, qs] = m_new
                acc_ref[h, :, qs] = acc_ref[h, :, qs] * alpha + _dot(vth, p.astype(BF16))
        return carry

    lax.fori_loop(0, nkc, attend_chunk, 0)

    out_t = jnp.concatenate([acc_ref[h] / l_ref[h] for h in range(A_HEADS)], axis=0)
    o_ref[...] = out_t.T.astype(o_ref.dtype)


def _dsa(qt, k, vt, qit, ki2, wit, B):
    T = k.shape[0]
    S = T // B
    TQ, CK = DSA_TQ, DSA_CK
    nc, nq = S // CK, S // TQ
    top_k = min(TOPK_MAX, S // 4)
    tri = (jnp.arange(CK)[:, None] >= jnp.arange(CK)[None, :]).astype(BF16)
    qcol = lambda b, j: (0, b * nq + j)
    return pl.pallas_call(
        functools.partial(_dsa_kernel, top_k=top_k),
        grid=(B, nq),
        in_specs=[
            pl.BlockSpec((A_WIDTH, TQ), qcol),
            pl.BlockSpec((IDX_HEADS * IDX_DIM, TQ), qcol),
            pl.BlockSpec((IDX_HEADS, TQ), qcol),
            pl.BlockSpec((S, A_WIDTH), lambda b, j: (b, 0)),
            pl.BlockSpec((nc, A_WIDTH, CK), lambda b, j: (b, 0, 0)),
            pl.BlockSpec((S, 2 * IDX_DIM), lambda b, j: (b, 0)),
            pl.BlockSpec((CK, CK), lambda b, j: (0, 0)),
        ],
        out_specs=pl.BlockSpec((TQ, A_WIDTH), lambda b, j: (b * nq + j, 0)),
        out_shape=jax.ShapeDtypeStruct((T, A_WIDTH), BF16),
        scratch_shapes=[
            pltpu.VMEM((nc, CK, TQ), F32),
            pltpu.VMEM((nc, CK, TQ), BF16),
            pltpu.VMEM((CK, TQ), F32),
            pltpu.VMEM((A_HEADS, CK, TQ), F32),
            pltpu.VMEM((A_HEADS, LANES, TQ), BF16),
            pltpu.VMEM((IDX_HEADS, LANES, TQ), BF16),
            pltpu.VMEM((A_HEADS, 1, TQ), F32),
            pltpu.VMEM((A_HEADS, 1, TQ), F32),
            pltpu.VMEM((A_HEADS, A_HEAD_DIM, TQ), F32),
            pltpu.VMEM((1, TQ), F32),
            pltpu.VMEM((1, TQ), F32),
        ],
        compiler_params=_cparams(("parallel", "arbitrary")),
        name="dsa",
    )(qt, qit, wit, k, vt, ki2, tri)


LRU_SLABS = 8
HALO = SUBLANES


def _softplus(x):
    return jnp.maximum(x, 0.0) + jnp.log1p(jnp.exp(-jnp.abs(x)))


def _gelu_tanh(x):
    return 0.5 * x * (1.0 + jnp.tanh(0.7978845608028654 * (x + 0.044715 * (x * x * x))))


def _rglru_reset(first, xs_ref, hc_ref):
    @pl.when(first)
    def _():
        xs_ref[0:HALO, :] = jnp.zeros((HALO, xs_ref.shape[1]), F32)
        hc_ref[...] = jnp.zeros(hc_ref.shape, F32)


def _rglru_gates(x, cw_ref, cb_ref, wra_ref, bra_ref, wri_ref, bri_ref, lam_ref, xs_ref, a_ref, b_ref):
    ts, C = x.shape
    xs_ref[HALO:HALO + ts, :] = x

    def slab(r0, n):
        xc = cb_ref[...] + jnp.zeros((n, C), F32)
        for kk in range(CONV_W):
            off = HALO - (CONV_W - 1) + kk + r0
            xc = xc + cw_ref[kk:kk + 1, :] * xs_ref[off:off + n, :]
        xcb = xc.astype(BF16)
        r = _sigmoid(_dot(xcb, wra_ref[...]) + bra_ref[...])
        gi = _sigmoid(_dot(xcb, wri_ref[...]) + bri_ref[...])
        log_a = (-LRU_C) * r * _softplus(-lam_ref[...])
        a = jnp.exp(log_a)
        a_ref[r0:r0 + n, :] = a
        b_ref[r0:r0 + n, :] = jnp.sqrt(-jnp.tanh(log_a) * (1.0 + a * a)) * (gi * xc)

    def finish():
        xs_ref[0:HALO, :] = xs_ref[ts:ts + HALO, :]

    return slab, finish


def _rglru_scan(a_ref, b_ref, hc_ref):
    ts, C = a_ref.shape
    row = lax.broadcasted_iota(jnp.int32, (SUBLANES, C), 0)

    def group(g, carry):
        r0 = pl.multiple_of(g * SUBLANES, SUBLANES)
        av = a_ref[pl.ds(r0, SUBLANES), :]
        bv = b_ref[pl.ds(r0, SUBLANES), :]
        for sh in (1, 2, 4):
            a_sh = pltpu.roll(av, sh, axis=0)
            b_sh = pltpu.roll(bv, sh, axis=0)
            ok = row >= sh
            bv = jnp.where(ok, av * b_sh + bv, bv)
            av = jnp.where(ok, av * a_sh, av)
        h8 = av * carry + bv
        a_ref[pl.ds(r0, SUBLANES), :] = h8
        return jnp.broadcast_to(h8[SUBLANES - 1:SUBLANES, :], (SUBLANES, C))

    hc_ref[...] = lax.fori_loop(0, ts // SUBLANES, group, hc_ref[...])


OD_COLS = 3200
OD_TAIL = (2048, OD_COLS)


def _odd_proj_kernel(h_ref, g_ref, w_ref, wg2_ref, bg_ref, q_ref, k_ref, v_ref, r_ref, gk_ref):
    xn = _rms(h_ref[...], g_ref[...]).astype(BF16)

    def seg(a, b):
        return _dot(xn, w_ref[:, a:b])

    q_ref[...] = seg(0, GLA_DK) * (GLA_DKH ** -0.5)
    k_ref[...] = seg(GLA_DK, 2 * GLA_DK)
    v_ref[...] = seg(2 * GLA_DK, OD_TAIL[0]).astype(BF16)
    tail = seg(*OD_TAIL)
    r_ref[...] = tail[:, GLA_GATE_RANK:GLA_GATE_RANK + GLA_DV]
    first = tail[:, :LANES]
    lane = lax.broadcasted_iota(jnp.int32, first.shape, 1)
    glr = jnp.where(lane < GLA_GATE_RANK, first, 0.0).astype(BF16)
    z = _dot(glr, wg2_ref[...]) + bg_ref[...]
    gk_ref[...] = (-_softplus(-z)) * (1.0 / GLA_TAU)


def _odd_proj(h, g_pre, w_in, w_g2, b_g):
    T, D = h.shape
    tm = min(PROJ_TM, T)
    row = lambda i: (i, 0)
    fixed = lambda i: (0, 0)
    outs = [(GLA_DK, F32), (GLA_DK, F32), (GLA_DV, BF16), (GLA_DV, F32), (GLA_DK, F32)]
    return pl.pallas_call(
        _odd_proj_kernel,
        grid=(T // tm,),
        in_specs=[
            pl.BlockSpec((tm, D), row),
            pl.BlockSpec((1, D), fixed),
            _resident(w_in),
            pl.BlockSpec(w_g2.shape, fixed),
            pl.BlockSpec((1, GLA_DK), fixed),
        ],
        out_specs=[pl.BlockSpec((tm, n), row) for n, _ in outs],
        out_shape=[jax.ShapeDtypeStruct((T, n), dt) for n, dt in outs],
        compiler_params=_cparams(("parallel",)),
        name="odd_proj",
    )(h, g_pre, w_in, w_g2, b_g)


GLA_TS = 256


def _gla_kernel(q_ref, k_ref, v_ref, gk_ref, r_ref, hn_ref, tri_ref, o_ref,
                st_ref, qd_ref, oi_ref, u_ref, stb_ref):
    ts = q_ref.shape[1]
    C = GLA_CHUNK
    nch = ts // C

    @pl.when(pl.program_id(1) == 0)
    def _():
        st_ref[...] = jnp.zeros(st_ref.shape, F32)

    gk = gk_ref[0]
    g_hi = gk.astype(BF16)
    rem = gk - g_hi.astype(F32)
    g_mid = rem.astype(BF16)
    g_lo = (rem - g_mid.astype(F32)).astype(BF16)
    tri = tri_ref[...]
    G = _dot(tri, g_hi) + _dot(tri, g_mid) + _dot(tri, g_lo)

    kf = k_ref[0]
    qd_ref[...] = (q_ref[0] * jnp.exp(G)).astype(BF16)
    k_inv = (kf * jnp.exp(-G)).astype(BF16)
    g_last = [G[(c + 1) * C - 1:(c + 1) * C, :] for c in range(nch)]
    k_rem = jnp.concatenate(
        [kf[c * C:(c + 1) * C, :] * jnp.exp(g_last[c] - G[c * C:(c + 1) * C, :]) for c in range(nch)],
        axis=0).astype(BF16)

    ri = lax.broadcasted_iota(jnp.int32, (ts, ts), 0)
    ci = lax.broadcasted_iota(jnp.int32, (ts, ts), 1)
    same_chunk_causal = (ri >= ci) & (ri // C == ci // C)
    for h in range(GLA_HEADS):
        ksl = slice(h * GLA_DKH, (h + 1) * GLA_DKH)
        vsl = slice(h * GLA_DVH, (h + 1) * GLA_DVH)
        att = jnp.where(same_chunk_causal, _dot_nt(qd_ref[:, ksl], k_inv[:, ksl]), 0.0).astype(BF16)
        oi_ref[:, vsl] = _dot(att, v_ref[0, :, vsl])

    for c in range(nch):
        rows = slice(c * C, (c + 1) * C)
        for h in range(GLA_HEADS):
            ksl = slice(h * GLA_DKH, (h + 1) * GLA_DKH)
            vsl = slice(h * GLA_DVH, (h + 1) * GLA_DVH)
            u_ref[c, h] = _dot_tn(v_ref[0, rows, vsl], k_rem[rows, ksl])

    for h in range(GLA_HEADS):
        ksl = slice(h * GLA_DKH, (h + 1) * GLA_DKH)
        st = st_ref[h]
        for c in range(nch):
            stb_ref[c, h] = st.astype(BF16)
            st = st * jnp.exp(g_last[c][:, ksl]) + u_ref[c, h]
        st_ref[h] = st

    for c in range(nch):
        rows = slice(c * C, (c + 1) * C)
        for h in range(GLA_HEADS):
            ksl = slice(h * GLA_DKH, (h + 1) * GLA_DKH)
            vsl = slice(h * GLA_DVH, (h + 1) * GLA_DVH)
            o = oi_ref[rows, vsl] + _dot_nt(qd_ref[rows, ksl], stb_ref[c, h])
            on = _rms(o, hn_ref[...])
            rr = r_ref[0, rows, vsl]
            o_ref[0, rows, vsl] = (on * (rr * _sigmoid(rr))).astype(o_ref.dtype)


def _gla(q, k, v, gk, r, head_norm):
    B, S, _ = q.shape
    ts = min(GLA_TS, S)
    nch = ts // GLA_CHUNK
    pos = jnp.arange(ts)
    tri = ((pos[:, None] >= pos[None, :])
           & (pos[:, None] // GLA_CHUNK == pos[None, :] // GLA_CHUNK)).astype(BF16)
    blk = lambda b, s: (b, s, 0)
    return pl.pallas_call(
        _gla_kernel,
        grid=(B, S // ts),
        in_specs=[
            pl.BlockSpec((1, ts, GLA_DK), blk),
            pl.BlockSpec((1, ts, GLA_DK), blk),
            pl.BlockSpec((1, ts, GLA_DV), blk),
            pl.BlockSpec((1, ts, GLA_DK), blk),
            pl.BlockSpec((1, ts, GLA_DV), blk),
            pl.BlockSpec((1, GLA_DVH), lambda b, s: (0, 0)),
            pl.BlockSpec((ts, ts), lambda b, s: (0, 0)),
        ],
        out_specs=pl.BlockSpec((1, ts, GLA_DV), blk),
        out_shape=jax.ShapeDtypeStruct((B, S, GLA_DV), BF16),
        scratch_shapes=[
            pltpu.VMEM((GLA_HEADS, GLA_DVH, GLA_DKH), F32),
            pltpu.VMEM((ts, GLA_DK), BF16),
            pltpu.VMEM((ts, GLA_DV), F32),
            pltpu.VMEM((nch, GLA_HEADS, GLA_DVH, GLA_DKH), F32),
            pltpu.VMEM((nch, GLA_HEADS, GLA_DVH, GLA_DKH), BF16),
        ],
        compiler_params=_cparams(("parallel", "arbitrary")),
        name="gla",
    )(q, k, v, gk, r, head_norm, tri)


def _xa_kv_kernel(mem_ref, g_ref, w_ref, k_ref, v_ref):
    mn = _rms(mem_ref[...], g_ref[...]).astype(BF16)
    k_ref[...] = (_dot(mn, w_ref[:, :D_MODEL].astype(BF16)) * (XA_HEAD_DIM ** -0.5)).astype(BF16)
    v_ref[...] = _dot(mn, w_ref[:, D_MODEL:].astype(BF16)).astype(BF16)


def _xa_kv(mem, g_mem, w_kv, lead):
    B, M, D = mem.shape
    whole = lambda i: (0, 0)
    k, v = pl.pallas_call(
        _xa_kv_kernel,
        grid=(1,),
        in_specs=[pl.BlockSpec((B * M, D), whole), pl.BlockSpec((1, D), whole), _resident(w_kv, lead)],
        out_specs=[pl.BlockSpec((B * M, D), whole), pl.BlockSpec((B * M, D), whole)],
        out_shape=[jax.ShapeDtypeStruct((B * M, D), BF16)] * 2,
        compiler_params=_cparams(("arbitrary",)),
        name="xa_kv",
    )(mem.reshape(B * M, D), g_mem, w_kv)
    return k.reshape(B, M, D), v.reshape(B, M, D)


XA_TM = 1024


def _mix_out_xa_kernel(*refs, offsets):
    n = len(offsets)
    h_ref, gmix_ref, gpre_ref, gpost_ref = refs[:4]
    part_refs = refs[4:4 + n]
    wout_ref, wq_ref, k_ref, v_ref, wo_ref, o_ref = refs[4 + n:]
    m = None
    for p_ref, off in zip(part_refs, offsets):
        kk = p_ref.shape[-1]
        term = _dot(p_ref[0], wout_ref[off:off + kk, :])
        m = term if m is None else m + term
    x = h_ref[0] + _rms(m, gmix_ref[...])
    xn = _rms(x, gpre_ref[...]).astype(BF16)
    q = _dot(xn, wq_ref[...]).astype(BF16)
    heads = []
    for h in range(XA_HEADS):
        sl = slice(h * XA_HEAD_DIM, (h + 1) * XA_HEAD_DIM)
        s = _dot_nt(q[:, sl], k_ref[0, :, sl])
        p = jnp.exp(s - jnp.max(s, axis=-1, keepdims=True))
        oh = _dot(p.astype(BF16), v_ref[0, :, sl]) / jnp.sum(p, axis=-1, keepdims=True)
        heads.append(oh.astype(BF16))
    c = _dot(jnp.concatenate(heads, axis=-1), wo_ref[...])
    o_ref[0] = x + _rms(c, gpost_ref[...])


def _mix_out_xa(h, g_mix, g_pre, g_post, parts, w_out, w_q, kx, vx, w_o):
    B, S, D = h.shape
    M = kx.shape[1]
    tm = min(XA_TM, S)
    blk = lambda b, i: (b, i, 0)
    fixed = lambda b, i: (0, 0)
    offsets, off = [], 0
    for p in parts:
        offsets.append(off)
        off += p.shape[-1]
    return pl.pallas_call(
        functools.partial(_mix_out_xa_kernel, offsets=tuple(offsets)),
        grid=(B, S // tm),
        in_specs=[
            pl.BlockSpec((1, tm, D), blk),
            pl.BlockSpec((1, D), fixed),
            pl.BlockSpec((1, D), fixed),
            pl.BlockSpec((1, D), fixed),
            *[pl.BlockSpec((1, tm, p.shape[-1]), blk) for p in parts],
            _resident(w_out),
            _resident(w_q),
            pl.BlockSpec((1, M, D), lambda b, i: (b, 0, 0)),
            pl.BlockSpec((1, M, D), lambda b, i: (b, 0, 0)),
            _resident(w_o),
        ],
        out_specs=pl.BlockSpec((1, tm, D), blk),
        out_shape=jax.ShapeDtypeStruct((B, S, D), F32),
        compiler_params=_cparams(("parallel", "parallel")),
        name="mix_out_xa",
    )(h, g_mix, g_pre, g_post, *parts, w_out, w_q, kx, vx, w_o)


def _block_diag(w):
    G, n, _ = w.shape
    eye = jnp.eye(G, dtype=w.dtype)
    return (eye[:, None, :, None] * w[:, :, None, :]).reshape(G * n, G * n)


def _even_w_in(w):
    ki = w[:, 1280:1344]
    pad = jnp.zeros((w.shape[0], LANES - IDX_HEADS), w.dtype)
    return jnp.concatenate([w[:, :1280], ki, ki, w[:, 1352:2376], w[:, 1344:1352], pad], axis=1).astype(BF16)


def _odd_w_in(w):
    return jnp.pad(w.astype(BF16), ((0, 0), (0, OD_COLS - w.shape[1])))


def kernel(x, mem, norms, ffn_w_gu, ffn_w_down, xa_w_q, xa_w_kv, xa_w_o, ev_w_in, ev_kv_norm, ev_w_uk, ev_w_uv, ev_conv_w, ev_conv_b, ev_w_ra, ev_b_ra, ev_w_ri, ev_b_ri, ev_lam, ev_w_out, od_w_in, od_w_g2, od_b_g, od_head_norm, od_w_out):
    B, S, D = x.shape
    T = B * S
    depth = norms.shape[0]
    h = x.reshape(T, D)

    def gain(layer, idx):
        return norms[layer, idx][None, :]

    ffn_order = [(layer, j) for layer in range(depth) for j in range(2)]
    ffn_w = [ffn_w_gu[0, 0].astype(BF16), ffn_w_down[0, 0].astype(BF16)]

    def ffn(h, ffn_w, layer, j, g_pre, g_post, extra=()):
        k = ffn_order.index((layer, j))
        casts = list(extra)
        if k + 1 < len(ffn_order):
            casts = [(ffn_w_gu, ffn_order[k + 1]), (ffn_w_down, ffn_order[k + 1])] + casts
        h, cast = _ffn(h, g_pre, g_post, *ffn_w, side_casts=casts)
        n_next = len(casts) - len(extra)
        return h, cast[:n_next], cast[n_next:]

    for layer in range(depth):
        mix_out = (ev_w_out, (layer // 2,)) if layer % 2 == 0 else (od_w_out, (layer // 2,))
        h, ffn_w, (w_out, w_q, w_o) = ffn(
            h, ffn_w, layer, 0, gain(layer, N_FFN1_PRE), gain(layer, N_FFN1_POST),
            extra=[mix_out, (xa_w_q, (layer,)), (xa_w_o, (layer,))])

        if layer % 2 == 0:
            e = layer // 2
            lru_params = (ev_conv_w[e], ev_conv_b[e][None, :],
                          _block_diag(ev_w_ra[e]).astype(BF16), ev_b_ra[e].reshape(1, B_WIDTH),
                          _block_diag(ev_w_ri[e]).astype(BF16), ev_b_ri[e].reshape(1, B_WIDTH),
                          ev_lam[e][None, :])
            qt, k, vt, qit, ki2, wit, b_out = _even_proj(
                h, gain(layer, N_MIX_PRE), _even_w_in(ev_w_in[e]), ev_kv_norm[e][None, :],
                ev_w_uk[e].astype(BF16), ev_w_uv[e].astype(BF16), lru_params, S)
            a_out = _dsa(qt, k, vt, qit, ki2, wit, B)
            parts = [a_out.reshape(B, S, A_WIDTH), b_out.reshape(B, S, B_WIDTH)]
        else:
            o = layer // 2
            w_g2 = jnp.concatenate(
                [od_w_g2[o], jnp.zeros((LANES - GLA_GATE_RANK, GLA_DK), od_w_g2.dtype)], axis=0).astype(BF16)
            q, k, v, r, gk = _odd_proj(h, gain(layer, N_MIX_PRE), _odd_w_in(od_w_in[o]), w_g2,
                                       od_b_g[o][None, :])
            r3 = lambda a: a.reshape(B, S, a.shape[-1])
            g_out = _gla(r3(q), r3(k), r3(v), r3(gk), r3(r), od_head_norm[o][None, :])
            parts = [g_out]

        kx, vx = _xa_kv(mem, gain(layer, N_MEM_NORM), xa_w_kv, (layer,))
        h = _mix_out_xa(h.reshape(B, S, D), gain(layer, N_MIX_POST), gain(layer, N_XA_PRE),
                        gain(layer, N_XA_POST), parts, w_out, w_q, kx, vx, w_o).reshape(T, D)

        h, ffn_w, _ = ffn(h, ffn_w, layer, 1, gain(layer, N_FFN2_PRE), gain(layer, N_FFN2_POST))
    return h.reshape(B, S, D)
```

```python
import functools

import jax
import jax.numpy as jnp
from jax import lax
from jax.experimental import pallas as pl
from jax.experimental.pallas import tpu as pltpu

F32 = jnp.float32
BF16 = jnp.bfloat16

EPS = 1e-6
D_MODEL = 1024
D_FF = 2816
XA_HEADS = 4
XA_HEAD_DIM = D_MODEL // XA_HEADS
A_HEADS = 8
A_HEAD_DIM = 64
A_WIDTH = A_HEADS * A_HEAD_DIM
KV_RANK = 256
IDX_HEADS = 8
IDX_DIM = 64
TOPK_MAX = 256
B_WIDTH = D_MODEL - A_WIDTH
B_BLOCKS = 8
B_BLOCK_DIM = B_WIDTH // B_BLOCKS
CONV_W = 4
LRU_C = 8.0
GLA_HEADS = 4
GLA_DK = D_MODEL // 2
GLA_DV = D_MODEL
GLA_DKH = GLA_DK // GLA_HEADS
GLA_DVH = GLA_DV // GLA_HEADS
GLA_GATE_RANK = 16
GLA_TAU = 16.0
GLA_CHUNK = 64
(N_FFN1_PRE, N_FFN1_POST, N_MIX_PRE, N_MIX_POST, N_XA_PRE, N_XA_POST, N_MEM_NORM,
 N_FFN2_PRE, N_FFN2_POST) = range(9)

LANES = 128
SUBLANES = 8
MXU_COLS = 256
VMEM_LIMIT = 48 * 1024 * 1024

NEG_BIG = -1e30
LOG2E = 1.4426950408889634
INT_MIN = -2 ** 31
POS_INF_CODE = 0x7F800000
NEG_INF_CODE = -0x7F800001


def _cparams(sem):
    return pltpu.CompilerParams(dimension_semantics=sem, vmem_limit_bytes=VMEM_LIMIT)


def _rms(x, g):
    return x * lax.rsqrt(jnp.mean(x * x, axis=-1, keepdims=True) + EPS) * g


def _dot(a, b):
    return jnp.dot(a, b, preferred_element_type=F32)


def _dot_nt(a, b):
    return lax.dot_general(a, b, (((1,), (1,)), ((), ())), preferred_element_type=F32)


def _dot_tn(a, b):
    return lax.dot_general(a, b, (((0,), (0,)), ((), ())), preferred_element_type=F32)


def _sigmoid(x):
    return 1.0 / (1.0 + jnp.exp(-x))


FFN_TM = 1024
FFN_SLAB = 512
FFN_TF = 256


def _ffn_kernel(*refs, n_casts):
    h_ref, gpre_ref, gpost_ref, wgu_ref, wd_ref = refs[:5]
    cast_in = refs[5:5 + n_casts]
    o_ref = refs[5 + n_casts]
    cast_out = refs[6 + n_casts:6 + 2 * n_casts]
    act_ref = refs[6 + 2 * n_casts]
    for src_ref, dst_ref in zip(cast_in, cast_out):
        dst_ref[...] = src_ref[...].astype(BF16)
    F = wd_ref.shape[0]
    for r0 in range(0, h_ref.shape[0], FFN_SLAB):
        rows = slice(r0, r0 + FFN_SLAB)
        x = h_ref[rows, :]
        xn = _rms(x, gpre_ref[...]).astype(BF16)
        for c in range(F // FFN_TF):
            g = _dot(xn, wgu_ref[:, c * FFN_TF:(c + 1) * FFN_TF])
            u = _dot(xn, wgu_ref[:, F + c * FFN_TF:F + (c + 1) * FFN_TF])
            act_ref[rows, c * FFN_TF:(c + 1) * FFN_TF] = (g * _sigmoid(g) * u).astype(BF16)
        f = _dot(act_ref[rows, :], wd_ref[...])
        o_ref[rows, :] = x + 0.5 * _rms(f, gpost_ref[...])


def _resident(arr, lead=()):
    tail = arr.shape[len(lead):]
    index = tuple(lead) + (0,) * len(tail)
    return pl.BlockSpec((None,) * len(lead) + tail, lambda *_: index, pipeline_mode=pl.Buffered(1))


def _ffn(h, g_pre, g_post, w_gu, w_down, side_casts=()):
    T, D = h.shape
    F = w_down.shape[0]
    tm = min(FFN_TM, T)
    steps = T // tm
    in_specs = [
        pl.BlockSpec((tm, D), lambda i: (i, 0)),
        pl.BlockSpec((1, D), lambda i: (0, 0)),
        pl.BlockSpec((1, D), lambda i: (0, 0)),
        _resident(w_gu),
        _resident(w_down),
    ]
    out_specs = [pl.BlockSpec((tm, D), lambda i: (i, 0))]
    out_shape = [jax.ShapeDtypeStruct((T, D), F32)]
    args = [h, g_pre, g_post, w_gu, w_down]
    cast_specs = []
    for w_all, lead in side_casts:
        rows, cols = w_all.shape[-2:]
        slab = rows // steps
        assert slab * steps == rows and slab % 16 == 0, (rows, steps)
        cast_specs.append(pl.BlockSpec((None,) * len(lead) + (slab, cols),
                                       lambda i, lead=tuple(lead): lead + (i, 0)))
        out_specs.append(pl.BlockSpec((slab, cols), lambda i: (i, 0)))
        out_shape.append(jax.ShapeDtypeStruct((rows, cols), BF16))
        args.append(w_all)
    outs = pl.pallas_call(
        functools.partial(_ffn_kernel, n_casts=len(side_casts)),
        grid=(steps,),
        in_specs=in_specs + cast_specs,
        out_specs=out_specs,
        out_shape=out_shape,
        scratch_shapes=[pltpu.VMEM((tm, F), BF16)],
        compiler_params=_cparams(("parallel",)),
        name="ffn",
    )(*args)
    return outs[0], list(outs[1:])


PROJ_TM = 1024


EV_Q = (0, 512)
EV_CKV = (512, 768)
EV_QI = (768, 1280)
EV_KI2 = (1280, 1408)
EV_GATE = (1408, 1920)
EV_XB = (1920, 2432)
EV_WI = (2432, 2560)
EV_COLS = 2560


DSA_TQ = 512
DSA_CK = 512


def _even_proj_kernel(h_ref, g_ref, w_ref, kvn_ref, wuk_ref, wuv_ref,
                      cw_ref, cb_ref, wra_ref, bra_ref, wri_ref, bri_ref, lam_ref,
                      qt_ref, k_ref, vt_ref, qit_ref, ki_ref, wit_ref, bout_ref,
                      xs_ref, a_ref, b_ref, hc_ref, gg_ref, *, tiles_per_seq):
    _rglru_reset(pl.program_id(0) % tiles_per_seq == 0, xs_ref, hc_ref)
    xn = _rms(h_ref[...], g_ref[...]).astype(BF16)

    def seg(ab):
        return _dot(xn, w_ref[:, ab[0]:ab[1]])

    tm = h_ref.shape[0]
    lru_slab, lru_finish = _rglru_gates(seg(EV_XB), cw_ref, cb_ref, wra_ref, bra_ref, wri_ref, bri_ref,
                                        lam_ref, xs_ref, a_ref, b_ref)
    n_slab = tm // LRU_SLABS

    def p_gate():
        gg_ref[...] = _gelu_tanh(seg(EV_GATE))

    def p_q():
        qt_ref[...] = (seg(EV_Q) * (A_HEAD_DIM ** -0.5 * LOG2E)).T.astype(BF16)

    ckv = []

    def p_k():
        ckv.append(_rms(seg(EV_CKV), kvn_ref[...]).astype(BF16))
        k_ref[...] = _dot(ckv[0], wuk_ref[...]).astype(BF16)

    def p_v():
        v = _dot(ckv[0], wuv_ref[...]).astype(BF16)
        for c in range(vt_ref.shape[0]):
            vt_ref[c] = v[c * DSA_CK:(c + 1) * DSA_CK, :].T

    def p_qi():
        qit_ref[...] = (seg(EV_QI) * (IDX_DIM ** -0.5)).T.astype(BF16)

    def p_ki():
        ki_ref[...] = seg(EV_KI2).astype(BF16)

    def p_wi():
        wit_ref[...] = (seg(EV_WI) * (IDX_HEADS ** -0.5)).T[:IDX_HEADS, :]

    pieces = (p_gate, p_q, p_k, p_v, p_qi, p_ki, p_wi)
    for s in range(LRU_SLABS):
        lru_slab(s * n_slab, n_slab)
        if s < len(pieces):
            pieces[s]()
    lru_finish()
    _rglru_scan(a_ref, b_ref, hc_ref)
    bout_ref[...] = (a_ref[...] * gg_ref[...]).astype(bout_ref.dtype)


def _even_proj(h, g_pre, w_in, kv_norm, w_uk, w_uv, lru_params, seq_len):
    T, D = h.shape
    tm = min(PROJ_TM, seq_len)
    C = B_WIDTH
    row = lambda i: (i, 0)
    col = lambda i: (0, i)
    fixed = lambda i: (0, 0)
    out_specs = [
        pl.BlockSpec((A_WIDTH, tm), col),
        pl.BlockSpec((tm, A_WIDTH), row),
        pl.BlockSpec((tm // DSA_CK, A_WIDTH, DSA_CK), lambda i: (i, 0, 0)),
        pl.BlockSpec((IDX_HEADS * IDX_DIM, tm), col),
        pl.BlockSpec((tm, 2 * IDX_DIM), row),
        pl.BlockSpec((IDX_HEADS, tm), col),
        pl.BlockSpec((tm, C), row),
    ]
    out_shape = [
        jax.ShapeDtypeStruct((A_WIDTH, T), BF16),
        jax.ShapeDtypeStruct((T, A_WIDTH), BF16),
        jax.ShapeDtypeStruct((T // DSA_CK, A_WIDTH, DSA_CK), BF16),
        jax.ShapeDtypeStruct((IDX_HEADS * IDX_DIM, T), BF16),
        jax.ShapeDtypeStruct((T, 2 * IDX_DIM), BF16),
        jax.ShapeDtypeStruct((IDX_HEADS, T), F32),
        jax.ShapeDtypeStruct((T, C), BF16),
    ]
    return pl.pallas_call(
        functools.partial(_even_proj_kernel, tiles_per_seq=seq_len // tm),
        grid=(T // tm,),
        in_specs=[
            pl.BlockSpec((tm, D), row),
            pl.BlockSpec((1, D), fixed),
            _resident(w_in),
            pl.BlockSpec((1, KV_RANK), fixed),
            pl.BlockSpec(w_uk.shape, fixed),
            pl.BlockSpec(w_uv.shape, fixed),
            *[pl.BlockSpec(p.shape, fixed) for p in lru_params],
        ],
        out_specs=out_specs,
        out_shape=out_shape,
        scratch_shapes=[
            pltpu.VMEM((HALO + tm, C), F32),
            pltpu.VMEM((tm, C), F32),
            pltpu.VMEM((tm, C), F32),
            pltpu.VMEM((SUBLANES, C), F32),
            pltpu.VMEM((tm, C), F32),
        ],
        compiler_params=_cparams(("arbitrary",)),
        name="even_proj",
    )(h, g_pre, w_in, kv_norm, w_uk, w_uv, *lru_params)


ACC_ROWS = 4 * SUBLANES
ACC16_ROWS = 2 * ACC_ROWS
SEARCH_WIDTH = 0x18003
SEARCH_PASSES = 17


def _col_partial(x, op):
    rows, n = x.shape
    part = x.reshape(rows // ACC_ROWS, ACC_ROWS, n)
    return jnp.max(part, axis=0) if op == "max" else jnp.sum(part, axis=0)


def _col_reduce(x, op):
    part = _col_partial(x, op)
    return (jnp.max(part, axis=0, keepdims=True) if op == "max"
            else jnp.sum(part, axis=0, keepdims=True))


def _dsa_kernel(qt_ref, qit_ref, wit_ref, k_ref, vt_ref, ki_ref, tri_ref, o_ref,
                sc_ref, hb_ref, bias_ref, s_ref, qm_ref, qim_ref, m_ref, l_ref, acc_ref, need_ref, seen_ref,
                *, top_k):
    TQ, CK = DSA_TQ, DSA_CK
    j = pl.program_id(1)
    q0 = j * TQ
    nkc = (q0 + TQ + CK - 1) // CK

    low_half = lax.broadcasted_iota(jnp.int32, (LANES, TQ), 0) < A_HEAD_DIM
    for h in range(A_HEADS):
        pr = slice((h // 2) * LANES, (h // 2 + 1) * LANES)
        keep = low_half if h % 2 == 0 else jnp.logical_not(low_half)
        qm_ref[h] = jnp.where(keep, qt_ref[pr, :], jnp.zeros((), BF16))
        qim_ref[h] = jnp.where(keep, qit_ref[pr, :], jnp.zeros((), BF16))

    key_iota = lax.broadcasted_iota(jnp.int32, (CK, TQ), 0)
    q_pos = q0 + lax.broadcasted_iota(jnp.int32, (CK, TQ), 1)

    def score_chunk(c, carry):
        base = pl.multiple_of(c * CK, CK)
        kic = ki_ref[pl.ds(base, CK), :]
        acc = jnp.zeros((CK, TQ), F32)
        for h in range(IDX_HEADS):
            acc = acc + jnp.maximum(_dot(kic, qim_ref[h]), 0.0) * wit_ref[h:h + 1, :]
        sc = jnp.where(base + key_iota <= q_pos, acc, -jnp.inf)
        sc_ref[c] = sc
        hb_ref[c] = sc.astype(BF16)
        return carry

    lax.fori_loop(0, nkc, score_chunk, 0)

    def count(pred_fn):
        def body(c, cnt):
            for g in range(CK // ACC_ROWS):
                hit = pred_fn(sc_ref[c, g * ACC_ROWS:(g + 1) * ACC_ROWS, :])
                cnt = jnp.where(hit, cnt + 1, cnt)
            return cnt
        cnt = lax.fori_loop(0, nkc, body, jnp.zeros((ACC_ROWS, TQ), jnp.int32))
        return jnp.sum(cnt, axis=0, keepdims=True)

    def code_to_float(code):
        code = jnp.clip(code, NEG_INF_CODE, POS_INF_CODE)
        return lax.bitcast_convert_type(code ^ ((code >> 31) & jnp.int32(0x7FFFFFFF)), F32)

    def count16(cand_b):
        def body(c, cnt):
            for g in range(CK // ACC16_ROWS):
                blk = hb_ref[c, g * ACC16_ROWS:(g + 1) * ACC16_ROWS, :]
                cnt = jnp.where(blk >= cand_b, cnt + jnp.ones((), BF16), cnt)
            return cnt
        cnt = lax.fori_loop(0, nkc, body, jnp.zeros((ACC16_ROWS, TQ), BF16))
        return jnp.sum(cnt.astype(F32), axis=0, keepdims=True)

    def bit_pass16(i, prefix):
        cand = prefix + lax.shift_left(jnp.int32(1), 31 - i)
        grid = jnp.where(cand < 0, cand | jnp.int32(0xFFFF), cand)
        cand_b = jnp.broadcast_to(code_to_float(grid), (ACC16_ROWS, TQ)).astype(BF16)
        return jnp.where(count16(cand_b) >= top_k, cand, prefix)

    prefix = lax.fori_loop(0, 16, bit_pass16, jnp.full((1, TQ), INT_MIN, jnp.int32))

    def search_pass(i, carry):
        lo, hi, n_at = carry
        mid = lo + ((hi - lo) >> 1)
        mid_f = code_to_float(mid)
        n_ge = count(lambda sc: sc >= mid_f)
        take = n_ge >= top_k
        return jnp.where(take, mid, lo), jnp.where(take, hi, mid), jnp.where(take, n_ge, n_at)

    lo0 = jnp.where(prefix >= 0, prefix - 0x8001, prefix + 0x7FFE)
    hi0 = jnp.minimum(lo0, 0x7FFFFFFF - SEARCH_WIDTH) + SEARCH_WIDTH
    code, _, n_at = lax.fori_loop(
        0, SEARCH_PASSES, search_pass, (lo0, hi0, jnp.full((1, TQ), nkc * CK, jnp.int32)))
    thr = code_to_float(code)
    finite = thr > -jnp.inf
    has_ties = jnp.max(jnp.where(finite & (n_at > top_k), 1, 0)) > 0
    thr_sel = jnp.where(finite, thr, jnp.finfo(F32).min)

    m_ref[...] = jnp.full(m_ref.shape, -jnp.inf, F32)
    l_ref[...] = jnp.zeros(l_ref.shape, F32)
    acc_ref[...] = jnp.zeros(acc_ref.shape, F32)
    need_ref[...] = jnp.zeros(need_ref.shape, F32)
    seen_ref[...] = jnp.zeros(seen_ref.shape, F32)

    @pl.when(has_ties)
    def _():
        n_gt = count(lambda sc: sc > thr)
        need_ref[...] = jnp.where(finite, (top_k - n_gt).astype(F32), 0.0)

    def attend_chunk(c, carry):
        base = pl.multiple_of(c * CK, CK)

        @pl.when(jnp.logical_not(has_ties))
        def _():
            bias_ref[...] = jnp.where(sc_ref[c] >= thr_sel, 0.0, NEG_BIG)

        @pl.when(has_ties)
        def _():
            sc = sc_ref[c]
            eq = sc == thr
            rank = seen_ref[...] + _dot(tri_ref[...], eq.astype(BF16))
            sel = (sc > thr) | (eq & (rank <= need_ref[...]))
            bias_ref[...] = jnp.where(sel, 0.0, NEG_BIG)
            seen_ref[...] += _col_reduce(eq.astype(F32), "sum")

        cmax = []
        for h in range(A_HEADS):
            kp = k_ref[pl.ds(base, CK), (h // 2) * LANES:(h // 2 + 1) * LANES]
            s = _dot(kp, qm_ref[h]) + bias_ref[...]
            s_ref[h] = s
            cmax.append(_col_reduce(s, "max"))
        for h in range(A_HEADS):
            vth = vt_ref[c, h * A_HEAD_DIM:(h + 1) * A_HEAD_DIM, :]
            for lane0 in range(0, TQ, MXU_COLS):
                qs = slice(lane0, lane0 + MXU_COLS)
                m_old = m_ref[h, :, qs]
                m_new = jnp.maximum(m_old, cmax[h][:, qs])
                alpha = jnp.exp2(m_old - m_new)
                p = jnp.exp2(s_ref[h, :, qs] - m_new)
                l_ref[h, :, qs] = alpha * l_ref[h, :, qs] + _col_reduce(p, "sum")
                m_ref[h, :, qs] = m_new
                acc_ref[h, :, qs] = acc_ref[h, :, qs] * alpha + _dot(vth, p.astype(BF16))
        return carry

    lax.fori_loop(0, nkc, attend_chunk, 0)

    out_t = jnp.concatenate([acc_ref[h] / l_ref[h] for h in range(A_HEADS)], axis=0)
    o_ref[...] = out_t.T.astype(o_ref.dtype)


def _dsa(qt, k, vt, qit, ki2, wit, B):
    T = k.shape[0]
    S = T // B
    TQ, CK = DSA_TQ, DSA_CK
    nc, nq = S // CK, S // TQ
    top_k = min(TOPK_MAX, S // 4)
    tri = (jnp.arange(CK)[:, None] >= jnp.arange(CK)[None, :]).astype(BF16)
    qcol = lambda b, j: (0, b * nq + j)
    return pl.pallas_call(
        functools.partial(_dsa_kernel, top_k=top_k),
        grid=(B, nq),
        in_specs=[
            pl.BlockSpec((A_WIDTH, TQ), qcol),
            pl.BlockSpec((IDX_HEADS * IDX_DIM, TQ), qcol),
            pl.BlockSpec((IDX_HEADS, TQ), qcol),
            pl.BlockSpec((S, A_WIDTH), lambda b, j: (b, 0)),
            pl.BlockSpec((nc, A_WIDTH, CK), lambda b, j: (b, 0, 0)),
            pl.BlockSpec((S, 2 * IDX_DIM), lambda b, j: (b, 0)),
            pl.BlockSpec((CK, CK), lambda b, j: (0, 0)),
        ],
        out_specs=pl.BlockSpec((TQ, A_WIDTH), lambda b, j: (b * nq + j, 0)),
        out_shape=jax.ShapeDtypeStruct((T, A_WIDTH), BF16),
        scratch_shapes=[
            pltpu.VMEM((nc, CK, TQ), F32),
            pltpu.VMEM((nc, CK, TQ), BF16),
            pltpu.VMEM((CK, TQ), F32),
            pltpu.VMEM((A_HEADS, CK, TQ), F32),
            pltpu.VMEM((A_HEADS, LANES, TQ), BF16),
            pltpu.VMEM((IDX_HEADS, LANES, TQ), BF16),
            pltpu.VMEM((A_HEADS, 1, TQ), F32),
            pltpu.VMEM((A_HEADS, 1, TQ), F32),
            pltpu.VMEM((A_HEADS, A_HEAD_DIM, TQ), F32),
            pltpu.VMEM((1, TQ), F32),
            pltpu.VMEM((1, TQ), F32),
        ],
        compiler_params=_cparams(("parallel", "arbitrary")),
        name="dsa",
    )(qt, qit, wit, k, vt, ki2, tri)


LRU_SLABS = 8
HALO = SUBLANES


def _softplus(x):
    return jnp.maximum(x, 0.0) + jnp.log1p(jnp.exp(-jnp.abs(x)))


def _gelu_tanh(x):
    return 0.5 * x * (1.0 + jnp.tanh(0.7978845608028654 * (x + 0.044715 * (x * x * x))))


def _rglru_reset(first, xs_ref, hc_ref):
    @pl.when(first)
    def _():
        xs_ref[0:HALO, :] = jnp.zeros((HALO, xs_ref.shape[1]), F32)
        hc_ref[...] = jnp.zeros(hc_ref.shape, F32)


def _rglru_gates(x, cw_ref, cb_ref, wra_ref, bra_ref, wri_ref, bri_ref, lam_ref, xs_ref, a_ref, b_ref):
    ts, C = x.shape
    xs_ref[HALO:HALO + ts, :] = x

    def slab(r0, n):
        xc = cb_ref[...] + jnp.zeros((n, C), F32)
        for kk in range(CONV_W):
            off = HALO - (CONV_W - 1) + kk + r0
            xc = xc + cw_ref[kk:kk + 1, :] * xs_ref[off:off + n, :]
        xcb = xc.astype(BF16)
        r = _sigmoid(_dot(xcb, wra_ref[...]) + bra_ref[...])
        gi = _sigmoid(_dot(xcb, wri_ref[...]) + bri_ref[...])
        log_a = (-LRU_C) * r * _softplus(-lam_ref[...])
        a = jnp.exp(log_a)
        a_ref[r0:r0 + n, :] = a
        b_ref[r0:r0 + n, :] = jnp.sqrt(-jnp.tanh(log_a) * (1.0 + a * a)) * (gi * xc)

    def finish():
        xs_ref[0:HALO, :] = xs_ref[ts:ts + HALO, :]

    return slab, finish


def _rglru_scan(a_ref, b_ref, hc_ref):
    ts, C = a_ref.shape
    row = lax.broadcasted_iota(jnp.int32, (SUBLANES, C), 0)

    def group(g, carry):
        r0 = pl.multiple_of(g * SUBLANES, SUBLANES)
        av = a_ref[pl.ds(r0, SUBLANES), :]
        bv = b_ref[pl.ds(r0, SUBLANES), :]
        for sh in (1, 2, 4):
            a_sh = pltpu.roll(av, sh, axis=0)
            b_sh = pltpu.roll(bv, sh, axis=0)
            ok = row >= sh
            bv = jnp.where(ok, av * b_sh + bv, bv)
            av = jnp.where(ok, av * a_sh, av)
        h8 = av * carry + bv
        a_ref[pl.ds(r0, SUBLANES), :] = h8
        return jnp.broadcast_to(h8[SUBLANES - 1:SUBLANES, :], (SUBLANES, C))

    hc_ref[...] = lax.fori_loop(0, ts // SUBLANES, group, hc_ref[...])


OD_COLS = 3200
OD_TAIL = (2048, OD_COLS)


def _odd_proj_kernel(h_ref, g_ref, w_ref, wg2_ref, bg_ref, q_ref, k_ref, v_ref, r_ref, gk_ref):
    xn = _rms(h_ref[...], g_ref[...]).astype(BF16)

    def seg(a, b):
        return _dot(xn, w_ref[:, a:b])

    q_ref[...] = seg(0, GLA_DK) * (GLA_DKH ** -0.5)
    k_ref[...] = seg(GLA_DK, 2 * GLA_DK)
    v_ref[...] = seg(2 * GLA_DK, OD_TAIL[0]).astype(BF16)
    tail = seg(*OD_TAIL)
    r_ref[...] = tail[:, GLA_GATE_RANK:GLA_GATE_RANK + GLA_DV]
    first = tail[:, :LANES]
    lane = lax.broadcasted_iota(jnp.int32, first.shape, 1)
    glr = jnp.where(lane < GLA_GATE_RANK, first, 0.0).astype(BF16)
    z = _dot(glr, wg2_ref[...]) + bg_ref[...]
    gk_ref[...] = (-_softplus(-z)) * (1.0 / GLA_TAU)


def _odd_proj(h, g_pre, w_in, w_g2, b_g):
    T, D = h.shape
    tm = min(PROJ_TM, T)
    row = lambda i: (i, 0)
    fixed = lambda i: (0, 0)
    outs = [(GLA_DK, F32), (GLA_DK, F32), (GLA_DV, BF16), (GLA_DV, F32), (GLA_DK, F32)]
    return pl.pallas_call(
        _odd_proj_kernel,
        grid=(T // tm,),
        in_specs=[
            pl.BlockSpec((tm, D), row),
            pl.BlockSpec((1, D), fixed),
            _resident(w_in),
            pl.BlockSpec(w_g2.shape, fixed),
            pl.BlockSpec((1, GLA_DK), fixed),
        ],
        out_specs=[pl.BlockSpec((tm, n), row) for n, _ in outs],
        out_shape=[jax.ShapeDtypeStruct((T, n), dt) for n, dt in outs],
        compiler_params=_cparams(("parallel",)),
        name="odd_proj",
    )(h, g_pre, w_in, w_g2, b_g)


GLA_TS = 256


def _gla_kernel(q_ref, k_ref, v_ref, gk_ref, r_ref, hn_ref, tri_ref, o_ref,
                st_ref, qd_ref, oi_ref, u_ref, stb_ref):
    ts = q_ref.shape[1]
    C = GLA_CHUNK
    nch = ts // C

    @pl.when(pl.program_id(1) == 0)
    def _():
        st_ref[...] = jnp.zeros(st_ref.shape, F32)

    gk = gk_ref[0]
    g_hi = gk.astype(BF16)
    rem = gk - g_hi.astype(F32)
    g_mid = rem.astype(BF16)
    g_lo = (rem - g_mid.astype(F32)).astype(BF16)
    tri = tri_ref[...]
    G = _dot(tri, g_hi) + _dot(tri, g_mid) + _dot(tri, g_lo)

    kf = k_ref[0]
    qd_ref[...] = (q_ref[0] * jnp.exp(G)).astype(BF16)
    k_inv = (kf * jnp.exp(-G)).astype(BF16)
    g_last = [G[(c + 1) * C - 1:(c + 1) * C, :] for c in range(nch)]
    k_rem = jnp.concatenate(
        [kf[c * C:(c + 1) * C, :] * jnp.exp(g_last[c] - G[c * C:(c + 1) * C, :]) for c in range(nch)],
        axis=0).astype(BF16)

    ri = lax.broadcasted_iota(jnp.int32, (ts, ts), 0)
    ci = lax.broadcasted_iota(jnp.int32, (ts, ts), 1)
    same_chunk_causal = (ri >= ci) & (ri // C == ci // C)
    for h in range(GLA_HEADS):
        ksl = slice(h * GLA_DKH, (h + 1) * GLA_DKH)
        vsl = slice(h * GLA_DVH, (h + 1) * GLA_DVH)
        att = jnp.where(same_chunk_causal, _dot_nt(qd_ref[:, ksl], k_inv[:, ksl]), 0.0).astype(BF16)
        oi_ref[:, vsl] = _dot(att, v_ref[0, :, vsl])

    for c in range(nch):
        rows = slice(c * C, (c + 1) * C)
        for h in range(GLA_HEADS):
            ksl = slice(h * GLA_DKH, (h + 1) * GLA_DKH)
            vsl = slice(h * GLA_DVH, (h + 1) * GLA_DVH)
            u_ref[c, h] = _dot_tn(v_ref[0, rows, vsl], k_rem[rows, ksl])

    for h in range(GLA_HEADS):
        ksl = slice(h * GLA_DKH, (h + 1) * GLA_DKH)
        st = st_ref[h]
        for c in range(nch):
            stb_ref[c, h] = st.astype(BF16)
            st = st * jnp.exp(g_last[c][:, ksl]) + u_ref[c, h]
        st_ref[h] = st

    for c in range(nch):
        rows = slice(c * C, (c + 1) * C)
        for h in range(GLA_HEADS):
            ksl = slice(h * GLA_DKH, (h + 1) * GLA_DKH)
            vsl = slice(h * GLA_DVH, (h + 1) * GLA_DVH)
            o = oi_ref[rows, vsl] + _dot_nt(qd_ref[rows, ksl], stb_ref[c, h])
            on = _rms(o, hn_ref[...])
            rr = r_ref[0, rows, vsl]
            o_ref[0, rows, vsl] = (on * (rr * _sigmoid(rr))).astype(o_ref.dtype)


def _gla(q, k, v, gk, r, head_norm):
    B, S, _ = q.shape
    ts = min(GLA_TS, S)
    nch = ts // GLA_CHUNK
    pos = jnp.arange(ts)
    tri = ((pos[:, None] >= pos[None, :])
           & (pos[:, None] // GLA_CHUNK == pos[None, :] // GLA_CHUNK)).astype(BF16)
    blk = lambda b, s: (b, s, 0)
    return pl.pallas_call(
        _gla_kernel,
        grid=(B, S // ts),
        in_specs=[
            pl.BlockSpec((1, ts, GLA_DK), blk),
            pl.BlockSpec((1, ts, GLA_DK), blk),
            pl.BlockSpec((1, ts, GLA_DV), blk),
            pl.BlockSpec((1, ts, GLA_DK), blk),
            pl.BlockSpec((1, ts, GLA_DV), blk),
            pl.BlockSpec((1, GLA_DVH), lambda b, s: (0, 0)),
            pl.BlockSpec((ts, ts), lambda b, s: (0, 0)),
        ],
        out_specs=pl.BlockSpec((1, ts, GLA_DV), blk),
        out_shape=jax.ShapeDtypeStruct((B, S, GLA_DV), BF16),
        scratch_shapes=[
            pltpu.VMEM((GLA_HEADS, GLA_DVH, GLA_DKH), F32),
            pltpu.VMEM((ts, GLA_DK), BF16),
            pltpu.VMEM((ts, GLA_DV), F32),
            pltpu.VMEM((nch, GLA_HEADS, GLA_DVH, GLA_DKH), F32),
            pltpu.VMEM((nch, GLA_HEADS, GLA_DVH, GLA_DKH), BF16),
        ],
        compiler_params=_cparams(("parallel", "arbitrary")),
        name="gla",
    )(q, k, v, gk, r, head_norm, tri)


def _xa_kv_kernel(mem_ref, g_ref, w_ref, k_ref, v_ref):
    mn = _rms(mem_ref[...], g_ref[...]).astype(BF16)
    k_ref[...] = (_dot(mn, w_ref[:, :D_MODEL].astype(BF16)) * (XA_HEAD_DIM ** -0.5)).astype(BF16)
    v_ref[...] = _dot(mn, w_ref[:, D_MODEL:].astype(BF16)).astype(BF16)


def _xa_kv(mem, g_mem, w_kv, lead):
    B, M, D = mem.shape
    whole = lambda i: (0, 0)
    k, v = pl.pallas_call(
        _xa_kv_kernel,
        grid=(1,),
        in_specs=[pl.BlockSpec((B * M, D), whole), pl.BlockSpec((1, D), whole), _resident(w_kv, lead)],
        out_specs=[pl.BlockSpec((B * M, D), whole), pl.BlockSpec((B * M, D), whole)],
        out_shape=[jax.ShapeDtypeStruct((B * M, D), BF16)] * 2,
        compiler_params=_cparams(("arbitrary",)),
        name="xa_kv",
    )(mem.reshape(B * M, D), g_mem, w_kv)
    return k.reshape(B, M, D), v.reshape(B, M, D)


XA_TM = 1024


def _mix_out_xa_kernel(*refs, offsets):
    n = len(offsets)
    h_ref, gmix_ref, gpre_ref, gpost_ref = refs[:4]
    part_refs = refs[4:4 + n]
    wout_ref, wq_ref, k_ref, v_ref, wo_ref, o_ref = refs[4 + n:]
    m = None
    for p_ref, off in zip(part_refs, offsets):
        kk = p_ref.shape[-1]
        term = _dot(p_ref[0], wout_ref[off:off + kk, :])
        m = term if m is None else m + term
    x = h_ref[0] + _rms(m, gmix_ref[...])
    xn = _rms(x, gpre_ref[...]).astype(BF16)
    q = _dot(xn, wq_ref[...]).astype(BF16)
    heads = []
    for h in range(XA_HEADS):
        sl = slice(h * XA_HEAD_DIM, (h + 1) * XA_HEAD_DIM)
        s = _dot_nt(q[:, sl], k_ref[0, :, sl])
        p = jnp.exp(s - jnp.max(s, axis=-1, keepdims=True))
        oh = _dot(p.astype(BF16), v_ref[0, :, sl]) / jnp.sum(p, axis=-1, keepdims=True)
        heads.append(oh.astype(BF16))
    c = _dot(jnp.concatenate(heads, axis=-1), wo_ref[...])
    o_ref[0] = x + _rms(c, gpost_ref[...])


def _mix_out_xa(h, g_mix, g_pre, g_post, parts, w_out, w_q, kx, vx, w_o):
    B, S, D = h.shape
    M = kx.shape[1]
    tm = min(XA_TM, S)
    blk = lambda b, i: (b, i, 0)
    fixed = lambda b, i: (0, 0)
    offsets, off = [], 0
    for p in parts:
        offsets.append(off)
        off += p.shape[-1]
    return pl.pallas_call(
        functools.partial(_mix_out_xa_kernel, offsets=tuple(offsets)),
        grid=(B, S // tm),
        in_specs=[
            pl.BlockSpec((1, tm, D), blk),
            pl.BlockSpec((1, D), fixed),
            pl.BlockSpec((1, D), fixed),
            pl.BlockSpec((1, D), fixed),
            *[pl.BlockSpec((1, tm, p.shape[-1]), blk) for p in parts],
            _resident(w_out),
            _resident(w_q),
            pl.BlockSpec((1, M, D), lambda b, i: (b, 0, 0)),
            pl.BlockSpec((1, M, D), lambda b, i: (b, 0, 0)),
            _resident(w_o),
        ],
        out_specs=pl.BlockSpec((1, tm, D), blk),
        out_shape=jax.ShapeDtypeStruct((B, S, D), F32),
        compiler_params=_cparams(("parallel", "parallel")),
        name="mix_out_xa",
    )(h, g_mix, g_pre, g_post, *parts, w_out, w_q, kx, vx, w_o)


def _block_diag(w):
    G, n, _ = w.shape
    eye = jnp.eye(G, dtype=w.dtype)
    return (eye[:, None, :, None] * w[:, :, None, :]).reshape(G * n, G * n)


def _even_w_in(w):
    ki = w[:, 1280:1344]
    pad = jnp.zeros((w.shape[0], LANES - IDX_HEADS), w.dtype)
    return jnp.concatenate([w[:, :1280], ki, ki, w[:, 1352:2376], w[:, 1344:1352], pad], axis=1).astype(BF16)


def _odd_w_in(w):
    return jnp.pad(w.astype(BF16), ((0, 0), (0, OD_COLS - w.shape[1])))


def kernel(x, mem, norms, ffn_w_gu, ffn_w_down, xa_w_q, xa_w_kv, xa_w_o, ev_w_in, ev_kv_norm, ev_w_uk, ev_w_uv, ev_conv_w, ev_conv_b, ev_w_ra, ev_b_ra, ev_w_ri, ev_b_ri, ev_lam, ev_w_out, od_w_in, od_w_g2, od_b_g, od_head_norm, od_w_out):
    B, S, D = x.shape
    T = B * S
    depth = norms.shape[0]
    h = x.reshape(T, D)

    def gain(layer, idx):
        return norms[layer, idx][None, :]

    ffn_order = [(layer, j) for layer in range(depth) for j in range(2)]
    ffn_w = [ffn_w_gu[0, 0].astype(BF16), ffn_w_down[0, 0].astype(BF16)]

    def ffn(h, ffn_w, layer, j, g_pre, g_post, extra=()):
        k = ffn_order.index((layer, j))
        casts = list(extra)
        if k + 1 < len(ffn_order):
            casts = [(ffn_w_gu, ffn_order[k + 1]), (ffn_w_down, ffn_order[k + 1])] + casts
        h, cast = _ffn(h, g_pre, g_post, *ffn_w, side_casts=casts)
        n_next = len(casts) - len(extra)
        return h, cast[:n_next], cast[n_next:]

    for layer in range(depth):
        mix_out = (ev_w_out, (layer // 2,)) if layer % 2 == 0 else (od_w_out, (layer // 2,))
        h, ffn_w, (w_out, w_q, w_o) = ffn(
            h, ffn_w, layer, 0, gain(layer, N_FFN1_PRE), gain(layer, N_FFN1_POST),
            extra=[mix_out, (xa_w_q, (layer,)), (xa_w_o, (layer,))])

        if layer % 2 == 0:
            e = layer // 2
            lru_params = (ev_conv_w[e], ev_conv_b[e][None, :],
                          _block_diag(ev_w_ra[e]).astype(BF16), ev_b_ra[e].reshape(1, B_WIDTH),
                          _block_diag(ev_w_ri[e]).astype(BF16), ev_b_ri[e].reshape(1, B_WIDTH),
                          ev_lam[e][None, :])
            qt, k, vt, qit, ki2, wit, b_out = _even_proj(
                h, gain(layer, N_MIX_PRE), _even_w_in(ev_w_in[e]), ev_kv_norm[e][None, :],
                ev_w_uk[e].astype(BF16), ev_w_uv[e].astype(BF16), lru_params, S)
            a_out = _dsa(qt, k, vt, qit, ki2, wit, B)
            parts = [a_out.reshape(B, S, A_WIDTH), b_out.reshape(B, S, B_WIDTH)]
        else:
            o = layer // 2
            w_g2 = jnp.concatenate(
                [od_w_g2[o], jnp.zeros((LANES - GLA_GATE_RANK, GLA_DK), od_w_g2.dtype)], axis=0).astype(BF16)
            q, k, v, r, gk = _odd_proj(h, gain(layer, N_MIX_PRE), _odd_w_in(od_w_in[o]), w_g2,
                                       od_b_g[o][None, :])
            r3 = lambda a: a.reshape(B, S, a.shape[-1])
            g_out = _gla(r3(q), r3(k), r3(v), r3(gk), r3(r), od_head_norm[o][None, :])
            parts = [g_out]

        kx, vx = _xa_kv(mem, gain(layer, N_MEM_NORM), xa_w_kv, (layer,))
        h = _mix_out_xa(h.reshape(B, S, D), gain(layer, N_MIX_POST), gain(layer, N_XA_PRE),
                        gain(layer, N_XA_POST), parts, w_out, w_q, kx, vx, w_o).reshape(T, D)

        h, ffn_w, _ = ffn(h, ffn_w, layer, 1, gain(layer, N_FFN2_PRE), gain(layer, N_FFN2_POST))
    return h.reshape(B, S, D)
```
